```python
import jax, jax.numpy as jnp
from jax import lax
import numpy as np

D_MODEL = 1024
BATCH = 8
SEQ = 2048
DEPTH = 2
DEC_BATCH = 128
DEC_SEQ = 8
PAST_LEN = 16384
PAGE_SIZE = 128

N_A_LAYERS = DEPTH // 2
N_B_LAYERS = DEPTH - N_A_LAYERS
D_RNN = D_MODEL
N_RNN_BLOCKS = 8
RNN_BLOCK = D_RNN // N_RNN_BLOCKS
CONV_WIDTH = 4
RG_C = 8.0
HEAD_DIM = 64
N_HEADS = D_MODEL // HEAD_DIM
N_KV_HEADS = 4
GROUP = N_HEADS // N_KV_HEADS
WINDOW = 128
ROT_DIM = HEAD_DIM // 4
ROPE_THETA = 500000.0
D_FF = ((8 * D_MODEL + 3 * 256 - 1) // (3 * 256)) * 256
EPS = 1e-6
NEG_INF = -1e30

kernel_name = 'yoco_rglru_swa_sink_adaln_step'


def rms_norm(x, g):
    xf = x.astype(jnp.float32)
    y = xf * lax.rsqrt(jnp.mean(xf * xf, axis=-1, keepdims=True) + EPS)
    return (y * g.astype(jnp.float32)).astype(x.dtype)


def modulate(h, shift, scale):
    return h * (1 + scale[:, None, :]) + shift[:, None, :]


def ada_mod(c, w, b, n):
    return jnp.split(jax.nn.silu(c) @ w + b, n, axis=-1)


def swiglu(h, w_in, w_out):
    gate, up = jnp.split(h @ w_in, 2, axis=-1)
    return (jax.nn.silu(gate) * up) @ w_out


def apply_partial_rope(x, pos):
    inv = ROPE_THETA ** (-jnp.arange(0, ROT_DIM, 2, dtype=jnp.float32) / ROT_DIM)
    ang = pos.astype(jnp.float32)[:, None] * inv[None, :]
    cos = jnp.cos(ang)[None, :, None, :]
    sin = jnp.sin(ang)[None, :, None, :]
    xf = x.astype(jnp.float32)
    x1 = xf[..., :ROT_DIM // 2]
    x2 = xf[..., ROT_DIM // 2:ROT_DIM]
    out = jnp.concatenate([x1 * cos - x2 * sin, x2 * cos + x1 * sin, xf[..., ROT_DIM:]], axis=-1)
    return out.astype(x.dtype)


def causal_dwconv(x, buf, w, b):
    T = x.shape[1]
    xpad = jnp.concatenate([buf.astype(x.dtype), x], axis=1)
    y = b
    for k in range(CONV_WIDTH):
        y = y + w[k] * xpad[:, k:k + T]
    return y, xpad[:, xpad.shape[1] - (CONV_WIDTH - 1):]


def rg_lru(x, gate_w, gate_b, lam, h0):
    B, T, _ = x.shape
    xb = x.reshape(B, T, N_RNN_BLOCKS, RNN_BLOCK)
    g = jnp.einsum('btnc,ncd->btnd', xb, gate_w) + gate_b
    g = jax.nn.sigmoid(g.astype(jnp.float32))
    r = g[..., :RNN_BLOCK].reshape(B, T, D_RNN)
    i = g[..., RNN_BLOCK:].reshape(B, T, D_RNN)
    log_a = RG_C * r * jax.nn.log_sigmoid(lam.astype(jnp.float32))
    a = jnp.exp(log_a)
    u = jnp.sqrt(-jnp.expm1(2.0 * log_a)) * (i * x.astype(jnp.float32))

    def step(h, au):
        h = au[0] * h + au[1]
        return h, h

    h_last, hs = lax.scan(step, h0.astype(jnp.float32),
                          (jnp.swapaxes(a, 0, 1), jnp.swapaxes(u, 0, 1)))
    return jnp.swapaxes(hs, 0, 1).astype(x.dtype), h_last


def recurrent_block(h, w_in, conv_w, conv_b, gate_w, gate_b, lam, w_out, conv_buf, h0):
    xr, yg = jnp.split(h @ w_in, 2, axis=-1)
    xr, new_buf = causal_dwconv(xr, conv_buf, conv_w, conv_b)
    o, h_last = rg_lru(xr, gate_w, gate_b, lam, h0)
    return (o * jax.nn.gelu(yg, approximate=True)) @ w_out, new_buf, h_last


def sliding_sink_attention(q, k_all, v_all, sinks, p0):
    B, T = q.shape[:2]
    qb = WINDOW if T % WINDOW == 0 else T
    nb = T // qb
    span = qb + WINDOW
    idx = (jnp.arange(nb) * qb)[:, None] + jnp.arange(span)[None, :]
    kb = k_all[:, idx]
    vb = v_all[:, idx]
    qg = q.reshape(B, nb, qb, N_KV_HEADS, GROUP, HEAD_DIM)
    s = jnp.einsum('bnqkgd,bnskd->bnkgqs', qg, kb).astype(jnp.float32) * (HEAD_DIM ** -0.5)
    rel = WINDOW + jnp.arange(qb)[:, None] - jnp.arange(span)[None, :]
    mask = ((rel >= 0) & (rel <= WINDOW))[None] & ((p0 - WINDOW + idx) >= 0)[:, None, :]
    s = jnp.where(mask[None, :, None, None], s, NEG_INF)
    sink = sinks.astype(jnp.float32).reshape(1, 1, N_KV_HEADS, GROUP, 1, 1)
    m = jnp.maximum(jnp.max(s, axis=-1, keepdims=True), sink)
    e = jnp.exp(s - m)
    p = e / (jnp.sum(e, axis=-1, keepdims=True) + jnp.exp(sink - m))
    o = jnp.einsum('bnkgqs,bnskd->bnqkgd', p.astype(vb.dtype), vb)
    return o.reshape(B, T, N_HEADS * HEAD_DIM)


def trunk(x, c, p0, conv_bufs, h0s, k_prev, v_prev, params):
    (ada_w, ada_b, norm_g, rnn_w_in, rnn_conv_w, rnn_conv_b, rnn_gate_w, rnn_gate_b,
     rnn_lambda, rnn_w_out, kv_ada_w, kv_ada_b, kv_norm_g, w_kv, attn_w_q, attn_sinks,
     attn_w_o, ffn_w_in, ffn_w_out, final_g) = params
    B, T, _ = x.shape
    pos = p0 + jnp.arange(T, dtype=jnp.int32)
    new_conv, new_h = [], []
    k_all = v_all = None
    for l in range(DEPTH):
        sh1, sc1, g1, sh2, sc2, g2 = ada_mod(c, ada_w[l], ada_b[l], 6)
        h = modulate(rms_norm(x, norm_g[l, 0]), sh1, sc1)
        if l < N_A_LAYERS:
            out, nbuf, h_last = recurrent_block(h, rnn_w_in[l], rnn_conv_w[l], rnn_conv_b[l],
                                                rnn_gate_w[l], rnn_gate_b[l], rnn_lambda[l],
                                                rnn_w_out[l], conv_bufs[l], h0s[l])
            new_conv.append(nbuf)
            new_h.append(h_last)
        else:
            j = l - N_A_LAYERS
            q = apply_partial_rope((h @ attn_w_q[j]).reshape(B, T, N_HEADS, HEAD_DIM), pos)
            out = sliding_sink_attention(q, k_all, v_all, attn_sinks[j], p0) @ attn_w_o[j]
        x = x + g1[:, None, :] * out
        h = modulate(rms_norm(x, norm_g[l, 1]), sh2, sc2)
        x = x + g2[:, None, :] * swiglu(h, ffn_w_in[l], ffn_w_out[l])
        if l == N_A_LAYERS - 1:
            kv_shift, kv_scale = ada_mod(c, kv_ada_w, kv_ada_b, 2)
            hk = modulate(rms_norm(x, kv_norm_g), kv_shift, kv_scale)
            kv = (hk @ w_kv).reshape(B, T, 2, N_KV_HEADS, HEAD_DIM)
            k_new = apply_partial_rope(kv[:, :, 0], pos)
            k_all = jnp.concatenate([k_prev.astype(x.dtype), k_new], axis=1)
            v_all = jnp.concatenate([v_prev.astype(x.dtype), kv[:, :, 1]], axis=1)
    y = rms_norm(x, final_g)
    L = k_all.shape[1]
    return y, jnp.stack(new_conv), jnp.stack(new_h), k_all[:, L - WINDOW:], v_all[:, L - WINDOW:]


def setup_inputs(seed: int = 0) -> dict:
    key = jax.random.key(seed)
    ks = iter(jax.random.split(key, 32))
    f32 = jnp.float32
    D = D_MODEL

    def nrm(shape, scale):
        return jax.random.normal(next(ks), shape, f32) * scale

    u = jax.random.uniform(next(ks), (N_A_LAYERS, D_RNN), f32, 0.9, 0.999)
    a0 = u ** (1.0 / RG_C)
    rnn_lambda = jnp.log(a0) - jnp.log1p(-a0)
    return {
        'x_prompt': nrm((BATCH, SEQ, D), 1.0),
        'x_sample': nrm((DEC_BATCH, DEC_SEQ, D), 1.0),
        'c_prompt': nrm((BATCH, D), 1.0),
        'c_sample': nrm((DEC_BATCH, D), 1.0),
        'state_conv': nrm((N_A_LAYERS, DEC_BATCH, CONV_WIDTH - 1, D_RNN), 1.0),
        'state_h': nrm((N_A_LAYERS, DEC_BATCH, D_RNN), 0.5),
        'cache_k': nrm((DEC_BATCH, WINDOW, N_KV_HEADS, HEAD_DIM), 1.0),
        'cache_v': nrm((DEC_BATCH, WINDOW, N_KV_HEADS, HEAD_DIM), 1.0),
        'ada_w': nrm((DEPTH, D, 6 * D), 0.5 * D ** -0.5),
        'ada_b': nrm((DEPTH, 6 * D), 0.05),
        'norm_g': 1.0 + nrm((DEPTH, 2, D), 0.02),
        'rnn_w_in': nrm((N_A_LAYERS, D, 2 * D_RNN), D ** -0.5),
        'rnn_conv_w': nrm((N_A_LAYERS, CONV_WIDTH, D_RNN), CONV_WIDTH ** -0.5),
        'rnn_conv_b': nrm((N_A_LAYERS, D_RNN), 0.02),
        'rnn_gate_w': nrm((N_A_LAYERS, N_RNN_BLOCKS, RNN_BLOCK, 2 * RNN_BLOCK), RNN_BLOCK ** -0.5),
        'rnn_gate_b': nrm((N_A_LAYERS, N_RNN_BLOCKS, 2 * RNN_BLOCK), 0.1),
        'rnn_lambda': rnn_lambda,
        'rnn_w_out': nrm((N_A_LAYERS, D_RNN, D), D_RNN ** -0.5),
        'kv_ada_w': nrm((D, 2 * D), 0.5 * D ** -0.5),
        'kv_ada_b': nrm((2 * D,), 0.05),
        'kv_norm_g': 1.0 + nrm((D,), 0.02),
        'w_kv': nrm((D, 2 * N_KV_HEADS * HEAD_DIM), D ** -0.5),
        'attn_w_q': nrm((N_B_LAYERS, D, N_HEADS * HEAD_DIM), D ** -0.5),
        'attn_sinks': nrm((N_B_LAYERS, N_HEADS), 1.0),
        'attn_w_o': nrm((N_B_LAYERS, N_HEADS * HEAD_DIM, D), (N_HEADS * HEAD_DIM) ** -0.5),
        'ffn_w_in': nrm((DEPTH, D, 2 * D_FF), D ** -0.5),
        'ffn_w_out': nrm((DEPTH, D_FF, D), D_FF ** -0.5),
        'final_g': 1.0 + nrm((D,), 0.02),
    }


def reference(x_prompt, x_sample, c_prompt, c_sample, state_conv, state_h, cache_k, cache_v,
              ada_w, ada_b, norm_g, rnn_w_in, rnn_conv_w, rnn_conv_b, rnn_gate_w, rnn_gate_b,
              rnn_lambda, rnn_w_out, kv_ada_w, kv_ada_b, kv_norm_g, w_kv, attn_w_q, attn_sinks,
              attn_w_o, ffn_w_in, ffn_w_out, final_g):
    params = (ada_w, ada_b, norm_g, rnn_w_in, rnn_conv_w, rnn_conv_b, rnn_gate_w, rnn_gate_b,
              rnn_lambda, rnn_w_out, kv_ada_w, kv_ada_b, kv_norm_g, w_kv, attn_w_q, attn_sinks,
              attn_w_o, ffn_w_in, ffn_w_out, final_g)
    B = x_prompt.shape[0]
    conv0 = jnp.zeros((N_A_LAYERS, B, CONV_WIDTH - 1, D_RNN), x_prompt.dtype)
    h00 = jnp.zeros((N_A_LAYERS, B, D_RNN), jnp.float32)
    kv0 = jnp.zeros((B, WINDOW, N_KV_HEADS, HEAD_DIM), x_prompt.dtype)
    y_prompt, prompt_conv, prompt_h, prompt_k, prompt_v = trunk(
        x_prompt, c_prompt, 0, conv0, h00, kv0, kv0, params)
    y_sample, sample_conv, sample_h, sample_k, sample_v = trunk(
        x_sample, c_sample, PAST_LEN, state_conv, state_h, cache_k, cache_v, params)
    return (y_prompt, y_sample, prompt_conv, prompt_h, prompt_k, prompt_v,
            sample_conv, sample_h, sample_k, sample_v)
```

```python
import functools

import jax
import jax.numpy as jnp
from jax import lax
from jax.experimental import pallas as pl
from jax.experimental.pallas import tpu as pltpu

F32 = jnp.float32
BF16 = jnp.bfloat16

D_MODEL = 1024
N_RNN_BLOCKS = 8
RNN_BLOCK = D_MODEL // N_RNN_BLOCKS
CONV_WIDTH = 4
RG_C = 8.0
HEAD_DIM = 64
N_HEADS = D_MODEL // HEAD_DIM
N_KV_HEADS = 4
GROUP = N_HEADS // N_KV_HEADS
KV_DIM = N_KV_HEADS * HEAD_DIM
WINDOW = 128
ROT_DIM = HEAD_DIM // 4
ROPE_THETA = 500000.0
EPS = 1e-6
NEG_INF = -1e30
PAST_LEN = 16384

LANES = 128
SUBLANES = 8
MXU_COLS = 256
VMEM_LIMIT = 56 * 1024 * 1024

PROMPT_TILE = 256
ADA_TILE_N = 2048
SAMPLE_ATTN_BATCH = 16
SAMPLE_BATCH_TILE = 64


def _dot(a, b):
    return jnp.dot(a, b, preferred_element_type=F32)


def _dot_nt(a, b):
    return lax.dot_general(a, b, (((1,), (1,)), ((), ())), preferred_element_type=F32)


def _sigmoid(x):
    return 1.0 / (1.0 + jnp.exp(-x))


def _silu(x):
    return x * _sigmoid(x)


def _gelu_tanh(x):
    return x * (0.5 * (1.0 + jnp.tanh(0.7978845608028654 * (x + 0.044715 * (x * x * x)))))


def _log_sigmoid(x):
    return -(jnp.maximum(-x, 0.0) + jnp.log1p(jnp.exp(-jnp.abs(x))))


def _rms(x):
    return x * lax.rsqrt(jnp.mean(x * x, axis=-1, keepdims=True) + EPS)


def _rms_mod(x, g, scale, shift):
    return (_rms(x) * g) * (1.0 + scale) + shift


def _tile_rows(m, reps):
    return jnp.concatenate([m] * reps, axis=0)


def _rope_block(blk, c, s_next, s_prev):
    return blk * c + pltpu.roll(blk, LANES - ROT_DIM // 2, 1) * s_next + pltpu.roll(blk, ROT_DIM // 2, 1) * s_prev


def _rope(x, c, s_next, s_prev):
    blocks = [_rope_block(x[:, j * LANES:(j + 1) * LANES], c, s_next, s_prev)
              for j in range(x.shape[1] // LANES)]
    return jnp.concatenate(blocks, axis=1)


def _proj_in(h, w_in_ref, xr_s, yg_s):
    d = xr_s.shape[1]
    cw = 2 * MXU_COLS
    for c in range(d // cw):
        xr_s[:, c * cw:(c + 1) * cw] = _dot(h, w_in_ref[:, c * cw:(c + 1) * cw])
        yg_s[:, c * cw:(c + 1) * cw] = _dot(h, w_in_ref[:, d + c * cw:d + (c + 1) * cw])


def _rglru_gates(xc, gw_ref, gb_ref, lam_ref, a_s, u_s):
    xcb = xc.astype(BF16)
    cl = RG_C * _log_sigmoid(lam_ref[...])
    for n in range(N_RNN_BLOCKS):
        blk = slice(n * RNN_BLOCK, (n + 1) * RNN_BLOCK)
        g = _dot(xcb[:, blk], gw_ref[n]) + gb_ref[n:n + 1, :]
        r = _sigmoid(g[:, :RNN_BLOCK])
        i = _sigmoid(g[:, RNN_BLOCK:])
        log_a = cl[:, blk] * r
        a = jnp.exp(log_a)
        a_s[:, blk] = a
        u_s[:, blk] = jnp.sqrt(-jnp.tanh(log_a) * (a * a + 1.0)) * (i * xc[:, blk])


def _swiglu(h, fin_ref, fout_ref, act_s):
    d_ff = fout_ref.shape[0]
    for c in range(d_ff // MXU_COLS):
        cols = slice(c * MXU_COLS, (c + 1) * MXU_COLS)
        gate = _dot(h, fin_ref[:, cols])
        up = _dot(h, fin_ref[:, d_ff + c * MXU_COLS:d_ff + (c + 1) * MXU_COLS])
        act_s[:, cols] = (_silu(gate) * up).astype(BF16)
    return _dot(act_s[...], fout_ref[...])


def _softmax_sink(s, mask, sink):
    s = jnp.where(mask, s, NEG_INF)
    mx = jnp.maximum(jnp.max(s, axis=-1, keepdims=True), sink)
    e = jnp.exp(s - mx)
    den = jnp.sum(e, axis=-1, keepdims=True) + jnp.exp(sink - mx)
    return e, 1.0 / den


def _ada_kernel(c_ref, w_ref, b_ref, o_ref):
    c = c_ref[...]
    o_ref[0] = _dot(_silu(c).astype(BF16), w_ref[0].astype(BF16)) + b_ref[0]


def _ada_call(c, w, b):
    n_layers, d, n = w.shape
    r = c.shape[0]
    return pl.pallas_call(
        _ada_kernel,
        grid=(n_layers, n // ADA_TILE_N),
        in_specs=[pl.BlockSpec((r, d), lambda l, j: (0, 0)),
                  pl.BlockSpec((1, d, ADA_TILE_N), lambda l, j: (l, 0, j)),
                  pl.BlockSpec((1, 1, ADA_TILE_N), lambda l, j: (l, 0, j))],
        out_specs=pl.BlockSpec((1, r, ADA_TILE_N), lambda l, j: (l, 0, j)),
        out_shape=jax.ShapeDtypeStruct((n_layers, r, n), F32),
        compiler_params=pltpu.CompilerParams(
            dimension_semantics=("arbitrary", "arbitrary"), vmem_limit_bytes=VMEM_LIMIT),
        name="ada_mod",
    )(c, w, b.reshape(n_layers, 1, n))


def _prompt_l0_kernel(x_ref, mod_ref, rope_ref, ng_ref, w_in_ref, cw_ref, cb_ref, gw_ref, gb_ref,
                      lam_ref, w_out_ref, fin_ref, fout_ref, kvg_ref, wkv_ref,
                      x2_ref, k_ref, v_ref, conv_ref, hl_ref,
                      xr_s, yg_s, a_s, u_s, o_s, act_s, hist_s, hc_s):
    tm, d = x_ref.shape

    @pl.when(pl.program_id(1) == 0)
    def _():
        hist_s[...] = jnp.zeros_like(hist_s)
        hc_s[...] = jnp.zeros_like(hc_s)

    x = x_ref[...]
    h = _rms_mod(x, ng_ref[0:1, :], mod_ref[1], mod_ref[0]).astype(BF16)
    _proj_in(h, w_in_ref, xr_s, yg_s)

    xr = xr_s[...]
    hist = hist_s[...]
    row8 = lax.broadcasted_iota(jnp.int32, (SUBLANES, d), 0)

    def shifted(k):
        rolled = pltpu.roll(xr, k, 0)
        first = jnp.where(row8 >= k, rolled[0:SUBLANES], pltpu.roll(hist, k, 0))
        return jnp.concatenate([first, rolled[SUBLANES:]], axis=0)

    xc = cb_ref[...]
    for j in range(CONV_WIDTH - 1):
        xc = xc + cw_ref[j:j + 1, :] * shifted(CONV_WIDTH - 1 - j)
    xc = xc + cw_ref[CONV_WIDTH - 1:CONV_WIDTH, :] * xr
    hist_s[...] = xr[tm - SUBLANES:]
    conv_ref[...] = xr[tm - SUBLANES:]

    _rglru_gates(xc, gw_ref, gb_ref, lam_ref, a_s, u_s)

    def scan_group(g, hprev):
        r0 = pl.multiple_of(g * SUBLANES, SUBLANES)
        a = a_s[pl.ds(r0, SUBLANES), :]
        u = u_s[pl.ds(r0, SUBLANES), :]
        for step in (1, 2, 4):
            keep = row8 >= step
            a_sh = jnp.where(keep, pltpu.roll(a, step, 0), 1.0)
            u_sh = jnp.where(keep, pltpu.roll(u, step, 0), 0.0)
            u = a * u_sh + u
            a = a * a_sh
        hs = a * hprev + u
        o_s[pl.ds(r0, SUBLANES), :] = hs
        return jnp.broadcast_to(hs[SUBLANES - 1:SUBLANES, :], hs.shape)

    h_fin = lax.fori_loop(0, tm // SUBLANES, scan_group, hc_s[...], unroll=4)
    hc_s[...] = h_fin
    hl_ref[...] = h_fin

    z = (o_s[...] * _gelu_tanh(yg_s[...])).astype(BF16)
    x1 = x + mod_ref[2] * _dot(z, w_out_ref[...])

    h2 = _rms_mod(x1, ng_ref[1:2, :], mod_ref[4], mod_ref[3]).astype(BF16)
    x2 = x1 + mod_ref[5] * _swiglu(h2, fin_ref, fout_ref, act_s)
    x2_ref[...] = x2

    hk = _rms_mod(x2, kvg_ref[...], mod_ref[7], mod_ref[6]).astype(BF16)
    kv = _dot(hk, wkv_ref[...])
    k_ref[...] = _rope(kv[:, :KV_DIM], rope_ref[0], rope_ref[1], rope_ref[2])
    v_ref[...] = kv[:, KV_DIM:]


def _const_spec(shape):
    zeros = (0,) * len(shape)
    return pl.BlockSpec(shape, lambda *_: zeros, pipeline_mode=pl.Buffered(1))


def _prompt_l0_call(x, mod, rope, ng, w_in, cw, cb, gw, gb, lam, w_out, fin, fout, kvg, wkv):
    nb, t, d = x.shape
    tm = PROMPT_TILE
    nt = t // tm
    d_ff = fout.shape[0]
    row_spec = lambda w: pl.BlockSpec((None, tm, w), lambda b, i: (b, i, 0))
    state_spec = pl.BlockSpec((None, SUBLANES, d), lambda b, i: (b, 0, 0))
    consts = (ng, w_in, cw, cb, gw, gb, lam, w_out, fin, fout, kvg, wkv)
    return pl.pallas_call(
        _prompt_l0_kernel,
        grid=(nb, nt),
        in_specs=[row_spec(d),
                  pl.BlockSpec((None,) + mod.shape[1:], lambda b, i: (b, 0, 0, 0)),
                  pl.BlockSpec((3, tm, LANES), lambda b, i: (0, i, 0))]
                 + [_const_spec(c.shape) for c in consts],
        out_specs=[row_spec(d), row_spec(KV_DIM), row_spec(KV_DIM), state_spec, state_spec],
        out_shape=[jax.ShapeDtypeStruct((nb, t, d), F32),
                   jax.ShapeDtypeStruct((nb, t, KV_DIM), F32),
                   jax.ShapeDtypeStruct((nb, t, KV_DIM), F32),
                   jax.ShapeDtypeStruct((nb, SUBLANES, d), F32),
                   jax.ShapeDtypeStruct((nb, SUBLANES, d), F32)],
        scratch_shapes=[pltpu.VMEM((tm, d), F32)] * 5
                       + [pltpu.VMEM((tm, d_ff), BF16),
                          pltpu.VMEM((SUBLANES, d), F32), pltpu.VMEM((SUBLANES, d), F32)],
        compiler_params=pltpu.CompilerParams(
            dimension_semantics=("arbitrary", "arbitrary"), vmem_limit_bytes=VMEM_LIMIT),
        name="prompt_layer0",
    )(x, mod, rope, *consts)


def _prompt_l1_kernel(sink_ref, x_ref, k_ref, v_ref, mod_ref, rope_ref, ng_ref, wq_ref, wo_ref,
                      fin_ref, fout_ref, fg_ref,
                      y_ref,
                      kw_s, vw_s, attn_s, act_s):
    tm, d = x_ref.shape
    t = pl.program_id(1)

    @pl.when(t == 0)
    def _():
        kw_s[0:WINDOW, :] = jnp.zeros((WINDOW, KV_DIM), BF16)
        vw_s[0:WINDOW, :] = jnp.zeros((WINDOW, KV_DIM), BF16)

    @pl.when(t > 0)
    def _():
        kw_s[0:WINDOW, :] = kw_s[tm:tm + WINDOW, :]
        vw_s[0:WINDOW, :] = vw_s[tm:tm + WINDOW, :]

    kw_s[WINDOW:WINDOW + tm, :] = k_ref[...].astype(BF16)
    vw_s[WINDOW:WINDOW + tm, :] = v_ref[...].astype(BF16)

    x = x_ref[...]
    h = _rms_mod(x, ng_ref[0:1, :], mod_ref[1], mod_ref[0]).astype(BF16)
    q = _rope(_dot(h, wq_ref[...]), rope_ref[0], rope_ref[1], rope_ref[2])
    q = (q * (HEAD_DIM ** -0.5)).astype(BF16)

    span = 2 * WINDOW
    qi = lax.broadcasted_iota(jnp.int32, (WINDOW, span), 0)
    si = lax.broadcasted_iota(jnp.int32, (WINDOW, span), 1)
    band = (si >= qi) & (si <= qi + WINDOW)
    for j in range(tm // WINDOW):
        mask = band & (si >= WINDOW - (t * tm + j * WINDOW))
        rows = slice(j * WINDOW, (j + 1) * WINDOW)
        kwin = kw_s[j * WINDOW:j * WINDOW + span, :]
        vwin = vw_s[j * WINDOW:j * WINDOW + span, :]
        outs = []
        for hd in range(N_HEADS):
            kv_cols = slice((hd // GROUP) * HEAD_DIM, (hd // GROUP + 1) * HEAD_DIM)
            s = _dot_nt(q[rows, hd * HEAD_DIM:(hd + 1) * HEAD_DIM], kwin[:, kv_cols])
            e, inv = _softmax_sink(s, mask, sink_ref[hd])
            outs.append(_dot((e * inv).astype(BF16), vwin[:, kv_cols]))
        attn_s[rows, :] = jnp.concatenate(outs, axis=1).astype(BF16)

    x1 = x + mod_ref[2] * _dot(attn_s[...], wo_ref[...])
    h2 = _rms_mod(x1, ng_ref[1:2, :], mod_ref[4], mod_ref[3]).astype(BF16)
    x2 = x1 + mod_ref[5] * _swiglu(h2, fin_ref, fout_ref, act_s)
    y_ref[...] = _rms(x2) * fg_ref[...]


def _prompt_l1_call(sinks, x, k, v, mod, rope, ng, wq, wo, fin, fout, fg):
    nb, t, d = x.shape
    tm = PROMPT_TILE
    nt = t // tm
    d_ff = fout.shape[0]
    row_spec = lambda w: pl.BlockSpec((None, tm, w), lambda b, i: (b, i, 0))
    consts = (ng, wq, wo, fin, fout, fg)
    return pl.pallas_call(
        _prompt_l1_kernel,
        grid=(nb, nt),
        in_specs=[pl.BlockSpec(memory_space=pltpu.SMEM),
                  row_spec(d), row_spec(KV_DIM), row_spec(KV_DIM),
                  pl.BlockSpec((None,) + mod.shape[1:], lambda b, i: (b, 0, 0, 0)),
                  pl.BlockSpec((3, tm, LANES), lambda b, i: (0, i, 0))]
                 + [_const_spec(c.shape) for c in consts],
        out_specs=row_spec(d),
        out_shape=jax.ShapeDtypeStruct((nb, t, d), F32),
        scratch_shapes=[pltpu.VMEM((WINDOW + tm, KV_DIM), BF16),
                        pltpu.VMEM((WINDOW + tm, KV_DIM), BF16),
                        pltpu.VMEM((tm, d), BF16),
                        pltpu.VMEM((tm, d_ff), BF16)],
        compiler_params=pltpu.CompilerParams(
            dimension_semantics=("arbitrary", "arbitrary"), vmem_limit_bytes=VMEM_LIMIT),
        name="prompt_layer1",
    )(sinks, x, k, v, mod, rope, *consts)


def _sample_l0_kernel(x_ref, mod_ref, qmod_ref, h0_ref, cst_ref, rope_ref, ng_ref, w_in_ref, cw_ref,
                      cb_ref, gw_ref, gb_ref, lam_ref, w_out_ref, fin_ref, fout_ref, kvg_ref, wkv_ref,
                      qg_ref, wq_ref,
                      x2_ref, k_ref, v_ref, q_ref, conv_ref, hl_ref,
                      xr_s, yg_s, a_s, u_s, o_s, act_s):
    nt, sb, d = x_ref.shape
    rows = nt * sb
    slab = lambda t: slice(t * sb, (t + 1) * sb)
    mod = lambda i: _tile_rows(mod_ref[i], nt)

    x = x_ref[...].reshape(rows, d)
    h = _rms_mod(x, ng_ref[0:1, :], mod(1), mod(0)).astype(BF16)
    _proj_in(h, w_in_ref, xr_s, yg_s)

    def conv_in(j):
        return cst_ref[j] if j < CONV_WIDTH - 1 else xr_s[slab(j - (CONV_WIDTH - 1)), :]

    xc_slabs = []
    for t in range(nt):
        acc = cb_ref[...]
        for j in range(CONV_WIDTH):
            acc = acc + cw_ref[j:j + 1, :] * conv_in(t + j)
        xc_slabs.append(acc)
    xc = jnp.concatenate(xc_slabs, axis=0)
    for j in range(CONV_WIDTH - 1):
        conv_ref[j] = xr_s[slab(nt - (CONV_WIDTH - 1) + j), :]

    _rglru_gates(xc, gw_ref, gb_ref, lam_ref, a_s, u_s)

    hs = h0_ref[...]
    for t in range(nt):
        hs = a_s[slab(t), :] * hs + u_s[slab(t), :]
        o_s[slab(t), :] = hs
    hl_ref[...] = hs

    z = (o_s[...] * _gelu_tanh(yg_s[...])).astype(BF16)
    x1 = x + mod(2) * _dot(z, w_out_ref[...])

    h2 = _rms_mod(x1, ng_ref[1:2, :], mod(4), mod(3)).astype(BF16)
    x2 = x1 + mod(5) * _swiglu(h2, fin_ref, fout_ref, act_s)
    x2_ref[...] = x2.reshape(nt, sb, d)

    hk = _rms_mod(x2, kvg_ref[...], mod(7), mod(6)).astype(BF16)
    kv = _dot(hk, wkv_ref[...])
    hq = _rms_mod(x2, qg_ref[...], _tile_rows(qmod_ref[1], nt), _tile_rows(qmod_ref[0], nt)).astype(BF16)
    q = _dot(hq, wq_ref[...])
    for t in range(nt):
        c, s_next, s_prev = rope_ref[0, t:t + 1, :], rope_ref[1, t:t + 1, :], rope_ref[2, t:t + 1, :]
        k_ref[t] = _rope(kv[slab(t), :KV_DIM], c, s_next, s_prev)
        q_ref[t] = _rope(q[slab(t), :], c, s_next, s_prev) * (HEAD_DIM ** -0.5)
    v_ref[...] = kv[:, KV_DIM:].reshape(nt, sb, KV_DIM)


def _sample_l0_call(x, mod, qmod, h0, cst, rope, ng, w_in, cw, cb, gw, gb, lam, w_out, fin, fout,
                    kvg, wkv, qg, wq):
    nt, nb, d = x.shape
    sb = SAMPLE_BATCH_TILE
    d_ff = fout.shape[0]
    rows = nt * sb
    slab_spec = lambda lead, w: pl.BlockSpec((lead, sb, w), lambda i: (0, i, 0))
    consts = (rope, ng, w_in, cw, cb, gw, gb, lam, w_out, fin, fout, kvg, wkv, qg, wq)
    return pl.pallas_call(
        _sample_l0_kernel,
        grid=(nb // sb,),
        in_specs=[slab_spec(nt, d), slab_spec(mod.shape[0], d), slab_spec(qmod.shape[0], d),
                  pl.BlockSpec((sb, d), lambda i: (i, 0)), slab_spec(CONV_WIDTH - 1, d)]
                 + [_const_spec(c.shape) for c in consts],
        out_specs=[slab_spec(nt, d), slab_spec(nt, KV_DIM), slab_spec(nt, KV_DIM), slab_spec(nt, d),
                   slab_spec(CONV_WIDTH - 1, d), pl.BlockSpec((sb, d), lambda i: (i, 0))],
        out_shape=[jax.ShapeDtypeStruct((nt, nb, d), F32),
                   jax.ShapeDtypeStruct((nt, nb, KV_DIM), F32),
                   jax.ShapeDtypeStruct((nt, nb, KV_DIM), F32),
                   jax.ShapeDtypeStruct((nt, nb, d), F32),
                   jax.ShapeDtypeStruct((CONV_WIDTH - 1, nb, d), F32),
                   jax.ShapeDtypeStruct((nb, d), F32)],
        scratch_shapes=[pltpu.VMEM((rows, d), F32)] * 5 + [pltpu.VMEM((rows, d_ff), BF16)],
        compiler_params=pltpu.CompilerParams(
            dimension_semantics=("arbitrary",), vmem_limit_bytes=VMEM_LIMIT),
        name="sample_layer0",
    )(x, mod, qmod, h0, cst, *consts)


def _sample_attn_kernel(q_ref, kn_ref, vn_ref, ck_ref, cv_ref, sink_ref,
                        attn_ref, ko_ref, vo_ref,
                        kw_s, vw_s):
    sb, nt, d = q_ref.shape
    span = 2 * WINDOW
    pad_rows = 2 * SUBLANES
    kw_s[WINDOW + pad_rows:span, :] = jnp.zeros((span - WINDOW - pad_rows, KV_DIM), BF16)
    vw_s[WINDOW + pad_rows:span, :] = jnp.zeros((span - WINDOW - pad_rows, KV_DIM), BF16)

    tok = lax.broadcasted_iota(jnp.int32, (GROUP * nt, span), 0) & (nt - 1)
    si = lax.broadcasted_iota(jnp.int32, (GROUP * nt, span), 1)
    mask = (si >= tok) & (si <= tok + WINDOW)
    new_pad = jnp.zeros((pad_rows - nt, KV_DIM), F32)

    def one_sequence(b, carry):
        kn, vn = kn_ref[b], vn_ref[b]
        ck, cv = ck_ref[b], cv_ref[b]
        ko_ref[b, 0:WINDOW - nt, :] = ck[nt:, :]
        ko_ref[b, WINDOW - nt:WINDOW, :] = kn
        vo_ref[b, 0:WINDOW - nt, :] = cv[nt:, :]
        vo_ref[b, WINDOW - nt:WINDOW, :] = vn
        kw_s[0:WINDOW, :] = ck.astype(BF16)
        vw_s[0:WINDOW, :] = cv.astype(BF16)
        kw_s[WINDOW:WINDOW + pad_rows, :] = jnp.concatenate([kn, new_pad], axis=0).astype(BF16)
        vw_s[WINDOW:WINDOW + pad_rows, :] = jnp.concatenate([vn, new_pad], axis=0).astype(BF16)

        qb = q_ref[b]
        group_out = []
        for g in range(N_KV_HEADS):
            cols = slice(g * HEAD_DIM, (g + 1) * HEAD_DIM)
            qg = jnp.concatenate(
                [qb[:, (g * GROUP + i) * HEAD_DIM:(g * GROUP + i + 1) * HEAD_DIM] for i in range(GROUP)],
                axis=0).astype(BF16)
            s = _dot_nt(qg, kw_s[:, cols])
            e, inv = _softmax_sink(s, mask, sink_ref[g][:, 0:1])
            group_out.append(_dot((e * inv).astype(BF16), vw_s[:, cols]))
        attn_ref[b] = jnp.concatenate(
            [group_out[hd // GROUP][(hd % GROUP) * nt:(hd % GROUP + 1) * nt, :] for hd in range(N_HEADS)],
            axis=1)
        return carry

    lax.fori_loop(0, sb, one_sequence, 0)


def _sample_attn_call(q, kn, vn, ck, cv, sink_rows):
    nb, nt, d = q.shape
    sb = SAMPLE_ATTN_BATCH
    seq_spec = lambda r, w: pl.BlockSpec((sb, r, w), lambda i: (i, 0, 0))
    return pl.pallas_call(
        _sample_attn_kernel,
        grid=(nb // sb,),
        in_specs=[seq_spec(nt, d), seq_spec(nt, KV_DIM), seq_spec(nt, KV_DIM),
                  seq_spec(WINDOW, KV_DIM), seq_spec(WINDOW, KV_DIM),
                  _const_spec(sink_rows.shape)],
        out_specs=[seq_spec(nt, d), seq_spec(WINDOW, KV_DIM), seq_spec(WINDOW, KV_DIM)],
        out_shape=[jax.ShapeDtypeStruct((nb, nt, d), F32),
                   jax.ShapeDtypeStruct((nb, WINDOW, KV_DIM), F32),
                   jax.ShapeDtypeStruct((nb, WINDOW, KV_DIM), F32)],
        scratch_shapes=[pltpu.VMEM((2 * WINDOW, KV_DIM), BF16), pltpu.VMEM((2 * WINDOW, KV_DIM), BF16)],
        compiler_params=pltpu.CompilerParams(
            dimension_semantics=("arbitrary",), vmem_limit_bytes=VMEM_LIMIT),
        name="sample_attention",
    )(q, kn, vn, ck, cv, sink_rows)


def _sample_l1_kernel(x_ref, attn_ref, mod_ref, ng_ref, wo_ref, fin_ref, fout_ref, fg_ref,
                      y_ref, act_s):
    nt, sb, d = x_ref.shape
    rows = nt * sb
    mod = lambda i: _tile_rows(mod_ref[i], nt)
    x = x_ref[...].reshape(rows, d)
    attn = attn_ref[...].reshape(rows, d).astype(BF16)
    x1 = x + mod(2) * _dot(attn, wo_ref[...])
    h2 = _rms_mod(x1, ng_ref[1:2, :], mod(4), mod(3)).astype(BF16)
    x2 = x1 + mod(5) * _swiglu(h2, fin_ref, fout_ref, act_s)
    y_ref[...] = (_rms(x2) * fg_ref[...]).reshape(nt, sb, d)


def _sample_l1_call(x, attn, mod, ng, wo, fin, fout, fg):
    nt, nb, d = x.shape
    sb = SAMPLE_BATCH_TILE
    d_ff = fout.shape[0]
    slab_spec = lambda lead: pl.BlockSpec((lead, sb, d), lambda i: (0, i, 0))
    consts = (ng, wo, fin, fout, fg)
    return pl.pallas_call(
        _sample_l1_kernel,
        grid=(nb // sb,),
        in_specs=[slab_spec(nt), slab_spec(nt), slab_spec(mod.shape[0])]
                 + [_const_spec(c.shape) for c in consts],
        out_specs=slab_spec(nt),
        out_shape=jax.ShapeDtypeStruct((nt, nb, d), F32),
        scratch_shapes=[pltpu.VMEM((nt * sb, d_ff), BF16)],
        compiler_params=pltpu.CompilerParams(
            dimension_semantics=("arbitrary",), vmem_limit_bytes=VMEM_LIMIT),
        name="sample_layer1",
    )(x, attn, mod, *consts)


def _rope_tables(pos):
    half = ROT_DIM // 2
    inv = ROPE_THETA ** (-jnp.arange(0, ROT_DIM, 2, dtype=F32) / ROT_DIM)
    ang = pos.astype(F32)[:, None] * inv[None, :]
    cos, sin = jnp.cos(ang), jnp.sin(ang)
    n = pos.shape[0]
    rest = HEAD_DIM - ROT_DIM
    c = jnp.concatenate([cos, cos, jnp.ones((n, rest), F32)], axis=1)
    s_next = jnp.concatenate([-sin, jnp.zeros((n, half + rest), F32)], axis=1)
    s_prev = jnp.concatenate([jnp.zeros((n, half), F32), sin, jnp.zeros((n, rest), F32)], axis=1)
    reps = LANES // HEAD_DIM
    return jnp.stack([jnp.tile(c, (1, reps)), jnp.tile(s_next, (1, reps)), jnp.tile(s_prev, (1, reps))])


def kernel(x_prompt, x_sample, c_prompt, c_sample, state_conv, state_h, cache_k, cache_v, ada_w, ada_b, norm_g, rnn_w_in, rnn_conv_w, rnn_conv_b, rnn_gate_w, rnn_gate_b, rnn_lambda, rnn_w_out, kv_ada_w, kv_ada_b, kv_norm_g, w_kv, attn_w_q, attn_sinks, attn_w_o, ffn_w_in, ffn_w_out, final_g):
    nb_p, t_p, d = x_prompt.shape
    nb_s, t_s, _ = x_sample.shape

    c_all = jnp.concatenate([c_prompt, c_sample], axis=0)
    ada = _ada_call(c_all, ada_w, ada_b)
    kv_ada = _ada_call(c_all, kv_ada_w[None], kv_ada_b[None])
    n_all = c_all.shape[0]
    mod_l0 = jnp.concatenate([ada[0].reshape(n_all, 6, d), kv_ada[0].reshape(n_all, 2, d)], axis=1)
    mod_l1 = ada[1].reshape(n_all, 6, d)

    bf = lambda w: w.astype(BF16)
    row = lambda v: v.reshape(1, -1)
    l0_consts = (norm_g[0], bf(rnn_w_in[0]), rnn_conv_w[0], row(rnn_conv_b[0]), bf(rnn_gate_w[0]),
                 rnn_gate_b[0], row(rnn_lambda[0]), bf(rnn_w_out[0]), bf(ffn_w_in[0]), bf(ffn_w_out[0]),
                 row(kv_norm_g), bf(w_kv))
    wq, wo = bf(attn_w_q[0]), bf(attn_w_o[0])
    fin1, fout1 = bf(ffn_w_in[1]), bf(ffn_w_out[1])

    rope_p = _rope_tables(jnp.arange(t_p, dtype=jnp.int32))
    x2_p, k_p, v_p, conv_p, hl_p = _prompt_l0_call(
        x_prompt, mod_l0[:nb_p, :, None, :], rope_p, *l0_consts)
    y_prompt = _prompt_l1_call(attn_sinks[0], x2_p, k_p, v_p, mod_l1[:nb_p, :, None, :], rope_p,
                               norm_g[1], wq, wo, fin1, fout1, row(final_g))

    rope_s = _rope_tables(PAST_LEN + jnp.arange(t_s, dtype=jnp.int32))
    mod_s0 = mod_l0[nb_p:].transpose(1, 0, 2)
    mod_s1 = mod_l1[nb_p:].transpose(1, 0, 2)
    x2_s, k_s, v_s, q_s, conv_s, hl_s = _sample_l0_call(
        x_sample.transpose(1, 0, 2), mod_s0, mod_s1[0:2], state_h[0], state_conv[0].transpose(1, 0, 2),
        rope_s, *l0_consts, norm_g[1, 0:1], wq)
    sink_rows = jnp.broadcast_to(
        jnp.repeat(attn_sinks[0].reshape(N_KV_HEADS, GROUP), t_s, axis=1)[:, :, None],
        (N_KV_HEADS, GROUP * t_s, LANES))
    attn_s, ko_s, vo_s = _sample_attn_call(
        q_s.transpose(1, 0, 2), k_s.transpose(1, 0, 2), v_s.transpose(1, 0, 2),
        cache_k.reshape(nb_s, WINDOW, KV_DIM), cache_v.reshape(nb_s, WINDOW, KV_DIM), sink_rows)
    y_s = _sample_l1_call(x2_s, attn_s.transpose(1, 0, 2), mod_s1, norm_g[1], wo, fin1, fout1,
                          row(final_g))

    kv_shape = (WINDOW, N_KV_HEADS, HEAD_DIM)
    return (y_prompt,
            y_s.transpose(1, 0, 2),
            conv_p[None, :, SUBLANES - (CONV_WIDTH - 1):, :],
            hl_p[None, :, 0, :],
            k_p[:, t_p - WINDOW:, :].reshape((nb_p,) + kv_shape),
            v_p[:, t_p - WINDOW:, :].reshape((nb_p,) + kv_shape),
            conv_s.transpose(1, 0, 2)[None],
            hl_s[None],
            ko_s.reshape((nb_s,) + kv_shape),
            vo_s.reshape((nb_s,) + kv_shape))
```

```python
import functools

import jax
import jax.numpy as jnp
from jax import lax
from jax.experimental import pallas as pl
from jax.experimental.pallas import tpu as pltpu

F32 = jnp.float32
BF16 = jnp.bfloat16

D_MODEL = 1024
N_RNN_BLOCKS = 8
RNN_BLOCK = D_MODEL // N_RNN_BLOCKS
CONV_WIDTH = 4
RG_C = 8.0
HEAD_DIM = 64
N_HEADS = D_MODEL // HEAD_DIM
N_KV_HEADS = 4
GROUP = N_HEADS // N_KV_HEADS
KV_DIM = N_KV_HEADS * HEAD_DIM
WINDOW = 128
ROT_DIM = HEAD_DIM // 4
ROPE_THETA = 500000.0
EPS = 1e-6
NEG_INF = -1e30
PAST_LEN = 16384

LANES = 128
SUBLANES = 8
MXU_COLS = 256
VMEM_LIMIT = 56 * 1024 * 1024

PROMPT_TILE = 256
ADA_TILE_N = 2048
SAMPLE_ATTN_BATCH = 16
SAMPLE_ATTN_UNROLL = 4
SAMPLE_BATCH_TILE = 64


def _dot(a, b):
    return jnp.dot(a, b, preferred_element_type=F32)


def _dot_nt(a, b):
    return lax.dot_general(a, b, (((1,), (1,)), ((), ())), preferred_element_type=F32)


def _sigmoid(x):
    return 1.0 / (1.0 + jnp.exp(-x))


def _silu(x):
    return x * _sigmoid(x)


def _gelu_tanh(x):
    return x * (0.5 * (1.0 + jnp.tanh(0.7978845608028654 * (x + 0.044715 * (x * x * x)))))


def _log_sigmoid(x):
    return -(jnp.maximum(-x, 0.0) + jnp.log1p(jnp.exp(-jnp.abs(x))))


def _rms(x):
    return x * lax.rsqrt(jnp.mean(x * x, axis=-1, keepdims=True) + EPS)


def _rms_mod(x, g, scale, shift):
    return (_rms(x) * g) * (1.0 + scale) + shift


def _tile_rows(m, reps):
    return jnp.concatenate([m] * reps, axis=0)


def _rope_block(blk, c, s_next, s_prev):
    return blk * c + pltpu.roll(blk, LANES - ROT_DIM // 2, 1) * s_next + pltpu.roll(blk, ROT_DIM // 2, 1) * s_prev


def _rope(x, c, s_next, s_prev):
    blocks = [_rope_block(x[:, j * LANES:(j + 1) * LANES], c, s_next, s_prev)
              for j in range(x.shape[1] // LANES)]
    return jnp.concatenate(blocks, axis=1)


def _proj_in(h, w_in_ref, xr_s, yg_s):
    d = xr_s.shape[1]
    cw = 2 * MXU_COLS
    for c in range(d // cw):
        xr_s[:, c * cw:(c + 1) * cw] = _dot(h, w_in_ref[:, c * cw:(c + 1) * cw])
        yg_s[:, c * cw:(c + 1) * cw] = _dot(h, w_in_ref[:, d + c * cw:d + (c + 1) * cw])


def _rglru_gates(xc, gw_ref, gb_ref, lam_ref, a_s, u_s):
    xcb = xc.astype(BF16)
    cl = RG_C * _log_sigmoid(lam_ref[...])
    for n in range(N_RNN_BLOCKS):
        blk = slice(n * RNN_BLOCK, (n + 1) * RNN_BLOCK)
        g = _dot(xcb[:, blk], gw_ref[n]) + gb_ref[n:n + 1, :]
        r = _sigmoid(g[:, :RNN_BLOCK])
        i = _sigmoid(g[:, RNN_BLOCK:])
        log_a = cl[:, blk] * r
        a = jnp.exp(log_a)
        a_s[:, blk] = a
        u_s[:, blk] = jnp.sqrt(-jnp.tanh(log_a) * (a * a + 1.0)) * (i * xc[:, blk])


def _swiglu(h, fin_ref, fout_ref, act_s):
    d_ff = fout_ref.shape[0]
    for c in range(d_ff // MXU_COLS):
        cols = slice(c * MXU_COLS, (c + 1) * MXU_COLS)
        gate = _dot(h, fin_ref[:, cols])
        up = _dot(h, fin_ref[:, d_ff + c * MXU_COLS:d_ff + (c + 1) * MXU_COLS])
        act_s[:, cols] = (_silu(gate) * up).astype(BF16)
    return _dot(act_s[...], fout_ref[...])


def _low_half(shape):
    return lax.broadcasted_iota(jnp.int32, shape, 1) < LANES // 2


def _dup_halves(blk):
    low = _low_half(blk.shape)
    rot = pltpu.roll(blk, LANES // 2, 1)
    return jnp.where(low, blk, rot), jnp.where(low, rot, blk)


def _split_halves(blk):
    low = _low_half(blk.shape)
    zero = jnp.zeros_like(blk)
    return jnp.where(low, blk, zero), jnp.where(low, zero, blk)


def _attn_scores(q_heads, kwin):
    return _dot_nt(jnp.concatenate(q_heads, axis=0).astype(BF16), kwin)


def _attn_probs(s_all, mask, sinks):
    rb = s_all.shape[0] // GROUP
    probs, maxes = [], []
    for i in range(GROUP):
        s = jnp.where(mask, s_all[i * rb:(i + 1) * rb, :], NEG_INF)
        mx = jnp.maximum(jnp.max(s, axis=-1, keepdims=True), sinks[i])
        probs.append(jnp.exp(s - mx))
        maxes.append(mx)
    return jnp.concatenate(probs, axis=0).astype(BF16), maxes


def _attn_values(p_all, vaug):
    return _dot(p_all, vaug)


def _attn_finish(res_all, maxes, sinks):
    rb = res_all.shape[0] // GROUP
    outs = []
    for i in range(GROUP):
        res = res_all[i * rb:(i + 1) * rb, :]
        den = res[:, LANES:] + jnp.exp(sinks[i] - maxes[i])
        outs.append(res[:, :LANES] * (1.0 / den))
    low = _low_half(outs[0].shape)
    return [jnp.where(low, outs[2 * i], outs[2 * i + 1]) for i in range(GROUP // 2)]


def _ada_kernel(c_ref, w_ref, b_ref, o_ref):
    c = c_ref[...]
    o_ref[0] = _dot(_silu(c).astype(BF16), w_ref[0].astype(BF16)) + b_ref[0]


def _ada_call(c, w, b):
    n_layers, d, n = w.shape
    r = c.shape[0]
    return pl.pallas_call(
        _ada_kernel,
        grid=(n_layers, n // ADA_TILE_N),
        in_specs=[pl.BlockSpec((r, d), lambda l, j: (0, 0)),
                  pl.BlockSpec((1, d, ADA_TILE_N), lambda l, j: (l, 0, j)),
                  pl.BlockSpec((1, 1, ADA_TILE_N), lambda l, j: (l, 0, j))],
        out_specs=pl.BlockSpec((1, r, ADA_TILE_N), lambda l, j: (l, 0, j)),
        out_shape=jax.ShapeDtypeStruct((n_layers, r, n), F32),
        compiler_params=pltpu.CompilerParams(
            dimension_semantics=("arbitrary", "arbitrary"), vmem_limit_bytes=VMEM_LIMIT),
        name="ada_mod",
    )(c, w, b.reshape(n_layers, 1, n))


def _prompt_l0_kernel(x_ref, mod_ref, rope_ref, ng_ref, w_in_ref, cw_ref, cb_ref, gw_ref, gb_ref,
                      lam_ref, w_out_ref, fin_ref, fout_ref, kvg_ref, wkv_ref,
                      x2_ref, k_ref, v_ref, conv_ref, hl_ref,
                      xr_s, yg_s, a_s, u_s, o_s, act_s, hist_s, hc_s):
    tm, d = x_ref.shape

    @pl.when(pl.program_id(1) == 0)
    def _():
        hist_s[...] = jnp.zeros_like(hist_s)
        hc_s[...] = jnp.zeros_like(hc_s)

    x = x_ref[...]
    h = _rms_mod(x, ng_ref[0:1, :], mod_ref[1], mod_ref[0]).astype(BF16)
    _proj_in(h, w_in_ref, xr_s, yg_s)

    xr = xr_s[...]
    hist = hist_s[...]
    row8 = lax.broadcasted_iota(jnp.int32, (SUBLANES, d), 0)

    def shifted(k):
        rolled = pltpu.roll(xr, k, 0)
        first = jnp.where(row8 >= k, rolled[0:SUBLANES], pltpu.roll(hist, k, 0))
        return jnp.concatenate([first, rolled[SUBLANES:]], axis=0)

    xc = cb_ref[...]
    for j in range(CONV_WIDTH - 1):
        xc = xc + cw_ref[j:j + 1, :] * shifted(CONV_WIDTH - 1 - j)
    xc = xc + cw_ref[CONV_WIDTH - 1:CONV_WIDTH, :] * xr
    hist_s[...] = xr[tm - SUBLANES:]
    conv_ref[...] = xr[tm - SUBLANES:]

    _rglru_gates(xc, gw_ref, gb_ref, lam_ref, a_s, u_s)

    def scan_group(g, hprev):
        r0 = pl.multiple_of(g * SUBLANES, SUBLANES)
        a = a_s[pl.ds(r0, SUBLANES), :]
        u = u_s[pl.ds(r0, SUBLANES), :]
        for step in (1, 2, 4):
            keep = row8 >= step
            a_sh = jnp.where(keep, pltpu.roll(a, step, 0), 1.0)
            u_sh = jnp.where(keep, pltpu.roll(u, step, 0), 0.0)
            u = a * u_sh + u
            a = a * a_sh
        hs = a * hprev + u
        o_s[pl.ds(r0, SUBLANES), :] = hs
        return jnp.broadcast_to(hs[SUBLANES - 1:SUBLANES, :], hs.shape)

    h_fin = lax.fori_loop(0, tm // SUBLANES, scan_group, hc_s[...], unroll=4)
    hc_s[...] = h_fin
    hl_ref[...] = h_fin

    z = (o_s[...] * _gelu_tanh(yg_s[...])).astype(BF16)
    x1 = x + mod_ref[2] * _dot(z, w_out_ref[...])

    h2 = _rms_mod(x1, ng_ref[1:2, :], mod_ref[4], mod_ref[3]).astype(BF16)
    x2 = x1 + mod_ref[5] * _swiglu(h2, fin_ref, fout_ref, act_s)
    x2_ref[...] = x2

    hk = _rms_mod(x2, kvg_ref[...], mod_ref[7], mod_ref[6]).astype(BF16)
    kv = _dot(hk, wkv_ref[...])
    k_ref[...] = _rope(kv[:, :KV_DIM], rope_ref[0], rope_ref[1], rope_ref[2])
    v_ref[...] = kv[:, KV_DIM:]


def _const_spec(shape):
    zeros = (0,) * len(shape)
    return pl.BlockSpec(shape, lambda *_: zeros, pipeline_mode=pl.Buffered(1))


def _prompt_l0_call(x, mod, rope, ng, w_in, cw, cb, gw, gb, lam, w_out, fin, fout, kvg, wkv):
    nb, t, d = x.shape
    tm = PROMPT_TILE
    nt = t // tm
    d_ff = fout.shape[0]
    row_spec = lambda w: pl.BlockSpec((None, tm, w), lambda b, i: (b, i, 0))
    state_spec = pl.BlockSpec((None, SUBLANES, d), lambda b, i: (b, 0, 0))
    consts = (ng, w_in, cw, cb, gw, gb, lam, w_out, fin, fout, kvg, wkv)
    return pl.pallas_call(
        _prompt_l0_kernel,
        grid=(nb, nt),
        in_specs=[row_spec(d),
                  pl.BlockSpec((None,) + mod.shape[1:], lambda b, i: (b, 0, 0, 0)),
                  pl.BlockSpec((3, tm, LANES), lambda b, i: (0, i, 0))]
                 + [_const_spec(c.shape) for c in consts],
        out_specs=[row_spec(d), row_spec(KV_DIM), row_spec(KV_DIM), state_spec, state_spec],
        out_shape=[jax.ShapeDtypeStruct((nb, t, d), F32),
                   jax.ShapeDtypeStruct((nb, t, KV_DIM), F32),
                   jax.ShapeDtypeStruct((nb, t, KV_DIM), F32),
                   jax.ShapeDtypeStruct((nb, SUBLANES, d), F32),
                   jax.ShapeDtypeStruct((nb, SUBLANES, d), F32)],
        scratch_shapes=[pltpu.VMEM((tm, d), F32)] * 5
                       + [pltpu.VMEM((tm, d_ff), BF16),
                          pltpu.VMEM((SUBLANES, d), F32), pltpu.VMEM((SUBLANES, d), F32)],
        compiler_params=pltpu.CompilerParams(
            dimension_semantics=("arbitrary", "arbitrary"), vmem_limit_bytes=VMEM_LIMIT),
        name="prompt_layer0",
    )(x, mod, rope, *consts)


def _prompt_l1_kernel(sink_ref, x_ref, k_ref, v_ref, mod_ref, rope_ref, ng_ref, wq_ref, wo_ref,
                      fin_ref, fout_ref, fg_ref,
                      y_ref,
                      kw_s, vw_s, attn_s, act_s):
    tm, d = x_ref.shape
    t = pl.program_id(1)

    @pl.when(t == 0)
    def _():
        kw_s[0:WINDOW, :] = jnp.zeros((WINDOW, kw_s.shape[1]), BF16)
        vw_s[0:WINDOW, :] = jnp.zeros((WINDOW, vw_s.shape[1]), BF16)

    @pl.when(t > 0)
    def _():
        kw_s[0:WINDOW, :] = kw_s[tm:tm + WINDOW, :]
        vw_s[0:WINDOW, :] = vw_s[tm:tm + WINDOW, :]

    new_rows = slice(WINDOW, WINDOW + tm)
    k, v = k_ref[...], v_ref[...]
    ones = jnp.ones((tm, LANES), BF16)
    for pb in range(KV_DIM // LANES):
        k_dup = _dup_halves(k[:, pb * LANES:(pb + 1) * LANES])
        v_dup = _dup_halves(v[:, pb * LANES:(pb + 1) * LANES])
        for i in range(2):
            g = 2 * pb + i
            kw_s[new_rows, g * LANES:(g + 1) * LANES] = k_dup[i].astype(BF16)
            vw_s[new_rows, 2 * g * LANES:(2 * g + 1) * LANES] = v_dup[i].astype(BF16)
            vw_s[new_rows, (2 * g + 1) * LANES:(2 * g + 2) * LANES] = ones

    x = x_ref[...]
    h = _rms_mod(x, ng_ref[0:1, :], mod_ref[1], mod_ref[0]).astype(BF16)
    q = _rope(_dot(h, wq_ref[...]), rope_ref[0], rope_ref[1], rope_ref[2]) * (HEAD_DIM ** -0.5)
    q_split = [_split_halves(q[:, p * LANES:(p + 1) * LANES]) for p in range(d // LANES)]

    span = 2 * WINDOW
    qi = lax.broadcasted_iota(jnp.int32, (WINDOW, span), 0)
    si = lax.broadcasted_iota(jnp.int32, (WINDOW, span), 1)
    band = (si >= qi) & (si <= qi + WINDOW)
    masks = [band & (si >= WINDOW - (t * tm + j * WINDOW)) for j in range(tm // WINDOW)]
    units = [(j, g) for j in range(tm // WINDOW) for g in range(N_KV_HEADS)]
    rows = lambda j: slice(j * WINDOW, (j + 1) * WINDOW)
    win = lambda j: slice(j * WINDOW, j * WINDOW + span)
    sinks = [[sink_ref[g * GROUP + i] for i in range(GROUP)] for g in range(N_KV_HEADS)]
    scores = [_attn_scores([q_split[2 * g + i // 2][i % 2][rows(j), :] for i in range(GROUP)],
                           kw_s[win(j), g * LANES:(g + 1) * LANES]) for j, g in units]
    probs = [_attn_probs(s_all, masks[j], sinks[g]) for s_all, (j, g) in zip(scores, units)]
    values = [_attn_values(p_all, vw_s[win(j), 2 * g * LANES:(2 * g + 2) * LANES])
              for (p_all, _), (j, g) in zip(probs, units)]
    for res_all, (_, maxes), (j, g) in zip(values, probs, units):
        for i, pair in enumerate(_attn_finish(res_all, maxes, sinks[g])):
            col = (2 * g + i) * LANES
            attn_s[rows(j), col:col + LANES] = pair.astype(BF16)

    x1 = x + mod_ref[2] * _dot(attn_s[...], wo_ref[...])
    h2 = _rms_mod(x1, ng_ref[1:2, :], mod_ref[4], mod_ref[3]).astype(BF16)
    x2 = x1 + mod_ref[5] * _swiglu(h2, fin_ref, fout_ref, act_s)
    y_ref[...] = _rms(x2) * fg_ref[...]


def _prompt_l1_call(sinks, x, k, v, mod, rope, ng, wq, wo, fin, fout, fg):
    nb, t, d = x.shape
    tm = PROMPT_TILE
    nt = t // tm
    d_ff = fout.shape[0]
    row_spec = lambda w: pl.BlockSpec((None, tm, w), lambda b, i: (b, i, 0))
    consts = (ng, wq, wo, fin, fout, fg)
    return pl.pallas_call(
        _prompt_l1_kernel,
        grid=(nb, nt),
        in_specs=[pl.BlockSpec(memory_space=pltpu.SMEM),
                  row_spec(d), row_spec(KV_DIM), row_spec(KV_DIM),
                  pl.BlockSpec((None,) + mod.shape[1:], lambda b, i: (b, 0, 0, 0)),
                  pl.BlockSpec((3, tm, LANES), lambda b, i: (0, i, 0))]
                 + [_const_spec(c.shape) for c in consts],
        out_specs=row_spec(d),
        out_shape=jax.ShapeDtypeStruct((nb, t, d), F32),
        scratch_shapes=[pltpu.VMEM((WINDOW + tm, N_KV_HEADS * LANES), BF16),
                        pltpu.VMEM((WINDOW + tm, N_KV_HEADS * 2 * LANES), BF16),
                        pltpu.VMEM((tm, d), BF16),
                        pltpu.VMEM((tm, d_ff), BF16)],
        compiler_params=pltpu.CompilerParams(
            dimension_semantics=("arbitrary", "arbitrary"), vmem_limit_bytes=VMEM_LIMIT),
        name="prompt_layer1",
    )(sinks, x, k, v, mod, rope, *consts)


def _sample_l0_kernel(x_ref, mod_ref, qmod_ref, h0_ref, cst_ref, rope_ref, ng_ref, w_in_ref, cw_ref,
                      cb_ref, gw_ref, gb_ref, lam_ref, w_out_ref, fin_ref, fout_ref, kvg_ref, wkv_ref,
                      qg_ref, wq_ref,
                      x2_ref, k_ref, v_ref, q_ref, conv_ref, hl_ref,
                      xr_s, yg_s, a_s, u_s, o_s, act_s):
    nt, sb, d = x_ref.shape
    rows = nt * sb
    slab = lambda t: slice(t * sb, (t + 1) * sb)
    mod = lambda i: _tile_rows(mod_ref[i], nt)

    x = x_ref[...].reshape(rows, d)
    h = _rms_mod(x, ng_ref[0:1, :], mod(1), mod(0)).astype(BF16)
    _proj_in(h, w_in_ref, xr_s, yg_s)

    def conv_in(j):
        return cst_ref[j] if j < CONV_WIDTH - 1 else xr_s[slab(j - (CONV_WIDTH - 1)), :]

    xc_slabs = []
    for t in range(nt):
        acc = cb_ref[...]
        for j in range(CONV_WIDTH):
            acc = acc + cw_ref[j:j + 1, :] * conv_in(t + j)
        xc_slabs.append(acc)
    xc = jnp.concatenate(xc_slabs, axis=0)
    for j in range(CONV_WIDTH - 1):
        conv_ref[j] = xr_s[slab(nt - (CONV_WIDTH - 1) + j), :]

    _rglru_gates(xc, gw_ref, gb_ref, lam_ref, a_s, u_s)

    hs = h0_ref[...]
    for t in range(nt):
        hs = a_s[slab(t), :] * hs + u_s[slab(t), :]
        o_s[slab(t), :] = hs
    hl_ref[...] = hs

    z = (o_s[...] * _gelu_tanh(yg_s[...])).astype(BF16)
    x1 = x + mod(2) * _dot(z, w_out_ref[...])

    h2 = _rms_mod(x1, ng_ref[1:2, :], mod(4), mod(3)).astype(BF16)
    x2 = x1 + mod(5) * _swiglu(h2, fin_ref, fout_ref, act_s)
    x2_ref[...] = x2.reshape(nt, sb, d)

    hk = _rms_mod(x2, kvg_ref[...], mod(7), mod(6)).astype(BF16)
    kv = _dot(hk, wkv_ref[...])
    hq = _rms_mod(x2, qg_ref[...], _tile_rows(qmod_ref[1], nt), _tile_rows(qmod_ref[0], nt)).astype(BF16)
    q = _dot(hq, wq_ref[...])
    for t in range(nt):
        c, s_next, s_prev = rope_ref[0, t:t + 1, :], rope_ref[1, t:t + 1, :], rope_ref[2, t:t + 1, :]
        k_ref[t] = _rope(kv[slab(t), :KV_DIM], c, s_next, s_prev)
        q_ref[t] = _rope(q[slab(t), :], c, s_next, s_prev) * (HEAD_DIM ** -0.5)
    v_ref[...] = kv[:, KV_DIM:].reshape(nt, sb, KV_DIM)


def _sample_l0_call(x, mod, qmod, h0, cst, rope, ng, w_in, cw, cb, gw, gb, lam, w_out, fin, fout,
                    kvg, wkv, qg, wq):
    nt, nb, d = x.shape
    sb = SAMPLE_BATCH_TILE
    d_ff = fout.shape[0]
    rows = nt * sb
    slab_spec = lambda lead, w: pl.BlockSpec((lead, sb, w), lambda i: (0, i, 0))
    consts = (rope, ng, w_in, cw, cb, gw, gb, lam, w_out, fin, fout, kvg, wkv, qg, wq)
    return pl.pallas_call(
        _sample_l0_kernel,
        grid=(nb // sb,),
        in_specs=[slab_spec(nt, d), slab_spec(mod.shape[0], d), slab_spec(qmod.shape[0], d),
                  pl.BlockSpec((sb, d), lambda i: (i, 0)), slab_spec(CONV_WIDTH - 1, d)]
                 + [_const_spec(c.shape) for c in consts],
        out_specs=[slab_spec(nt, d), slab_spec(nt, KV_DIM), slab_spec(nt, KV_DIM), slab_spec(nt, d),
                   slab_spec(CONV_WIDTH - 1, d), pl.BlockSpec((sb, d), lambda i: (i, 0))],
        out_shape=[jax.ShapeDtypeStruct((nt, nb, d), F32),
                   jax.ShapeDtypeStruct((nt, nb, KV_DIM), F32),
                   jax.ShapeDtypeStruct((nt, nb, KV_DIM), F32),
                   jax.ShapeDtypeStruct((nt, nb, d), F32),
                   jax.ShapeDtypeStruct((CONV_WIDTH - 1, nb, d), F32),
                   jax.ShapeDtypeStruct((nb, d), F32)],
        scratch_shapes=[pltpu.VMEM((rows, d), F32)] * 5 + [pltpu.VMEM((rows, d_ff), BF16)],
        compiler_params=pltpu.CompilerParams(
            dimension_semantics=("arbitrary",), vmem_limit_bytes=VMEM_LIMIT),
        name="sample_layer0",
    )(x, mod, qmod, h0, cst, *consts)


def _sample_attn_kernel(sink_ref, q_ref, kn_ref, vn_ref, ck_ref, cv_ref,
                        attn_ref, ko_ref, vo_ref):
    sb, nt, d = q_ref.shape
    span = 2 * WINDOW

    tok = lax.broadcasted_iota(jnp.int32, (nt, span), 0)
    si = lax.broadcasted_iota(jnp.int32, (nt, span), 1)
    mask = (si >= tok) & (si <= tok + WINDOW)
    pad = jnp.zeros((WINDOW - nt, LANES), F32)
    ones = jnp.ones((span, LANES), BF16)

    sinks = [[sink_ref[g * GROUP + i] for i in range(GROUP)] for g in range(N_KV_HEADS)]

    def windows(b):
        kn, vn = kn_ref[b], vn_ref[b]
        ck, cv = ck_ref[b], cv_ref[b]
        ko_ref[b, 0:WINDOW - nt, :] = ck[nt:, :]
        ko_ref[b, WINDOW - nt:WINDOW, :] = kn
        vo_ref[b, 0:WINDOW - nt, :] = cv[nt:, :]
        vo_ref[b, WINDOW - nt:WINDOW, :] = vn
        kwin, vaug = [], []
        for pb in range(KV_DIM // LANES):
            blk = slice(pb * LANES, (pb + 1) * LANES)
            (kc, kn_d), (vc, vn_d) = [(_dup_halves(c[:, blk]), _dup_halves(n[:, blk]))
                                      for c, n in ((ck, kn), (cv, vn))]
            for i in range(2):
                kwin.append(jnp.concatenate([kc[i], kn_d[i], pad], axis=0).astype(BF16))
                vcol = jnp.concatenate([vc[i], vn_d[i], pad], axis=0).astype(BF16)
                vaug.append(jnp.concatenate([vcol, ones], axis=1))
        return kwin, vaug

    def sequences(it, carry):
        seqs = [it * SAMPLE_ATTN_UNROLL + u for u in range(SAMPLE_ATTN_UNROLL)]
        wins = [windows(b) for b in seqs]
        units = [(u, g) for u in range(SAMPLE_ATTN_UNROLL) for g in range(N_KV_HEADS)]
        q_split = [[_split_halves(q_ref[b][:, p * LANES:(p + 1) * LANES]) for p in range(d // LANES)]
                   for b in seqs]
        scores = [_attn_scores([q_split[u][2 * g + i // 2][i % 2] for i in range(GROUP)], wins[u][0][g])
                  for u, g in units]
        probs = [_attn_probs(s_all, mask, sinks[g]) for s_all, (u, g) in zip(scores, units)]
        values = [_attn_values(p_all, wins[u][1][g]) for (p_all, _), (u, g) in zip(probs, units)]
        pairs = [[] for _ in seqs]
        for res_all, (_, maxes), (u, g) in zip(values, probs, units):
            pairs[u] += _attn_finish(res_all, maxes, sinks[g])
        for u, b in enumerate(seqs):
            attn_ref[b] = jnp.concatenate(pairs[u], axis=1)
        return carry

    lax.fori_loop(0, sb // SAMPLE_ATTN_UNROLL, sequences, 0)


def _sample_attn_call(sinks, q, kn, vn, ck, cv):
    nb, nt, d = q.shape
    sb = SAMPLE_ATTN_BATCH
    seq_spec = lambda r, w: pl.BlockSpec((sb, r, w), lambda i: (i, 0, 0))
    return pl.pallas_call(
        _sample_attn_kernel,
        grid=(nb // sb,),
        in_specs=[pl.BlockSpec(memory_space=pltpu.SMEM),
                  seq_spec(nt, d), seq_spec(nt, KV_DIM), seq_spec(nt, KV_DIM),
                  seq_spec(WINDOW, KV_DIM), seq_spec(WINDOW, KV_DIM)],
        out_specs=[seq_spec(nt, d), seq_spec(WINDOW, KV_DIM), seq_spec(WINDOW, KV_DIM)],
        out_shape=[jax.ShapeDtypeStruct((nb, nt, d), F32),
                   jax.ShapeDtypeStruct((nb, WINDOW, KV_DIM), F32),
                   jax.ShapeDtypeStruct((nb, WINDOW, KV_DIM), F32)],
        compiler_params=pltpu.CompilerParams(
            dimension_semantics=("arbitrary",), vmem_limit_bytes=VMEM_LIMIT),
        name="sample_attention",
    )(sinks, q, kn, vn, ck, cv)


def _sample_l1_kernel(x_ref, attn_ref, mod_ref, ng_ref, wo_ref, fin_ref, fout_ref, fg_ref,
                      y_ref, act_s):
    nt, sb, d = x_ref.shape
    rows = nt * sb
    mod = lambda i: _tile_rows(mod_ref[i], nt)
    x = x_ref[...].reshape(rows, d)
    attn = attn_ref[...].reshape(rows, d).astype(BF16)
    x1 = x + mod(2) * _dot(attn, wo_ref[...])
    h2 = _rms_mod(x1, ng_ref[1:2, :], mod(4), mod(3)).astype(BF16)
    x2 = x1 + mod(5) * _swiglu(h2, fin_ref, fout_ref, act_s)
    y_ref[...] = (_rms(x2) * fg_ref[...]).reshape(nt, sb, d)


def _sample_l1_call(x, attn, mod, ng, wo, fin, fout, fg):
    nt, nb, d = x.shape
    sb = SAMPLE_BATCH_TILE
    d_ff = fout.shape[0]
    slab_spec = lambda lead: pl.BlockSpec((lead, sb, d), lambda i: (0, i, 0))
    consts = (ng, wo, fin, fout, fg)
    return pl.pallas_call(
        _sample_l1_kernel,
        grid=(nb // sb,),
        in_specs=[slab_spec(nt), slab_spec(nt), slab_spec(mod.shape[0])]
                 + [_const_spec(c.shape) for c in consts],
        out_specs=slab_spec(nt),
        out_shape=jax.ShapeDtypeStruct((nt, nb, d), F32),
        scratch_shapes=[pltpu.VMEM((nt * sb, d_ff), BF16)],
        compiler_params=pltpu.CompilerParams(
            dimension_semantics=("arbitrary",), vmem_limit_bytes=VMEM_LIMIT),
        name="sample_layer1",
    )(x, attn, mod, *consts)


def _rope_tables(pos):
    half = ROT_DIM // 2
    inv = ROPE_THETA ** (-jnp.arange(0, ROT_DIM, 2, dtype=F32) / ROT_DIM)
    ang = pos.astype(F32)[:, None] * inv[None, :]
    cos, sin = jnp.cos(ang), jnp.sin(ang)
    n = pos.shape[0]
    rest = HEAD_DIM - ROT_DIM
    c = jnp.concatenate([cos, cos, jnp.ones((n, rest), F32)], axis=1)
    s_next = jnp.concatenate([-sin, jnp.zeros((n, half + rest), F32)], axis=1)
    s_prev = jnp.concatenate([jnp.zeros((n, half), F32), sin, jnp.zeros((n, rest), F32)], axis=1)
    reps = LANES // HEAD_DIM
    return jnp.stack([jnp.tile(c, (1, reps)), jnp.tile(s_next, (1, reps)), jnp.tile(s_prev, (1, reps))])


def kernel(x_prompt, x_sample, c_prompt, c_sample, state_conv, state_h, cache_k, cache_v, ada_w, ada_b, norm_g, rnn_w_in, rnn_conv_w, rnn_conv_b, rnn_gate_w, rnn_gate_b, rnn_lambda, rnn_w_out, kv_ada_w, kv_ada_b, kv_norm_g, w_kv, attn_w_q, attn_sinks, attn_w_o, ffn_w_in, ffn_w_out, final_g):
    nb_p, t_p, d = x_prompt.shape
    nb_s, t_s, _ = x_sample.shape

    c_all = jnp.concatenate([c_prompt, c_sample], axis=0)
    ada = _ada_call(c_all, ada_w, ada_b)
    kv_ada = _ada_call(c_all, kv_ada_w[None], kv_ada_b[None])
    n_all = c_all.shape[0]
    mod_l0 = jnp.concatenate([ada[0].reshape(n_all, 6, d), kv_ada[0].reshape(n_all, 2, d)], axis=1)
    mod_l1 = ada[1].reshape(n_all, 6, d)

    bf = lambda w: w.astype(BF16)
    row = lambda v: v.reshape(1, -1)
    l0_consts = (norm_g[0], bf(rnn_w_in[0]), rnn_conv_w[0], row(rnn_conv_b[0]), bf(rnn_gate_w[0]),
                 rnn_gate_b[0], row(rnn_lambda[0]), bf(rnn_w_out[0]), bf(ffn_w_in[0]), bf(ffn_w_out[0]),
                 row(kv_norm_g), bf(w_kv))
    wq, wo = bf(attn_w_q[0]), bf(attn_w_o[0])
    fin1, fout1 = bf(ffn_w_in[1]), bf(ffn_w_out[1])

    rope_p = _rope_tables(jnp.arange(t_p, dtype=jnp.int32))
    x2_p, k_p, v_p, conv_p, hl_p = _prompt_l0_call(
        x_prompt, mod_l0[:nb_p, :, None, :], rope_p, *l0_consts)
    y_prompt = _prompt_l1_call(attn_sinks[0], x2_p, k_p, v_p, mod_l1[:nb_p, :, None, :], rope_p,
                               norm_g[1], wq, wo, fin1, fout1, row(final_g))

    rope_s = _rope_tables(PAST_LEN + jnp.arange(t_s, dtype=jnp.int32))
    mod_s0 = mod_l0[nb_p:].transpose(1, 0, 2)
    mod_s1 = mod_l1[nb_p:].transpose(1, 0, 2)
    x2_s, k_s, v_s, q_s, conv_s, hl_s = _sample_l0_call(
        x_sample.transpose(1, 0, 2), mod_s0, mod_s1[0:2], state_h[0], state_conv[0].transpose(1, 0, 2),
        rope_s, *l0_consts, norm_g[1, 0:1], wq)
    attn_s, ko_s, vo_s = _sample_attn_call(
        attn_sinks[0], q_s.transpose(1, 0, 2), k_s.transpose(1, 0, 2), v_s.transpose(1, 0, 2),
        cache_k.reshape(nb_s, WINDOW, KV_DIM), cache_v.reshape(nb_s, WINDOW, KV_DIM))
    y_s = _sample_l1_call(x2_s, attn_s.transpose(1, 0, 2), mod_s1, norm_g[1], wo, fin1, fout1,
                          row(final_g))

    kv_shape = (WINDOW, N_KV_HEADS, HEAD_DIM)
    return (y_prompt,
            y_s.transpose(1, 0, 2),
            conv_p[None, :, SUBLANES - (CONV_WIDTH - 1):, :],
            hl_p[None, :, 0, :],
            k_p[:, t_p - WINDOW:, :].reshape((nb_p,) + kv_shape),
            v_p[:, t_p - WINDOW:, :].reshape((nb_p,) + kv_shape),
            conv_s.transpose(1, 0, 2)[None],
            hl_s[None],
            ko_s.reshape((nb_s,) + kv_shape),
            vo_s.reshape((nb_s,) + kv_shape))
```

```python
import functools

import jax
import jax.numpy as jnp
from jax import lax
from jax.experimental import pallas as pl
from jax.experimental.pallas import tpu as pltpu

F32 = jnp.float32
BF16 = jnp.bfloat16

D_MODEL = 1024
N_RNN_BLOCKS = 8
RNN_BLOCK = D_MODEL // N_RNN_BLOCKS
CONV_WIDTH = 4
RG_C = 8.0
HEAD_DIM = 64
N_HEADS = D_MODEL // HEAD_DIM
N_KV_HEADS = 4
GROUP = N_HEADS // N_KV_HEADS
KV_DIM = N_KV_HEADS * HEAD_DIM
WINDOW = 128
ROT_DIM = HEAD_DIM // 4
ROPE_THETA = 500000.0
EPS = 1e-6
NEG_INF = -1e30
PAST_LEN = 16384

LANES = 128
SUBLANES = 8
MXU_COLS = 256
VMEM_LIMIT = 56 * 1024 * 1024

PROMPT_TILE = 256
FFN_SPLIT = 3
ADA_TILE_N = 2048
SAMPLE_ATTN_BATCH = 16
SAMPLE_ATTN_UNROLL = 4
SAMPLE_BATCH_TILE = 64


def _dot(a, b):
    return jnp.dot(a, b, preferred_element_type=F32)


def _dot_nt(a, b):
    return lax.dot_general(a, b, (((1,), (1,)), ((), ())), preferred_element_type=F32)


def _sigmoid(x):
    return 1.0 / (1.0 + jnp.exp(-x))


def _silu(x):
    return x * _sigmoid(x)


def _gelu_tanh(x):
    return x * (0.5 * (1.0 + jnp.tanh(0.7978845608028654 * (x + 0.044715 * (x * x * x)))))


def _log_sigmoid(x):
    return -(jnp.maximum(-x, 0.0) + jnp.log1p(jnp.exp(-jnp.abs(x))))


def _rms(x):
    return x * lax.rsqrt(jnp.mean(x * x, axis=-1, keepdims=True) + EPS)


def _rms_mod(x, g, scale, shift):
    return (_rms(x) * g) * (1.0 + scale) + shift


def _tile_rows(m, reps):
    return jnp.concatenate([m] * reps, axis=0)


def _rope_block(blk, c, s_next, s_prev):
    return blk * c + pltpu.roll(blk, LANES - ROT_DIM // 2, 1) * s_next + pltpu.roll(blk, ROT_DIM // 2, 1) * s_prev


def _rope(x, c, s_next, s_prev):
    blocks = [_rope_block(x[:, j * LANES:(j + 1) * LANES], c, s_next, s_prev)
              for j in range(x.shape[1] // LANES)]
    return jnp.concatenate(blocks, axis=1)


def _proj_in(h, w_in_ref, xr_s, yg_s):
    d = xr_s.shape[1]
    cw = 2 * MXU_COLS
    for c in range(d // cw):
        xr_s[:, c * cw:(c + 1) * cw] = _dot(h, w_in_ref[:, c * cw:(c + 1) * cw])
        yg_s[:, c * cw:(c + 1) * cw] = _dot(h, w_in_ref[:, d + c * cw:d + (c + 1) * cw])


def _rglru_gates(xc, gw_ref, gb_ref, lam_ref, a_s, u_s):
    xcb = xc.astype(BF16)
    cl = RG_C * _log_sigmoid(lam_ref[...])
    for n in range(N_RNN_BLOCKS):
        blk = slice(n * RNN_BLOCK, (n + 1) * RNN_BLOCK)
        g = _dot(xcb[:, blk], gw_ref[n]) + gb_ref[n:n + 1, :]
        r = _sigmoid(g[:, :RNN_BLOCK])
        i = _sigmoid(g[:, RNN_BLOCK:])
        log_a = cl[:, blk] * r
        a = jnp.exp(log_a)
        a_s[:, blk] = a
        u_s[:, blk] = jnp.sqrt(-jnp.tanh(log_a) * (a * a + 1.0)) * (i * xc[:, blk])


def _swiglu_in(h, fin_ref, act_s, chunks):
    d_ff = act_s.shape[1]
    for c in chunks:
        cols = slice(c * MXU_COLS, (c + 1) * MXU_COLS)
        gate = _dot(h, fin_ref[:, cols])
        up = _dot(h, fin_ref[:, d_ff + c * MXU_COLS:d_ff + (c + 1) * MXU_COLS])
        act_s[:, cols] = (_silu(gate) * up).astype(BF16)


def _swiglu(h, fin_ref, fout_ref, act_s):
    _swiglu_in(h, fin_ref, act_s, range(fout_ref.shape[0] // MXU_COLS))
    return _dot(act_s[...], fout_ref[...])


def _low_half(shape):
    return lax.broadcasted_iota(jnp.int32, shape, 1) < LANES // 2


def _dup_halves(blk):
    low = _low_half(blk.shape)
    rot = pltpu.roll(blk, LANES // 2, 1)
    return jnp.where(low, blk, rot), jnp.where(low, rot, blk)


def _split_halves(blk):
    low = _low_half(blk.shape)
    zero = jnp.zeros_like(blk)
    return jnp.where(low, blk, zero), jnp.where(low, zero, blk)


def _attn_scores(q_heads, kwin):
    return _dot_nt(jnp.concatenate(q_heads, axis=0).astype(BF16), kwin)


def _attn_probs(s_all, mask, sinks):
    rb = s_all.shape[0] // GROUP
    probs, maxes = [], []
    for i in range(GROUP):
        s = jnp.where(mask, s_all[i * rb:(i + 1) * rb, :], NEG_INF)
        mx = jnp.maximum(jnp.max(s, axis=-1, keepdims=True), sinks[i])
        probs.append(jnp.exp(s - mx))
        maxes.append(mx)
    return jnp.concatenate(probs, axis=0).astype(BF16), maxes


def _attn_values(p_all, vaug):
    return _dot(p_all, vaug)


def _attn_finish(res_all, maxes, sinks):
    rb = res_all.shape[0] // GROUP
    outs = []
    for i in range(GROUP):
        res = res_all[i * rb:(i + 1) * rb, :]
        den = res[:, LANES:] + jnp.exp(sinks[i] - maxes[i])
        outs.append(res[:, :LANES] * (1.0 / den))
    low = _low_half(outs[0].shape)
    return [jnp.where(low, outs[2 * i], outs[2 * i + 1]) for i in range(GROUP // 2)]


def _ada_kernel(c_ref, w_ref, b_ref, o_ref):
    c = c_ref[...]
    o_ref[0] = _dot(_silu(c).astype(BF16), w_ref[0].astype(BF16)) + b_ref[0]


def _ada_call(c, w, b):
    n_layers, d, n = w.shape
    r = c.shape[0]
    return pl.pallas_call(
        _ada_kernel,
        grid=(n_layers, n // ADA_TILE_N),
        in_specs=[pl.BlockSpec((r, d), lambda l, j: (0, 0)),
                  pl.BlockSpec((1, d, ADA_TILE_N), lambda l, j: (l, 0, j)),
                  pl.BlockSpec((1, 1, ADA_TILE_N), lambda l, j: (l, 0, j))],
        out_specs=pl.BlockSpec((1, r, ADA_TILE_N), lambda l, j: (l, 0, j)),
        out_shape=jax.ShapeDtypeStruct((n_layers, r, n), F32),
        compiler_params=pltpu.CompilerParams(
            dimension_semantics=("arbitrary", "arbitrary"), vmem_limit_bytes=VMEM_LIMIT),
        name="ada_mod",
    )(c, w, b.reshape(n_layers, 1, n))


def _prompt_l0_kernel(x_ref, mod_ref, modp_ref, rope_ref, ng_ref, w_in_ref, cw_ref, cb_ref, gw_ref,
                      gb_ref, lam_ref, w_out_ref, fin_ref, fout_ref, kvg_ref, wkv_ref,
                      x2_ref, k_ref, v_ref, conv_ref, hl_ref,
                      xr_s, yg_s, act_s, hist_s, hc_s, z_s, xk_s, *, nt, n_tiles):
    tm, d = x_ref.shape
    s = pl.program_id(0)
    slot = lax.rem(s, 2)
    prev = 1 - slot
    n_chunks = fout_ref.shape[0] // MXU_COLS

    @pl.when(s == 0)
    def _():
        z_s[1] = jnp.zeros((tm, d), BF16)
        xk_s[1] = jnp.zeros((tm, d), F32)

    @pl.when(lax.rem(jnp.minimum(s, n_tiles - 1), nt) == 0)
    def _():
        hist_s[...] = jnp.zeros_like(hist_s)
        hc_s[...] = jnp.zeros_like(hc_s)

    out_prev = _dot(z_s[prev], w_out_ref[...])

    x = x_ref[...]
    h = _rms_mod(x, ng_ref[0:1, :], mod_ref[1], mod_ref[0]).astype(BF16)
    _proj_in(h, w_in_ref, xr_s, yg_s)
    xk_s[slot] = x

    x1 = xk_s[prev] + modp_ref[2] * out_prev
    h2 = _rms_mod(x1, ng_ref[1:2, :], modp_ref[4], modp_ref[3]).astype(BF16)
    _swiglu_in(h2, fin_ref, act_s, range(0, FFN_SPLIT))

    xr = xr_s[...]
    hist = hist_s[...]
    row8 = lax.broadcasted_iota(jnp.int32, (SUBLANES, d), 0)

    def shifted(k):
        rolled = pltpu.roll(xr, k, 0)
        first = jnp.where(row8 >= k, rolled[0:SUBLANES], pltpu.roll(hist, k, 0))
        return jnp.concatenate([first, rolled[SUBLANES:]], axis=0)

    xc = cb_ref[...]
    for j in range(CONV_WIDTH - 1):
        xc = xc + cw_ref[j:j + 1, :] * shifted(CONV_WIDTH - 1 - j)
    xc = xc + cw_ref[CONV_WIDTH - 1:CONV_WIDTH, :] * xr
    hist_s[...] = xr[tm - SUBLANES:]
    xr_s[...] = xc

    cl = RG_C * _log_sigmoid(lam_ref[...])
    row8b = lax.broadcasted_iota(jnp.int32, (SUBLANES, RNN_BLOCK), 0)

    def recurrent_block(n):
        blk = slice(n * RNN_BLOCK, (n + 1) * RNN_BLOCK)
        xc_blk = xr_s[:, blk]
        gates = _dot(xc_blk.astype(BF16), gw_ref[n]) + gb_ref[n:n + 1, :]
        r = _sigmoid(gates[:, :RNN_BLOCK])
        i = _sigmoid(gates[:, RNN_BLOCK:])
        log_a = cl[:, blk] * r
        a_all = jnp.exp(log_a)
        u_all = jnp.sqrt(-jnp.tanh(log_a) * (a_all * a_all + 1.0)) * (i * xc_blk)
        hprev = hc_s[:, blk]
        hs_groups = []
        for g in range(tm // SUBLANES):
            a = a_all[g * SUBLANES:(g + 1) * SUBLANES, :]
            u = u_all[g * SUBLANES:(g + 1) * SUBLANES, :]
            for step in (1, 2, 4):
                keep = row8b >= step
                a_sh = jnp.where(keep, pltpu.roll(a, step, 0), 1.0)
                u_sh = jnp.where(keep, pltpu.roll(u, step, 0), 0.0)
                u = a * u_sh + u
                a = a * a_sh
            hs = a * hprev + u
            hprev = jnp.broadcast_to(hs[SUBLANES - 1:SUBLANES, :], hs.shape)
            hs_groups.append(hs)
        hc_s[:, blk] = hprev
        z_s[slot, :, blk] = (jnp.concatenate(hs_groups, axis=0) * _gelu_tanh(yg_s[:, blk])).astype(BF16)

    blocks = list(range(N_RNN_BLOCKS))
    for c in range(FFN_SPLIT, n_chunks):
        _swiglu_in(h2, fin_ref, act_s, [c])
        if blocks:
            recurrent_block(blocks.pop(0))
    ffn = _dot(act_s[...], fout_ref[...])
    for n in blocks:
        recurrent_block(n)

    @pl.when(s < n_tiles)
    def _():
        conv_ref[...] = hist_s[...]
        hl_ref[...] = hc_s[...]

    x2 = x1 + modp_ref[5] * ffn
    x2_ref[...] = x2
    hk = _rms_mod(x2, kvg_ref[...], modp_ref[7], modp_ref[6]).astype(BF16)
    kv = _dot(hk, wkv_ref[...])
    k_ref[...] = _rope(kv[:, :KV_DIM], rope_ref[0], rope_ref[1], rope_ref[2])
    v_ref[...] = kv[:, KV_DIM:]


def _const_spec(shape):
    zeros = (0,) * len(shape)
    return pl.BlockSpec(shape, lambda *_: zeros, pipeline_mode=pl.Buffered(1))


def _prompt_l0_call(x, mod, rope, ng, w_in, cw, cb, gw, gb, lam, w_out, fin, fout, kvg, wkv):
    nb, t, d = x.shape
    tm = PROMPT_TILE
    nt = t // tm
    n_tiles = nb * nt
    d_ff = fout.shape[0]
    cur = lambda s: jnp.minimum(s, n_tiles - 1)
    prv = lambda s: jnp.maximum(s - 1, 0)
    tile_spec = lambda w, f: pl.BlockSpec((None, tm, w), lambda s: (f(s) // nt, lax.rem(f(s), nt), 0))
    mod_spec = lambda f: pl.BlockSpec((None,) + mod.shape[1:], lambda s: (f(s) // nt, 0, 0, 0))
    state_spec = pl.BlockSpec((None, SUBLANES, d), lambda s: (cur(s) // nt, 0, 0))
    consts = (ng, w_in, cw, cb, gw, gb, lam, w_out, fin, fout, kvg, wkv)
    return pl.pallas_call(
        functools.partial(_prompt_l0_kernel, nt=nt, n_tiles=n_tiles),
        grid=(n_tiles + 1,),
        in_specs=[tile_spec(d, cur), mod_spec(cur), mod_spec(prv),
                  pl.BlockSpec((3, tm, LANES), lambda s: (0, lax.rem(prv(s), nt), 0))]
                 + [_const_spec(c.shape) for c in consts],
        out_specs=[tile_spec(d, prv), tile_spec(KV_DIM, prv), tile_spec(KV_DIM, prv),
                   state_spec, state_spec],
        out_shape=[jax.ShapeDtypeStruct((nb, t, d), F32),
                   jax.ShapeDtypeStruct((nb, t, KV_DIM), F32),
                   jax.ShapeDtypeStruct((nb, t, KV_DIM), F32),
                   jax.ShapeDtypeStruct((nb, SUBLANES, d), F32),
                   jax.ShapeDtypeStruct((nb, SUBLANES, d), F32)],
        scratch_shapes=[pltpu.VMEM((tm, d), F32), pltpu.VMEM((tm, d), F32),
                        pltpu.VMEM((tm, d_ff), BF16),
                        pltpu.VMEM((SUBLANES, d), F32), pltpu.VMEM((SUBLANES, d), F32),
                        pltpu.VMEM((2, tm, d), BF16), pltpu.VMEM((2, tm, d), F32)],
        compiler_params=pltpu.CompilerParams(
            dimension_semantics=("arbitrary",), vmem_limit_bytes=VMEM_LIMIT),
        name="prompt_layer0",
    )(x, mod, mod, rope, *consts)


def _prompt_l1_kernel(sink_ref, x_ref, k_ref, v_ref, mod_ref, rope_ref, ng_ref, wq_ref, wo_ref,
                      fin_ref, fout_ref, fg_ref,
                      y_ref,
                      kw_s, vw_s, attn_s, act_s):
    tm, d = x_ref.shape
    t = pl.program_id(1)

    @pl.when(t == 0)
    def _():
        kw_s[0:WINDOW, :] = jnp.zeros((WINDOW, kw_s.shape[1]), BF16)
        vw_s[0:WINDOW, :] = jnp.zeros((WINDOW, vw_s.shape[1]), BF16)

    @pl.when(t > 0)
    def _():
        kw_s[0:WINDOW, :] = kw_s[tm:tm + WINDOW, :]
        vw_s[0:WINDOW, :] = vw_s[tm:tm + WINDOW, :]

    new_rows = slice(WINDOW, WINDOW + tm)
    k, v = k_ref[...], v_ref[...]
    ones = jnp.ones((tm, LANES), BF16)
    for pb in range(KV_DIM // LANES):
        k_dup = _dup_halves(k[:, pb * LANES:(pb + 1) * LANES])
        v_dup = _dup_halves(v[:, pb * LANES:(pb + 1) * LANES])
        for i in range(2):
            g = 2 * pb + i
            kw_s[new_rows, g * LANES:(g + 1) * LANES] = k_dup[i].astype(BF16)
            vw_s[new_rows, 2 * g * LANES:(2 * g + 1) * LANES] = v_dup[i].astype(BF16)
            vw_s[new_rows, (2 * g + 1) * LANES:(2 * g + 2) * LANES] = ones

    x = x_ref[...]
    h = _rms_mod(x, ng_ref[0:1, :], mod_ref[1], mod_ref[0]).astype(BF16)
    q = _rope(_dot(h, wq_ref[...]), rope_ref[0], rope_ref[1], rope_ref[2]) * (HEAD_DIM ** -0.5)
    q_split = [_split_halves(q[:, p * LANES:(p + 1) * LANES]) for p in range(d // LANES)]

    span = 2 * WINDOW
    qi = lax.broadcasted_iota(jnp.int32, (WINDOW, span), 0)
    si = lax.broadcasted_iota(jnp.int32, (WINDOW, span), 1)
    band = (si >= qi) & (si <= qi + WINDOW)
    masks = [band & (si >= WINDOW - (t * tm + j * WINDOW)) for j in range(tm // WINDOW)]
    units = [(j, g) for j in range(tm // WINDOW) for g in range(N_KV_HEADS)]
    rows = lambda j: slice(j * WINDOW, (j + 1) * WINDOW)
    win = lambda j: slice(j * WINDOW, j * WINDOW + span)
    sinks = [[sink_ref[g * GROUP + i] for i in range(GROUP)] for g in range(N_KV_HEADS)]
    scores = [_attn_scores([q_split[2 * g + i // 2][i % 2][rows(j), :] for i in range(GROUP)],
                           kw_s[win(j), g * LANES:(g + 1) * LANES]) for j, g in units]
    probs = [_attn_probs(s_all, masks[j], sinks[g]) for s_all, (j, g) in zip(scores, units)]
    values = [_attn_values(p_all, vw_s[win(j), 2 * g * LANES:(2 * g + 2) * LANES])
              for (p_all, _), (j, g) in zip(probs, units)]
    for res_all, (_, maxes), (j, g) in zip(values, probs, units):
        for i, pair in enumerate(_attn_finish(res_all, maxes, sinks[g])):
            col = (2 * g + i) * LANES
            attn_s[rows(j), col:col + LANES] = pair.astype(BF16)

    x1 = x + mod_ref[2] * _dot(attn_s[...], wo_ref[...])
    h2 = _rms_mod(x1, ng_ref[1:2, :], mod_ref[4], mod_ref[3]).astype(BF16)
    x2 = x1 + mod_ref[5] * _swiglu(h2, fin_ref, fout_ref, act_s)
    y_ref[...] = _rms(x2) * fg_ref[...]


def _prompt_l1_call(sinks, x, k, v, mod, rope, ng, wq, wo, fin, fout, fg):
    nb, t, d = x.shape
    tm = PROMPT_TILE
    nt = t // tm
    d_ff = fout.shape[0]
    row_spec = lambda w: pl.BlockSpec((None, tm, w), lambda b, i: (b, i, 0))
    consts = (ng, wq, wo, fin, fout, fg)
    return pl.pallas_call(
        _prompt_l1_kernel,
        grid=(nb, nt),
        in_specs=[pl.BlockSpec(memory_space=pltpu.SMEM),
                  row_spec(d), row_spec(KV_DIM), row_spec(KV_DIM),
                  pl.BlockSpec((None,) + mod.shape[1:], lambda b, i: (b, 0, 0, 0)),
                  pl.BlockSpec((3, tm, LANES), lambda b, i: (0, i, 0))]
                 + [_const_spec(c.shape) for c in consts],
        out_specs=row_spec(d),
        out_shape=jax.ShapeDtypeStruct((nb, t, d), F32),
        scratch_shapes=[pltpu.VMEM((WINDOW + tm, N_KV_HEADS * LANES), BF16),
                        pltpu.VMEM((WINDOW + tm, N_KV_HEADS * 2 * LANES), BF16),
                        pltpu.VMEM((tm, d), BF16),
                        pltpu.VMEM((tm, d_ff), BF16)],
        compiler_params=pltpu.CompilerParams(
            dimension_semantics=("arbitrary", "arbitrary"), vmem_limit_bytes=VMEM_LIMIT),
        name="prompt_layer1",
    )(sinks, x, k, v, mod, rope, *consts)


def _sample_l0_kernel(x_ref, mod_ref, qmod_ref, h0_ref, cst_ref, rope_ref, ng_ref, w_in_ref, cw_ref,
                      cb_ref, gw_ref, gb_ref, lam_ref, w_out_ref, fin_ref, fout_ref, kvg_ref, wkv_ref,
                      qg_ref, wq_ref,
                      x2_ref, k_ref, v_ref, q_ref, conv_ref, hl_ref,
                      xr_s, yg_s, a_s, u_s, o_s, act_s):
    nt, sb, d = x_ref.shape
    rows = nt * sb
    slab = lambda t: slice(t * sb, (t + 1) * sb)
    mod = lambda i: _tile_rows(mod_ref[i], nt)

    x = x_ref[...].reshape(rows, d)
    h = _rms_mod(x, ng_ref[0:1, :], mod(1), mod(0)).astype(BF16)
    _proj_in(h, w_in_ref, xr_s, yg_s)

    def conv_in(j):
        return cst_ref[j] if j < CONV_WIDTH - 1 else xr_s[slab(j - (CONV_WIDTH - 1)), :]

    xc_slabs = []
    for t in range(nt):
        acc = cb_ref[...]
        for j in range(CONV_WIDTH):
            acc = acc + cw_ref[j:j + 1, :] * conv_in(t + j)
        xc_slabs.append(acc)
    xc = jnp.concatenate(xc_slabs, axis=0)
    for j in range(CONV_WIDTH - 1):
        conv_ref[j] = xr_s[slab(nt - (CONV_WIDTH - 1) + j), :]

    _rglru_gates(xc, gw_ref, gb_ref, lam_ref, a_s, u_s)

    hs = h0_ref[...]
    for t in range(nt):
        hs = a_s[slab(t), :] * hs + u_s[slab(t), :]
        o_s[slab(t), :] = hs
    hl_ref[...] = hs

    z = (o_s[...] * _gelu_tanh(yg_s[...])).astype(BF16)
    x1 = x + mod(2) * _dot(z, w_out_ref[...])

    h2 = _rms_mod(x1, ng_ref[1:2, :], mod(4), mod(3)).astype(BF16)
    x2 = x1 + mod(5) * _swiglu(h2, fin_ref, fout_ref, act_s)
    x2_ref[...] = x2.reshape(nt, sb, d)

    hk = _rms_mod(x2, kvg_ref[...], mod(7), mod(6)).astype(BF16)
    kv = _dot(hk, wkv_ref[...])
    hq = _rms_mod(x2, qg_ref[...], _tile_rows(qmod_ref[1], nt), _tile_rows(qmod_ref[0], nt)).astype(BF16)
    q = _dot(hq, wq_ref[...])
    for t in range(nt):
        c, s_next, s_prev = rope_ref[0, t:t + 1, :], rope_ref[1, t:t + 1, :], rope_ref[2, t:t + 1, :]
        k_ref[t] = _rope(kv[slab(t), :KV_DIM], c, s_next, s_prev)
        q_ref[t] = _rope(q[slab(t), :], c, s_next, s_prev) * (HEAD_DIM ** -0.5)
    v_ref[...] = kv[:, KV_DIM:].reshape(nt, sb, KV_DIM)


def _sample_l0_call(x, mod, qmod, h0, cst, rope, ng, w_in, cw, cb, gw, gb, lam, w_out, fin, fout,
                    kvg, wkv, qg, wq):
    nt, nb, d = x.shape
    sb = SAMPLE_BATCH_TILE
    d_ff = fout.shape[0]
    rows = nt * sb
    slab_spec = lambda lead, w: pl.BlockSpec((lead, sb, w), lambda i: (0, i, 0))
    consts = (rope, ng, w_in, cw, cb, gw, gb, lam, w_out, fin, fout, kvg, wkv, qg, wq)
    return pl.pallas_call(
        _sample_l0_kernel,
        grid=(nb // sb,),
        in_specs=[slab_spec(nt, d), slab_spec(mod.shape[0], d), slab_spec(qmod.shape[0], d),
                  pl.BlockSpec((sb, d), lambda i: (i, 0)), slab_spec(CONV_WIDTH - 1, d)]
                 + [_const_spec(c.shape) for c in consts],
        out_specs=[slab_spec(nt, d), slab_spec(nt, KV_DIM), slab_spec(nt, KV_DIM), slab_spec(nt, d),
                   slab_spec(CONV_WIDTH - 1, d), pl.BlockSpec((sb, d), lambda i: (i, 0))],
        out_shape=[jax.ShapeDtypeStruct((nt, nb, d), F32),
                   jax.ShapeDtypeStruct((nt, nb, KV_DIM), F32),
                   jax.ShapeDtypeStruct((nt, nb, KV_DIM), F32),
                   jax.ShapeDtypeStruct((nt, nb, d), F32),
                   jax.ShapeDtypeStruct((CONV_WIDTH - 1, nb, d), F32),
                   jax.ShapeDtypeStruct((nb, d), F32)],
        scratch_shapes=[pltpu.VMEM((rows, d), F32)] * 5 + [pltpu.VMEM((rows, d_ff), BF16)],
        compiler_params=pltpu.CompilerParams(
            dimension_semantics=("arbitrary",), vmem_limit_bytes=VMEM_LIMIT),
        name="sample_layer0",
    )(x, mod, qmod, h0, cst, *consts)


def _sample_attn_kernel(sink_ref, q_ref, kn_ref, vn_ref, ck_ref, cv_ref,
                        attn_ref, ko_ref, vo_ref):
    sb, nt, d = q_ref.shape
    span = 2 * WINDOW

    tok = lax.broadcasted_iota(jnp.int32, (nt, span), 0)
    si = lax.broadcasted_iota(jnp.int32, (nt, span), 1)
    mask = (si >= tok) & (si <= tok + WINDOW)
    pad = jnp.zeros((WINDOW - nt, LANES), F32)
    ones = jnp.ones((span, LANES), BF16)

    sinks = [[sink_ref[g * GROUP + i] for i in range(GROUP)] for g in range(N_KV_HEADS)]

    def windows(b):
        kn, vn = kn_ref[b], vn_ref[b]
        ck, cv = ck_ref[b], cv_ref[b]
        ko_ref[b, 0:WINDOW - nt, :] = ck[nt:, :]
        ko_ref[b, WINDOW - nt:WINDOW, :] = kn
        vo_ref[b, 0:WINDOW - nt, :] = cv[nt:, :]
        vo_ref[b, WINDOW - nt:WINDOW, :] = vn
        kwin, vaug = [], []
        for pb in range(KV_DIM // LANES):
            blk = slice(pb * LANES, (pb + 1) * LANES)
            (kc, kn_d), (vc, vn_d) = [(_dup_halves(c[:, blk]), _dup_halves(n[:, blk]))
                                      for c, n in ((ck, kn), (cv, vn))]
            for i in range(2):
                kwin.append(jnp.concatenate([kc[i], kn_d[i], pad], axis=0).astype(BF16))
                vcol = jnp.concatenate([vc[i], vn_d[i], pad], axis=0).astype(BF16)
                vaug.append(jnp.concatenate([vcol, ones], axis=1))
        return kwin, vaug

    def sequences(it, carry):
        seqs = [it * SAMPLE_ATTN_UNROLL + u for u in range(SAMPLE_ATTN_UNROLL)]
        wins = [windows(b) for b in seqs]
        units = [(u, g) for u in range(SAMPLE_ATTN_UNROLL) for g in range(N_KV_HEADS)]
        q_split = [[_split_halves(q_ref[b][:, p * LANES:(p + 1) * LANES]) for p in range(d // LANES)]
                   for b in seqs]
        scores = [_attn_scores([q_split[u][2 * g + i // 2][i % 2] for i in range(GROUP)], wins[u][0][g])
                  for u, g in units]
        probs = [_attn_probs(s_all, mask, sinks[g]) for s_all, (u, g) in zip(scores, units)]
        values = [_attn_values(p_all, wins[u][1][g]) for (p_all, _), (u, g) in zip(probs, units)]
        pairs = [[] for _ in seqs]
        for res_all, (_, maxes), (u, g) in zip(values, probs, units):
            pairs[u] += _attn_finish(res_all, maxes, sinks[g])
        for u, b in enumerate(seqs):
            attn_ref[b] = jnp.concatenate(pairs[u], axis=1)
        return carry

    lax.fori_loop(0, sb // SAMPLE_ATTN_UNROLL, sequences, 0)


def _sample_attn_call(sinks, q, kn, vn, ck, cv):
    nb, nt, d = q.shape
    sb = SAMPLE_ATTN_BATCH
    seq_spec = lambda r, w: pl.BlockSpec((sb, r, w), lambda i: (i, 0, 0))
    return pl.pallas_call(
        _sample_attn_kernel,
        grid=(nb // sb,),
        in_specs=[pl.BlockSpec(memory_space=pltpu.SMEM),
                  seq_spec(nt, d), seq_spec(nt, KV_DIM), seq_spec(nt, KV_DIM),
                  seq_spec(WINDOW, KV_DIM), seq_spec(WINDOW, KV_DIM)],
        out_specs=[seq_spec(nt, d), seq_spec(WINDOW, KV_DIM), seq_spec(WINDOW, KV_DIM)],
        out_shape=[jax.ShapeDtypeStruct((nb, nt, d), F32),
                   jax.ShapeDtypeStruct((nb, WINDOW, KV_DIM), F32),
                   jax.ShapeDtypeStruct((nb, WINDOW, KV_DIM), F32)],
        compiler_params=pltpu.CompilerParams(
            dimension_semantics=("arbitrary",), vmem_limit_bytes=VMEM_LIMIT),
        name="sample_attention",
    )(sinks, q, kn, vn, ck, cv)


def _sample_l1_kernel(x_ref, attn_ref, mod_ref, ng_ref, wo_ref, fin_ref, fout_ref, fg_ref,
                      y_ref, act_s):
    nt, sb, d = x_ref.shape
    rows = nt * sb
    mod = lambda i: _tile_rows(mod_ref[i], nt)
    x = x_ref[...].reshape(rows, d)
    attn = attn_ref[...].reshape(rows, d).astype(BF16)
    x1 = x + mod(2) * _dot(attn, wo_ref[...])
    h2 = _rms_mod(x1, ng_ref[1:2, :], mod(4), mod(3)).astype(BF16)
    x2 = x1 + mod(5) * _swiglu(h2, fin_ref, fout_ref, act_s)
    y_ref[...] = (_rms(x2) * fg_ref[...]).reshape(nt, sb, d)


def _sample_l1_call(x, attn, mod, ng, wo, fin, fout, fg):
    nt, nb, d = x.shape
    sb = SAMPLE_BATCH_TILE
    d_ff = fout.shape[0]
    slab_spec = lambda lead: pl.BlockSpec((lead, sb, d), lambda i: (0, i, 0))
    consts = (ng, wo, fin, fout, fg)
    return pl.pallas_call(
        _sample_l1_kernel,
        grid=(nb // sb,),
        in_specs=[slab_spec(nt), slab_spec(nt), slab_spec(mod.shape[0])]
                 + [_const_spec(c.shape) for c in consts],
        out_specs=slab_spec(nt),
        out_shape=jax.ShapeDtypeStruct((nt, nb, d), F32),
        scratch_shapes=[pltpu.VMEM((nt * sb, d_ff), BF16)],
        compiler_params=pltpu.CompilerParams(
            dimension_semantics=("arbitrary",), vmem_limit_bytes=VMEM_LIMIT),
        name="sample_layer1",
    )(x, attn, mod, *consts)


def _rope_tables(pos):
    half = ROT_DIM // 2
    inv = ROPE_THETA ** (-jnp.arange(0, ROT_DIM, 2, dtype=F32) / ROT_DIM)
    ang = pos.astype(F32)[:, None] * inv[None, :]
    cos, sin = jnp.cos(ang), jnp.sin(ang)
    n = pos.shape[0]
    rest = HEAD_DIM - ROT_DIM
    c = jnp.concatenate([cos, cos, jnp.ones((n, rest), F32)], axis=1)
    s_next = jnp.concatenate([-sin, jnp.zeros((n, half + rest), F32)], axis=1)
    s_prev = jnp.concatenate([jnp.zeros((n, half), F32), sin, jnp.zeros((n, rest), F32)], axis=1)
    reps = LANES // HEAD_DIM
    return jnp.stack([jnp.tile(c, (1, reps)), jnp.tile(s_next, (1, reps)), jnp.tile(s_prev, (1, reps))])


def kernel(x_prompt, x_sample, c_prompt, c_sample, state_conv, state_h, cache_k, cache_v, ada_w, ada_b, norm_g, rnn_w_in, rnn_conv_w, rnn_conv_b, rnn_gate_w, rnn_gate_b, rnn_lambda, rnn_w_out, kv_ada_w, kv_ada_b, kv_norm_g, w_kv, attn_w_q, attn_sinks, attn_w_o, ffn_w_in, ffn_w_out, final_g):
    nb_p, t_p, d = x_prompt.shape
    nb_s, t_s, _ = x_sample.shape

    c_all = jnp.concatenate([c_prompt, c_sample], axis=0)
    ada = _ada_call(c_all, ada_w, ada_b)
    kv_ada = _ada_call(c_all, kv_ada_w[None], kv_ada_b[None])
    n_all = c_all.shape[0]
    mod_l0 = jnp.concatenate([ada[0].reshape(n_all, 6, d), kv_ada[0].reshape(n_all, 2, d)], axis=1)
    mod_l1 = ada[1].reshape(n_all, 6, d)

    bf = lambda w: w.astype(BF16)
    row = lambda v: v.reshape(1, -1)
    l0_consts = (norm_g[0], bf(rnn_w_in[0]), rnn_conv_w[0], row(rnn_conv_b[0]), bf(rnn_gate_w[0]),
                 rnn_gate_b[0], row(rnn_lambda[0]), bf(rnn_w_out[0]), bf(ffn_w_in[0]), bf(ffn_w_out[0]),
                 row(kv_norm_g), bf(w_kv))
    wq, wo = bf(attn_w_q[0]), bf(attn_w_o[0])
    fin1, fout1 = bf(ffn_w_in[1]), bf(ffn_w_out[1])

    rope_p = _rope_tables(jnp.arange(t_p, dtype=jnp.int32))
    x2_p, k_p, v_p, conv_p, hl_p = _prompt_l0_call(
        x_prompt, mod_l0[:nb_p, :, None, :], rope_p, *l0_consts)
    y_prompt = _prompt_l1_call(attn_sinks[0], x2_p, k_p, v_p, mod_l1[:nb_p, :, None, :], rope_p,
                               norm_g[1], wq, wo, fin1, fout1, row(final_g))

    rope_s = _rope_tables(PAST_LEN + jnp.arange(t_s, dtype=jnp.int32))
    mod_s0 = mod_l0[nb_p:].transpose(1, 0, 2)
    mod_s1 = mod_l1[nb_p:].transpose(1, 0, 2)
    x2_s, k_s, v_s, q_s, conv_s, hl_s = _sample_l0_call(
        x_sample.transpose(1, 0, 2), mod_s0, mod_s1[0:2], state_h[0], state_conv[0].transpose(1, 0, 2),
        rope_s, *l0_consts, norm_g[1, 0:1], wq)
    attn_s, ko_s, vo_s = _sample_attn_call(
        attn_sinks[0], q_s.transpose(1, 0, 2), k_s.transpose(1, 0, 2), v_s.transpose(1, 0, 2),
        cache_k.reshape(nb_s, WINDOW, KV_DIM), cache_v.reshape(nb_s, WINDOW, KV_DIM))
    y_s = _sample_l1_call(x2_s, attn_s.transpose(1, 0, 2), mod_s1, norm_g[1], wo, fin1, fout1,
                          row(final_g))

    kv_shape = (WINDOW, N_KV_HEADS, HEAD_DIM)
    return (y_prompt,
            y_s.transpose(1, 0, 2),
            conv_p[None, :, SUBLANES - (CONV_WIDTH - 1):, :],
            hl_p[None, :, 0, :],
            k_p[:, t_p - WINDOW:, :].reshape((nb_p,) + kv_shape),
            v_p[:, t_p - WINDOW:, :].reshape((nb_p,) + kv_shape),
            conv_s.transpose(1, 0, 2)[None],
            hl_s[None],
            ko_s.reshape((nb_s,) + kv_shape),
            vo_s.reshape((nb_s,) + kv_shape))
```

```python
import functools

import jax
import jax.numpy as jnp
from jax import lax
from jax.experimental import pallas as pl
from jax.experimental.pallas import tpu as pltpu

F32 = jnp.float32
BF16 = jnp.bfloat16

D_MODEL = 1024
N_RNN_BLOCKS = 8
RNN_BLOCK = D_MODEL // N_RNN_BLOCKS
CONV_WIDTH = 4
RG_C = 8.0
HEAD_DIM = 64
N_HEADS = D_MODEL // HEAD_DIM
N_KV_HEADS = 4
GROUP = N_HEADS // N_KV_HEADS
KV_DIM = N_KV_HEADS * HEAD_DIM
WINDOW = 128
ROT_DIM = HEAD_DIM // 4
ROPE_THETA = 500000.0
EPS = 1e-6
NEG_INF = -1e30
PAST_LEN = 16384

LANES = 128
SUBLANES = 8
MXU_COLS = 256
VMEM_LIMIT = 56 * 1024 * 1024

PROMPT_TILE = 256
FFN_SPLIT = 3
L1_FFN_SPLITS = (3, 7)
ADA_TILE_N = 2048
SAMPLE_ATTN_BATCH = 16
SAMPLE_ATTN_UNROLL = 4
SAMPLE_BATCH_TILE = 32


def _dot(a, b):
    return jnp.dot(a, b, preferred_element_type=F32)


def _dot_nt(a, b):
    return lax.dot_general(a, b, (((1,), (1,)), ((), ())), preferred_element_type=F32)


def _sigmoid(x):
    return 1.0 / (1.0 + jnp.exp(-x))


def _silu(x):
    return x * _sigmoid(x)


def _gelu_tanh(x):
    return x * (0.5 * (1.0 + jnp.tanh(0.7978845608028654 * (x + 0.044715 * (x * x * x)))))


def _log_sigmoid(x):
    return -(jnp.maximum(-x, 0.0) + jnp.log1p(jnp.exp(-jnp.abs(x))))


def _rms(x):
    return x * lax.rsqrt(jnp.mean(x * x, axis=-1, keepdims=True) + EPS)


def _rms_mod(x, g, scale, shift):
    return (_rms(x) * g) * (1.0 + scale) + shift


def _tile_rows(m, reps):
    return jnp.concatenate([m] * reps, axis=0)


def _rope_block(blk, c, s_next, s_prev):
    return blk * c + pltpu.roll(blk, LANES - ROT_DIM // 2, 1) * s_next + pltpu.roll(blk, ROT_DIM // 2, 1) * s_prev


def _rope(x, c, s_next, s_prev):
    blocks = [_rope_block(x[:, j * LANES:(j + 1) * LANES], c, s_next, s_prev)
              for j in range(x.shape[1] // LANES)]
    return jnp.concatenate(blocks, axis=1)


def _proj_in(h, w_in_ref, xr_s, yg_s):
    d = xr_s.shape[1]
    cw = 2 * MXU_COLS
    for c in range(d // cw):
        xr_s[:, c * cw:(c + 1) * cw] = _dot(h, w_in_ref[:, c * cw:(c + 1) * cw])
        yg_s[:, c * cw:(c + 1) * cw] = _dot(h, w_in_ref[:, d + c * cw:d + (c + 1) * cw])


def _rglru_gates(xc, gw_ref, gb_ref, lam_ref, a_s, u_s):
    xcb = xc.astype(BF16)
    cl = RG_C * _log_sigmoid(lam_ref[...])
    for n in range(N_RNN_BLOCKS):
        blk = slice(n * RNN_BLOCK, (n + 1) * RNN_BLOCK)
        g = _dot(xcb[:, blk], gw_ref[n]) + gb_ref[n:n + 1, :]
        r = _sigmoid(g[:, :RNN_BLOCK])
        i = _sigmoid(g[:, RNN_BLOCK:])
        log_a = cl[:, blk] * r
        a = jnp.exp(log_a)
        a_s[:, blk] = a
        u_s[:, blk] = jnp.sqrt(-jnp.tanh(log_a) * (a * a + 1.0)) * (i * xc[:, blk])


def _swiglu_in(h, fin_ref, act_s, chunks):
    d_ff = act_s.shape[1]
    for c in chunks:
        cols = slice(c * MXU_COLS, (c + 1) * MXU_COLS)
        gate = _dot(h, fin_ref[:, cols])
        up = _dot(h, fin_ref[:, d_ff + c * MXU_COLS:d_ff + (c + 1) * MXU_COLS])
        act_s[:, cols] = (_silu(gate) * up).astype(BF16)


def _swiglu(h, fin_ref, fout_ref, act_s):
    _swiglu_in(h, fin_ref, act_s, range(fout_ref.shape[0] // MXU_COLS))
    return _dot(act_s[...], fout_ref[...])


def _low_half(shape):
    return lax.broadcasted_iota(jnp.int32, shape, 1) < LANES // 2


def _dup_halves(blk):
    low = _low_half(blk.shape)
    rot = pltpu.roll(blk, LANES // 2, 1)
    return jnp.where(low, blk, rot), jnp.where(low, rot, blk)


def _split_halves(blk):
    low = _low_half(blk.shape)
    zero = jnp.zeros_like(blk)
    return jnp.where(low, blk, zero), jnp.where(low, zero, blk)


def _attn_scores(q_heads, kwin):
    return _dot_nt(jnp.concatenate(q_heads, axis=0).astype(BF16), kwin)


def _attn_probs(s_all, mask, sinks):
    rb = s_all.shape[0] // GROUP
    probs, maxes = [], []
    for i in range(GROUP):
        s = jnp.where(mask, s_all[i * rb:(i + 1) * rb, :], NEG_INF)
        mx = jnp.maximum(jnp.max(s, axis=-1, keepdims=True), sinks[i])
        probs.append(jnp.exp(s - mx))
        maxes.append(mx)
    return jnp.concatenate(probs, axis=0).astype(BF16), maxes


def _attn_values(p_all, vaug):
    return _dot(p_all, vaug)


def _attn_finish(res_all, maxes, sinks):
    rb = res_all.shape[0] // GROUP
    outs = []
    for i in range(GROUP):
        res = res_all[i * rb:(i + 1) * rb, :]
        den = res[:, LANES:] + jnp.exp(sinks[i] - maxes[i])
        outs.append(res[:, :LANES] * (1.0 / den))
    low = _low_half(outs[0].shape)
    return [jnp.where(low, outs[2 * i], outs[2 * i + 1]) for i in range(GROUP // 2)]


def _ada_kernel(c_ref, w_ref, b_ref, o_ref):
    c = c_ref[...]
    o_ref[0] = _dot(_silu(c).astype(BF16), w_ref[0].astype(BF16)) + b_ref[0]


def _ada_call(c, w, b):
    n_layers, d, n = w.shape
    r = c.shape[0]
    return pl.pallas_call(
        _ada_kernel,
        grid=(n_layers, n // ADA_TILE_N),
        in_specs=[pl.BlockSpec((r, d), lambda l, j: (0, 0)),
                  pl.BlockSpec((1, d, ADA_TILE_N), lambda l, j: (l, 0, j)),
                  pl.BlockSpec((1, 1, ADA_TILE_N), lambda l, j: (l, 0, j))],
        out_specs=pl.BlockSpec((1, r, ADA_TILE_N), lambda l, j: (l, 0, j)),
        out_shape=jax.ShapeDtypeStruct((n_layers, r, n), F32),
        compiler_params=pltpu.CompilerParams(
            dimension_semantics=("arbitrary", "arbitrary"), vmem_limit_bytes=VMEM_LIMIT),
        name="ada_mod",
    )(c, w, b.reshape(n_layers, 1, n))


def _prompt_l0_kernel(x_ref, mod_ref, kvmod_ref, rope_ref, ng_ref, w_in_ref, cw_ref, cb_ref, gw_ref,
                      gb_ref, lam_ref, w_out_ref, fin_ref, fout_ref, kvg_ref, wkv_ref,
                      x2_ref, k_ref, v_ref, conv_ref, hl_ref,
                      xr_s, yg_s, act_s, hist_s, hc_s, z_s, xk_s, *, nt, n_tiles):
    tm, d = x_ref.shape
    s = pl.program_id(0)
    slot = lax.rem(s, 2)
    prev = 1 - slot
    n_chunks = fout_ref.shape[0] // MXU_COLS
    b_cur = jnp.minimum(s, n_tiles - 1) // nt
    b_prev = jnp.maximum(s - 1, 0) // nt
    mod = lambda i: _mod_row(mod_ref, b_cur, i)
    modp = lambda i: _mod_row(mod_ref, b_prev, i)

    @pl.when(s == 0)
    def _():
        z_s[1] = jnp.zeros((tm, d), BF16)
        xk_s[1] = jnp.zeros((tm, d), F32)

    @pl.when(lax.rem(jnp.minimum(s, n_tiles - 1), nt) == 0)
    def _():
        hist_s[...] = jnp.zeros_like(hist_s)
        hc_s[...] = jnp.zeros_like(hc_s)

    out_prev = _dot(z_s[prev], w_out_ref[...])

    x = x_ref[...]
    h = _rms_mod(x, ng_ref[0:1, :], mod(1), mod(0)).astype(BF16)
    _proj_in(h, w_in_ref, xr_s, yg_s)
    xk_s[slot] = x

    x1 = xk_s[prev] + modp(2) * out_prev
    h2 = _rms_mod(x1, ng_ref[1:2, :], modp(4), modp(3)).astype(BF16)
    _swiglu_in(h2, fin_ref, act_s, range(0, FFN_SPLIT))

    xr = xr_s[...]
    hist = hist_s[...]
    row8 = lax.broadcasted_iota(jnp.int32, (SUBLANES, d), 0)

    def shifted(k):
        rolled = pltpu.roll(xr, k, 0)
        first = jnp.where(row8 >= k, rolled[0:SUBLANES], pltpu.roll(hist, k, 0))
        return jnp.concatenate([first, rolled[SUBLANES:]], axis=0)

    xc = cb_ref[...]
    for j in range(CONV_WIDTH - 1):
        xc = xc + cw_ref[j:j + 1, :] * shifted(CONV_WIDTH - 1 - j)
    xc = xc + cw_ref[CONV_WIDTH - 1:CONV_WIDTH, :] * xr
    hist_s[...] = xr[tm - SUBLANES:]
    xr_s[...] = xc

    cl = RG_C * _log_sigmoid(lam_ref[...])
    row8b = lax.broadcasted_iota(jnp.int32, (SUBLANES, RNN_BLOCK), 0)

    def recurrent_block(n):
        blk = slice(n * RNN_BLOCK, (n + 1) * RNN_BLOCK)
        xc_blk = xr_s[:, blk]
        gates = _dot(xc_blk.astype(BF16), gw_ref[n]) + gb_ref[n:n + 1, :]
        r = _sigmoid(gates[:, :RNN_BLOCK])
        i = _sigmoid(gates[:, RNN_BLOCK:])
        log_a = cl[:, blk] * r
        a_all = jnp.exp(log_a)
        u_all = jnp.sqrt(-jnp.tanh(log_a) * (a_all * a_all + 1.0)) * (i * xc_blk)
        hprev = hc_s[:, blk]
        hs_groups = []
        for g in range(tm // SUBLANES):
            a = a_all[g * SUBLANES:(g + 1) * SUBLANES, :]
            u = u_all[g * SUBLANES:(g + 1) * SUBLANES, :]
            for step in (1, 2, 4):
                keep = row8b >= step
                a_sh = jnp.where(keep, pltpu.roll(a, step, 0), 1.0)
                u_sh = jnp.where(keep, pltpu.roll(u, step, 0), 0.0)
                u = a * u_sh + u
                a = a * a_sh
            hs = a * hprev + u
            hprev = jnp.broadcast_to(hs[SUBLANES - 1:SUBLANES, :], hs.shape)
            hs_groups.append(hs)
        hc_s[:, blk] = hprev
        z_s[slot, :, blk] = (jnp.concatenate(hs_groups, axis=0) * _gelu_tanh(yg_s[:, blk])).astype(BF16)

    blocks = list(range(N_RNN_BLOCKS))
    for c in range(FFN_SPLIT, n_chunks):
        _swiglu_in(h2, fin_ref, act_s, [c])
        if blocks:
            recurrent_block(blocks.pop(0))
    ffn = _dot(act_s[...], fout_ref[...])
    for n in blocks:
        recurrent_block(n)

    @pl.when(s < n_tiles)
    def _():
        conv_ref[...] = hist_s[...]
        hl_ref[...] = hc_s[...]

    x2 = x1 + modp(5) * ffn
    x2_ref[...] = x2
    hk = _rms_mod(x2, kvg_ref[...], _mod_row(kvmod_ref, b_prev, 1), _mod_row(kvmod_ref, b_prev, 0)).astype(BF16)
    kv = _dot(hk, wkv_ref[...])
    k_ref[...] = _rope(kv[:, :KV_DIM], rope_ref[0], rope_ref[1], rope_ref[2])
    v_ref[...] = kv[:, KV_DIM:]


def _resident(arr, lead=None):
    if lead is None:
        index, shape = (0,) * arr.ndim, arr.shape
    else:
        index, shape = (lead,) + (0,) * (arr.ndim - 1), (None,) + arr.shape[1:]
    return pl.BlockSpec(shape, lambda *_: index, pipeline_mode=pl.Buffered(1))


def _prompt_mod_specs(ada, kv_ada, layer, row_block):
    return [_resident_rows(ada, layer, row_block), _resident_rows(kv_ada, 0, row_block)]


def _resident_rows(arr, lead, row_block):
    index = (lead, row_block, 0)
    return pl.BlockSpec((None, SUBLANES, arr.shape[2]), lambda *_: index, pipeline_mode=pl.Buffered(1))


def _mod_row(ref, b, i):
    return ref[pl.ds(b, 1), i * D_MODEL:(i + 1) * D_MODEL]


def _prompt_l0_call(x, ada, kv_ada, mod_row_block, rope, consts):
    nb, t, d = x.shape
    tm = PROMPT_TILE
    nt = t // tm
    n_tiles = nb * nt
    d_ff = consts[9][0].shape[1]
    cur = lambda s: jnp.minimum(s, n_tiles - 1)
    prv = lambda s: jnp.maximum(s - 1, 0)
    tile_spec = lambda w, f: pl.BlockSpec((None, tm, w), lambda s: (f(s) // nt, lax.rem(f(s), nt), 0))
    state_spec = pl.BlockSpec((None, SUBLANES, d), lambda s: (cur(s) // nt, 0, 0))
    return pl.pallas_call(
        functools.partial(_prompt_l0_kernel, nt=nt, n_tiles=n_tiles),
        grid=(n_tiles + 1,),
        in_specs=[tile_spec(d, cur)] + _prompt_mod_specs(ada, kv_ada, 0, mod_row_block)
                 + [pl.BlockSpec((3, tm, LANES), lambda s: (0, lax.rem(prv(s), nt), 0))]
                 + [_resident(*c) for c in consts],
        out_specs=[tile_spec(d, prv), tile_spec(KV_DIM, prv), tile_spec(KV_DIM, prv),
                   state_spec, state_spec],
        out_shape=[jax.ShapeDtypeStruct((nb, t, d), F32),
                   jax.ShapeDtypeStruct((nb, t, KV_DIM), F32),
                   jax.ShapeDtypeStruct((nb, t, KV_DIM), F32),
                   jax.ShapeDtypeStruct((nb, SUBLANES, d), F32),
                   jax.ShapeDtypeStruct((nb, SUBLANES, d), F32)],
        scratch_shapes=[pltpu.VMEM((tm, d), F32), pltpu.VMEM((tm, d), F32),
                        pltpu.VMEM((tm, d_ff), BF16),
                        pltpu.VMEM((SUBLANES, d), F32), pltpu.VMEM((SUBLANES, d), F32),
                        pltpu.VMEM((2, tm, d), BF16), pltpu.VMEM((2, tm, d), F32)],
        compiler_params=pltpu.CompilerParams(
            dimension_semantics=("arbitrary",), vmem_limit_bytes=VMEM_LIMIT),
        name="prompt_layer0",
    )(x, ada, kv_ada, rope, *[c[0] for c in consts])


def _prompt_l1_kernel(sink_ref, x_ref, k_ref, v_ref, mod_ref, rope_ref, ng_ref, wq_ref, wo_ref,
                      fin_ref, fout_ref, fg_ref,
                      y_ref,
                      kw_s, vw_s, attn_s, act_s, xk_s, *, nt, n_tiles):
    tm, d = x_ref.shape
    s = pl.program_id(0)
    slot = lax.rem(s, 2)
    prev = 1 - slot
    n_chunks = fout_ref.shape[0] // MXU_COLS
    cur_tile = jnp.minimum(s, n_tiles - 1)
    t = lax.rem(cur_tile, nt)
    b_cur = cur_tile // nt
    b_prev = jnp.maximum(s - 1, 0) // nt
    mod = lambda i: _mod_row(mod_ref, b_cur, i)
    modp = lambda i: _mod_row(mod_ref, b_prev, i)

    @pl.when(s == 0)
    def _():
        attn_s[1] = jnp.zeros((tm, d), BF16)
        xk_s[1] = jnp.zeros((tm, d), F32)

    @pl.when(t == 0)
    def _():
        kw_s[0:WINDOW, :] = jnp.zeros((WINDOW, kw_s.shape[1]), BF16)
        vw_s[0:WINDOW, :] = jnp.zeros((WINDOW, vw_s.shape[1]), BF16)

    @pl.when(t > 0)
    def _():
        kw_s[0:WINDOW, :] = kw_s[tm:tm + WINDOW, :]
        vw_s[0:WINDOW, :] = vw_s[tm:tm + WINDOW, :]

    out_prev = _dot(attn_s[prev], wo_ref[...])

    x = x_ref[...]
    h = _rms_mod(x, ng_ref[0:1, :], mod(1), mod(0)).astype(BF16)
    q = _rope(_dot(h, wq_ref[...]), rope_ref[0], rope_ref[1], rope_ref[2]) * (HEAD_DIM ** -0.5)
    q_split = [_split_halves(q[:, p * LANES:(p + 1) * LANES]) for p in range(d // LANES)]
    xk_s[slot] = x

    new_rows = slice(WINDOW, WINDOW + tm)
    k, v = k_ref[...], v_ref[...]
    ones = jnp.ones((tm, LANES), BF16)
    for pb in range(KV_DIM // LANES):
        k_dup = _dup_halves(k[:, pb * LANES:(pb + 1) * LANES])
        v_dup = _dup_halves(v[:, pb * LANES:(pb + 1) * LANES])
        for i in range(2):
            g = 2 * pb + i
            kw_s[new_rows, g * LANES:(g + 1) * LANES] = k_dup[i].astype(BF16)
            vw_s[new_rows, 2 * g * LANES:(2 * g + 1) * LANES] = v_dup[i].astype(BF16)
            vw_s[new_rows, (2 * g + 1) * LANES:(2 * g + 2) * LANES] = ones

    x1 = xk_s[prev] + modp(2) * out_prev
    h2 = _rms_mod(x1, ng_ref[1:2, :], modp(4), modp(3)).astype(BF16)
    _swiglu_in(h2, fin_ref, act_s, range(0, L1_FFN_SPLITS[0]))

    span = 2 * WINDOW
    qi = lax.broadcasted_iota(jnp.int32, (WINDOW, span), 0)
    si = lax.broadcasted_iota(jnp.int32, (WINDOW, span), 1)
    band = (si >= qi) & (si <= qi + WINDOW)
    masks = [band & (si >= WINDOW - (t * tm + j * WINDOW)) for j in range(tm // WINDOW)]
    units = [(j, g) for j in range(tm // WINDOW) for g in range(N_KV_HEADS)]
    rows = lambda j: slice(j * WINDOW, (j + 1) * WINDOW)
    win = lambda j: slice(j * WINDOW, j * WINDOW + span)
    sinks = [[sink_ref[0, g * GROUP + i] for i in range(GROUP)] for g in range(N_KV_HEADS)]
    scores = [_attn_scores([q_split[2 * g + i // 2][i % 2][rows(j), :] for i in range(GROUP)],
                           kw_s[win(j), g * LANES:(g + 1) * LANES]) for j, g in units]
    _swiglu_in(h2, fin_ref, act_s, range(L1_FFN_SPLITS[0], L1_FFN_SPLITS[1]))
    probs = [_attn_probs(s_all, masks[j], sinks[g]) for s_all, (j, g) in zip(scores, units)]
    values = [_attn_values(p_all, vw_s[win(j), 2 * g * LANES:(2 * g + 2) * LANES])
              for (p_all, _), (j, g) in zip(probs, units)]
    _swiglu_in(h2, fin_ref, act_s, range(L1_FFN_SPLITS[1], n_chunks))
    for res_all, (_, maxes), (j, g) in zip(values, probs, units):
        for i, pair in enumerate(_attn_finish(res_all, maxes, sinks[g])):
            col = (2 * g + i) * LANES
            attn_s[slot, rows(j), col:col + LANES] = pair.astype(BF16)

    x2 = x1 + modp(5) * _dot(act_s[...], fout_ref[...])
    y_ref[...] = _rms(x2) * fg_ref[...]


def _prompt_l1_call(sinks, x, k, v, ada, mod_row_block, rope, consts):
    nb, t, d = x.shape
    tm = PROMPT_TILE
    nt = t // tm
    n_tiles = nb * nt
    d_ff = consts[4][0].shape[1]
    cur = lambda s: jnp.minimum(s, n_tiles - 1)
    prv = lambda s: jnp.maximum(s - 1, 0)
    tile_spec = lambda w, f: pl.BlockSpec((None, tm, w), lambda s: (f(s) // nt, lax.rem(f(s), nt), 0))
    return pl.pallas_call(
        functools.partial(_prompt_l1_kernel, nt=nt, n_tiles=n_tiles),
        grid=(n_tiles + 1,),
        in_specs=[pl.BlockSpec(memory_space=pltpu.SMEM),
                  tile_spec(d, cur), tile_spec(KV_DIM, cur), tile_spec(KV_DIM, cur),
                  _resident_rows(ada, 1, mod_row_block),
                  pl.BlockSpec((3, tm, LANES), lambda s: (0, lax.rem(cur(s), nt), 0))]
                 + [_resident(*c) for c in consts],
        out_specs=tile_spec(d, prv),
        out_shape=jax.ShapeDtypeStruct((nb, t, d), F32),
        scratch_shapes=[pltpu.VMEM((WINDOW + tm, N_KV_HEADS * LANES), BF16),
                        pltpu.VMEM((WINDOW + tm, N_KV_HEADS * 2 * LANES), BF16),
                        pltpu.VMEM((2, tm, d), BF16),
                        pltpu.VMEM((tm, d_ff), BF16),
                        pltpu.VMEM((2, tm, d), F32)],
        compiler_params=pltpu.CompilerParams(
            dimension_semantics=("arbitrary",), vmem_limit_bytes=VMEM_LIMIT),
        name="prompt_layer1",
    )(sinks, x, k, v, ada, rope, *[c[0] for c in consts])


def _sample_l0_kernel(x_ref, mod_ref, kvmod_ref, qmod_ref, h0_ref, cst_ref, rope_ref, ng_ref, w_in_ref,
                      cw_ref, cb_ref, gw_ref, gb_ref, lam_ref, w_out_ref, fin_ref, fout_ref, kvg_ref,
                      wkv_ref, qg_ref, wq_ref,
                      x2_ref, k_ref, v_ref, q_ref, conv_ref, hl_ref,
                      xr_s, yg_s, a_s, u_s, o_s, act_s):
    nt, sb, d = x_ref.shape
    rows = nt * sb
    slab = lambda t: slice(t * sb, (t + 1) * sb)
    vec = lambda ref, i: _tile_rows(ref[:, i * d:(i + 1) * d], nt)
    mod = lambda i: vec(mod_ref, i)

    x = x_ref[...].reshape(rows, d)
    h = _rms_mod(x, ng_ref[0:1, :], mod(1), mod(0)).astype(BF16)
    _proj_in(h, w_in_ref, xr_s, yg_s)

    def conv_in(j):
        return cst_ref[j] if j < CONV_WIDTH - 1 else xr_s[slab(j - (CONV_WIDTH - 1)), :]

    xc_slabs = []
    for t in range(nt):
        acc = cb_ref[...]
        for j in range(CONV_WIDTH):
            acc = acc + cw_ref[j:j + 1, :] * conv_in(t + j)
        xc_slabs.append(acc)
    xc = jnp.concatenate(xc_slabs, axis=0)
    for j in range(CONV_WIDTH - 1):
        conv_ref[j] = xr_s[slab(nt - (CONV_WIDTH - 1) + j), :]

    _rglru_gates(xc, gw_ref, gb_ref, lam_ref, a_s, u_s)

    hs = h0_ref[...]
    for t in range(nt):
        hs = a_s[slab(t), :] * hs + u_s[slab(t), :]
        o_s[slab(t), :] = hs
    hl_ref[...] = hs

    z = (o_s[...] * _gelu_tanh(yg_s[...])).astype(BF16)
    x1 = x + mod(2) * _dot(z, w_out_ref[...])

    h2 = _rms_mod(x1, ng_ref[1:2, :], mod(4), mod(3)).astype(BF16)
    x2 = x1 + mod(5) * _swiglu(h2, fin_ref, fout_ref, act_s)
    x2_ref[...] = x2.reshape(nt, sb, d)

    hk = _rms_mod(x2, kvg_ref[...], vec(kvmod_ref, 1), vec(kvmod_ref, 0)).astype(BF16)
    kv = _dot(hk, wkv_ref[...])
    hq = _rms_mod(x2, qg_ref[0:1, :], vec(qmod_ref, 1), vec(qmod_ref, 0)).astype(BF16)
    q = _dot(hq, wq_ref[...])
    for t in range(nt):
        c, s_next, s_prev = rope_ref[0, t:t + 1, :], rope_ref[1, t:t + 1, :], rope_ref[2, t:t + 1, :]
        k_ref[t] = _rope(kv[slab(t), :KV_DIM], c, s_next, s_prev)
        q_ref[t] = _rope(q[slab(t), :], c, s_next, s_prev) * (HEAD_DIM ** -0.5)
    v_ref[...] = kv[:, KV_DIM:].reshape(nt, sb, KV_DIM)


def _sample_mod_spec(arr, lead, sb):
    return pl.BlockSpec((None, sb, arr.shape[2]), lambda i: (lead, i, 0))


def _sample_l0_call(x, ada, kv_ada, h0, cst, consts):
    nt, nb, d = x.shape
    sb = SAMPLE_BATCH_TILE
    d_ff = consts[10][0].shape[1]
    rows = nt * sb
    slab_spec = lambda lead, w: pl.BlockSpec((lead, sb, w), lambda i: (0, i, 0))
    return pl.pallas_call(
        _sample_l0_kernel,
        grid=(nb // sb,),
        in_specs=[slab_spec(nt, d), _sample_mod_spec(ada, 0, sb), _sample_mod_spec(kv_ada, 0, sb),
                  _sample_mod_spec(ada, 1, sb),
                  pl.BlockSpec((sb, d), lambda i: (i, 0)), slab_spec(CONV_WIDTH - 1, d)]
                 + [_resident(*c) for c in consts],
        out_specs=[slab_spec(nt, d), slab_spec(nt, KV_DIM), slab_spec(nt, KV_DIM), slab_spec(nt, d),
                   slab_spec(CONV_WIDTH - 1, d), pl.BlockSpec((sb, d), lambda i: (i, 0))],
        out_shape=[jax.ShapeDtypeStruct((nt, nb, d), F32),
                   jax.ShapeDtypeStruct((nt, nb, KV_DIM), F32),
                   jax.ShapeDtypeStruct((nt, nb, KV_DIM), F32),
                   jax.ShapeDtypeStruct((nt, nb, d), F32),
                   jax.ShapeDtypeStruct((CONV_WIDTH - 1, nb, d), F32),
                   jax.ShapeDtypeStruct((nb, d), F32)],
        scratch_shapes=[pltpu.VMEM((rows, d), F32)] * 5 + [pltpu.VMEM((rows, d_ff), BF16)],
        compiler_params=pltpu.CompilerParams(
            dimension_semantics=("arbitrary",), vmem_limit_bytes=VMEM_LIMIT),
        name="sample_layer0",
    )(x, ada, kv_ada, ada, h0, cst, *[c[0] for c in consts])


def _sample_attn_kernel(sink_ref, q_ref, kn_ref, vn_ref, ck_ref, cv_ref,
                        attn_ref, ko_ref, vo_ref):
    sb, nt, d = q_ref.shape
    span = 2 * WINDOW

    tok = lax.broadcasted_iota(jnp.int32, (nt, span), 0)
    si = lax.broadcasted_iota(jnp.int32, (nt, span), 1)
    mask = (si >= tok) & (si <= tok + WINDOW)
    pad = jnp.zeros((WINDOW - nt, LANES), F32)
    ones = jnp.ones((span, LANES), BF16)

    sinks = [[sink_ref[0, g * GROUP + i] for i in range(GROUP)] for g in range(N_KV_HEADS)]

    def windows(b):
        kn, vn = kn_ref[b], vn_ref[b]
        ck, cv = ck_ref[b], cv_ref[b]
        ko_ref[b, 0:WINDOW - nt, :] = ck[nt:, :]
        ko_ref[b, WINDOW - nt:WINDOW, :] = kn
        vo_ref[b, 0:WINDOW - nt, :] = cv[nt:, :]
        vo_ref[b, WINDOW - nt:WINDOW, :] = vn
        kwin, vaug = [], []
        for pb in range(KV_DIM // LANES):
            blk = slice(pb * LANES, (pb + 1) * LANES)
            (kc, kn_d), (vc, vn_d) = [(_dup_halves(c[:, blk]), _dup_halves(n[:, blk]))
                                      for c, n in ((ck, kn), (cv, vn))]
            for i in range(2):
                kwin.append(jnp.concatenate([kc[i], kn_d[i], pad], axis=0).astype(BF16))
                vcol = jnp.concatenate([vc[i], vn_d[i], pad], axis=0).astype(BF16)
                vaug.append(jnp.concatenate([vcol, ones], axis=1))
        return kwin, vaug

    def sequences(it, carry):
        seqs = [it * SAMPLE_ATTN_UNROLL + u for u in range(SAMPLE_ATTN_UNROLL)]
        wins = [windows(b) for b in seqs]
        units = [(u, g) for u in range(SAMPLE_ATTN_UNROLL) for g in range(N_KV_HEADS)]
        q_split = [[_split_halves(q_ref[b][:, p * LANES:(p + 1) * LANES]) for p in range(d // LANES)]
                   for b in seqs]
        scores = [_attn_scores([q_split[u][2 * g + i // 2][i % 2] for i in range(GROUP)], wins[u][0][g])
                  for u, g in units]
        probs = [_attn_probs(s_all, mask, sinks[g]) for s_all, (u, g) in zip(scores, units)]
        values = [_attn_values(p_all, wins[u][1][g]) for (p_all, _), (u, g) in zip(probs, units)]
        pairs = [[] for _ in seqs]
        for res_all, (_, maxes), (u, g) in zip(values, probs, units):
            pairs[u] += _attn_finish(res_all, maxes, sinks[g])
        for u, b in enumerate(seqs):
            attn_ref[b] = jnp.concatenate(pairs[u], axis=1)
        return carry

    lax.fori_loop(0, sb // SAMPLE_ATTN_UNROLL, sequences, 0)


def _sample_attn_call(sinks, q, kn, vn, ck, cv):
    nb, nt, d = q.shape
    sb = SAMPLE_ATTN_BATCH
    seq_spec = lambda r, w: pl.BlockSpec((sb, r, w), lambda i: (i, 0, 0))
    return pl.pallas_call(
        _sample_attn_kernel,
        grid=(nb // sb,),
        in_specs=[pl.BlockSpec(memory_space=pltpu.SMEM),
                  seq_spec(nt, d), seq_spec(nt, KV_DIM), seq_spec(nt, KV_DIM),
                  seq_spec(WINDOW, KV_DIM), seq_spec(WINDOW, KV_DIM)],
        out_specs=[seq_spec(nt, d), seq_spec(WINDOW, KV_DIM), seq_spec(WINDOW, KV_DIM)],
        out_shape=[jax.ShapeDtypeStruct((nb, nt, d), F32),
                   jax.ShapeDtypeStruct((nb, WINDOW, KV_DIM), F32),
                   jax.ShapeDtypeStruct((nb, WINDOW, KV_DIM), F32)],
        compiler_params=pltpu.CompilerParams(
            dimension_semantics=("arbitrary",), vmem_limit_bytes=VMEM_LIMIT),
        name="sample_attention",
    )(sinks, q, kn, vn, ck, cv)


def _sample_l1_kernel(x_ref, attn_ref, mod_ref, ng_ref, wo_ref, fin_ref, fout_ref, fg_ref,
                      y_ref, act_s):
    nt, sb, d = x_ref.shape
    rows = nt * sb
    mod = lambda i: _tile_rows(mod_ref[:, i * d:(i + 1) * d], nt)
    x = x_ref[...].reshape(rows, d)
    attn = attn_ref[...].reshape(rows, d).astype(BF16)
    x1 = x + mod(2) * _dot(attn, wo_ref[...])
    h2 = _rms_mod(x1, ng_ref[1:2, :], mod(4), mod(3)).astype(BF16)
    x2 = x1 + mod(5) * _swiglu(h2, fin_ref, fout_ref, act_s)
    y_ref[...] = (_rms(x2) * fg_ref[...]).reshape(nt, sb, d)


def _sample_l1_call(x, attn, ada, consts):
    nt, nb, d = x.shape
    sb = SAMPLE_BATCH_TILE
    d_ff = consts[3][0].shape[1]
    slab_spec = lambda lead: pl.BlockSpec((lead, sb, d), lambda i: (0, i, 0))
    return pl.pallas_call(
        _sample_l1_kernel,
        grid=(nb // sb,),
        in_specs=[slab_spec(nt), slab_spec(nt), _sample_mod_spec(ada, 1, sb)]
                 + [_resident(*c) for c in consts],
        out_specs=slab_spec(nt),
        out_shape=jax.ShapeDtypeStruct((nt, nb, d), F32),
        scratch_shapes=[pltpu.VMEM((nt * sb, d_ff), BF16)],
        compiler_params=pltpu.CompilerParams(
            dimension_semantics=("arbitrary",), vmem_limit_bytes=VMEM_LIMIT),
        name="sample_layer1",
    )(x, attn, ada, *[c[0] for c in consts])


def _rope_tables(pos):
    half = ROT_DIM // 2
    inv = ROPE_THETA ** (-jnp.arange(0, ROT_DIM, 2, dtype=F32) / ROT_DIM)
    ang = pos.astype(F32)[:, None] * inv[None, :]
    cos, sin = jnp.cos(ang), jnp.sin(ang)
    n = pos.shape[0]
    rest = HEAD_DIM - ROT_DIM
    c = jnp.concatenate([cos, cos, jnp.ones((n, rest), F32)], axis=1)
    s_next = jnp.concatenate([-sin, jnp.zeros((n, half + rest), F32)], axis=1)
    s_prev = jnp.concatenate([jnp.zeros((n, half), F32), sin, jnp.zeros((n, rest), F32)], axis=1)
    reps = LANES // HEAD_DIM
    return jnp.stack([jnp.tile(c, (1, reps)), jnp.tile(s_next, (1, reps)), jnp.tile(s_prev, (1, reps))])


def kernel(x_prompt, x_sample, c_prompt, c_sample, state_conv, state_h, cache_k, cache_v, ada_w, ada_b, norm_g, rnn_w_in, rnn_conv_w, rnn_conv_b, rnn_gate_w, rnn_gate_b, rnn_lambda, rnn_w_out, kv_ada_w, kv_ada_b, kv_norm_g, w_kv, attn_w_q, attn_sinks, attn_w_o, ffn_w_in, ffn_w_out, final_g):
    nb_p, t_p, d = x_prompt.shape
    nb_s, t_s, _ = x_sample.shape

    assert rnn_w_in.shape[0] == 1 and attn_w_q.shape[0] == 1 and nb_s % SUBLANES == 0

    c_all = jnp.concatenate([c_sample, c_prompt], axis=0)
    ada = _ada_call(c_all, ada_w, ada_b)
    kv_ada = _ada_call(c_all, kv_ada_w[None], kv_ada_b[None])
    prompt_row_block = nb_s // SUBLANES

    bf = lambda w: w.astype(BF16)
    row = lambda v: v.reshape(1, -1)
    ffn_in, ffn_out = bf(ffn_w_in), bf(ffn_w_out)
    wq, wo = (bf(attn_w_q), 0), (bf(attn_w_o), 0)
    l0_consts = [(norm_g, 0), (bf(rnn_w_in), 0), (rnn_conv_w, 0), (rnn_conv_b, None), (bf(rnn_gate_w), 0),
                 (rnn_gate_b, 0), (rnn_lambda, None), (bf(rnn_w_out), 0), (ffn_in, 0), (ffn_out, 0),
                 (row(kv_norm_g), None), (bf(w_kv), None)]
    l1_ffn = [(ffn_in, 1), (ffn_out, 1), (row(final_g), None)]

    rope_p = _rope_tables(jnp.arange(t_p, dtype=jnp.int32))
    x2_p, k_p, v_p, conv_p, hl_p = _prompt_l0_call(x_prompt, ada, kv_ada, prompt_row_block, rope_p, l0_consts)
    y_prompt = _prompt_l1_call(attn_sinks, x2_p, k_p, v_p, ada, prompt_row_block, rope_p,
                               [(norm_g, 1), wq, wo] + l1_ffn)

    rope_s = _rope_tables(PAST_LEN + jnp.arange(t_s, dtype=jnp.int32))
    x2_s, k_s, v_s, q_s, conv_s, hl_s = _sample_l0_call(
        x_sample.transpose(1, 0, 2), ada, kv_ada, state_h[0], state_conv[0].transpose(1, 0, 2),
        [(rope_s, None)] + l0_consts + [(norm_g, 1), wq])
    attn_s, ko_s, vo_s = _sample_attn_call(
        attn_sinks, q_s.transpose(1, 0, 2), k_s.transpose(1, 0, 2), v_s.transpose(1, 0, 2),
        cache_k.reshape(nb_s, WINDOW, KV_DIM), cache_v.reshape(nb_s, WINDOW, KV_DIM))
    y_s = _sample_l1_call(x2_s, attn_s.transpose(1, 0, 2), ada, [(norm_g, 1), wo] + l1_ffn)

    kv_shape = (WINDOW, N_KV_HEADS, HEAD_DIM)
    return (y_prompt,
            y_s.transpose(1, 0, 2),
            conv_p[None, :, SUBLANES - (CONV_WIDTH - 1):, :],
            hl_p[None, :, 0, :],
            k_p[:, t_p - WINDOW:, :].reshape((nb_p,) + kv_shape),
            v_p[:, t_p - WINDOW:, :].reshape((nb_p,) + kv_shape),
            conv_s.transpose(1, 0, 2)[None],
            hl_s[None],
            ko_s.reshape((nb_s,) + kv_shape),
            vo_s.reshape((nb_s,) + kv_shape))
```

```python
import functools

import jax
import jax.numpy as jnp
from jax import lax
from jax.experimental import pallas as pl
from jax.experimental.pallas import tpu as pltpu

F32 = jnp.float32
BF16 = jnp.bfloat16

D_MODEL = 1024
N_RNN_BLOCKS = 8
RNN_BLOCK = D_MODEL // N_RNN_BLOCKS
CONV_WIDTH = 4
RG_C = 8.0
HEAD_DIM = 64
N_HEADS = D_MODEL // HEAD_DIM
N_KV_HEADS = 4
GROUP = N_HEADS // N_KV_HEADS
KV_DIM = N_KV_HEADS * HEAD_DIM
WINDOW = 128
ROT_DIM = HEAD_DIM // 4
ROPE_THETA = 500000.0
EPS = 1e-6
NEG_INF = -1e30
LOG2_E = 1.4426950408889634
PAST_LEN = 16384

LANES = 128
SUBLANES = 8
MXU_COLS = 256
VMEM_LIMIT = 56 * 1024 * 1024

PROMPT_TILE = 256
FFN_SPLIT = 3
L1_FFN_SPLITS = (3, 7)
ADA_TILE_N = 2048
SAMPLE_ATTN_BATCH = 16
SAMPLE_ATTN_UNROLL = 4
SAMPLE_BATCH_TILE = 32


def _dot(a, b):
    return jnp.dot(a, b, preferred_element_type=F32)


def _dot_nt(a, b):
    return lax.dot_general(a, b, (((1,), (1,)), ((), ())), preferred_element_type=F32)


def _sigmoid(x):
    return 1.0 / (1.0 + jnp.exp2(x * (-LOG2_E)))


def _silu(x):
    return x * _sigmoid(x)


def _gelu_tanh(x):
    return x * (0.5 * (1.0 + jnp.tanh(0.7978845608028654 * (x + 0.044715 * (x * x * x)))))


def _log_sigmoid(x):
    return -(jnp.maximum(-x, 0.0) + jnp.log1p(jnp.exp(-jnp.abs(x))))


def _rms(x):
    return x * lax.rsqrt(jnp.mean(x * x, axis=-1, keepdims=True) + EPS)


def _rms_mod(x, g, scale, shift):
    return (_rms(x) * g) * (1.0 + scale) + shift


def _tile_rows(m, reps):
    return jnp.concatenate([m] * reps, axis=0)


def _rope_block(blk, c, s_next, s_prev):
    return blk * c + pltpu.roll(blk, LANES - ROT_DIM // 2, 1) * s_next + pltpu.roll(blk, ROT_DIM // 2, 1) * s_prev


def _rope(x, c, s_next, s_prev):
    blocks = [_rope_block(x[:, j * LANES:(j + 1) * LANES], c, s_next, s_prev)
              for j in range(x.shape[1] // LANES)]
    return jnp.concatenate(blocks, axis=1)


def _proj_in(h, w_in_ref, xr_s, yg_s):
    d = xr_s.shape[1]
    cw = 2 * MXU_COLS
    for c in range(d // cw):
        xr_s[:, c * cw:(c + 1) * cw] = _dot(h, w_in_ref[:, c * cw:(c + 1) * cw])
        yg_s[:, c * cw:(c + 1) * cw] = _dot(h, w_in_ref[:, d + c * cw:d + (c + 1) * cw])


def _rglru_gates(xc, gw_ref, gb_ref, lam_ref, a_s, u_s):
    xcb = xc.astype(BF16)
    cl = RG_C * _log_sigmoid(lam_ref[...])
    for n in range(N_RNN_BLOCKS):
        blk = slice(n * RNN_BLOCK, (n + 1) * RNN_BLOCK)
        g = _dot(xcb[:, blk], gw_ref[n]) + gb_ref[n:n + 1, :]
        r = _sigmoid(g[:, :RNN_BLOCK])
        i = _sigmoid(g[:, RNN_BLOCK:])
        log_a = cl[:, blk] * r
        a = jnp.exp(log_a)
        a_s[:, blk] = a
        u_s[:, blk] = jnp.sqrt(-jnp.tanh(log_a) * (a * a + 1.0)) * (i * xc[:, blk])


def _swiglu_in(h, fin_ref, act_s, chunks):
    d_ff = act_s.shape[1]
    for c in chunks:
        cols = slice(c * MXU_COLS, (c + 1) * MXU_COLS)
        gate = _dot(h, fin_ref[:, cols])
        up = _dot(h, fin_ref[:, d_ff + c * MXU_COLS:d_ff + (c + 1) * MXU_COLS])
        act_s[:, cols] = (_silu(gate) * up).astype(BF16)


def _swiglu(h, fin_ref, fout_ref, act_s):
    _swiglu_in(h, fin_ref, act_s, range(fout_ref.shape[0] // MXU_COLS))
    return _dot(act_s[...], fout_ref[...])


def _low_half(shape):
    return lax.broadcasted_iota(jnp.int32, shape, 1) < LANES // 2


def _dup_halves(blk):
    low = _low_half(blk.shape)
    rot = pltpu.roll(blk, LANES // 2, 1)
    return jnp.where(low, blk, rot), jnp.where(low, rot, blk)


def _split_halves(blk):
    low = _low_half(blk.shape)
    zero = jnp.zeros_like(blk)
    return jnp.where(low, blk, zero), jnp.where(low, zero, blk)


def _attn_scores(q_heads, kwin):
    return _dot_nt(jnp.concatenate(q_heads, axis=0).astype(BF16), kwin)


def _attn_probs(s_all, mask, sinks):
    rb = s_all.shape[0] // GROUP
    probs, maxes = [], []
    for i in range(GROUP):
        s = jnp.where(mask, s_all[i * rb:(i + 1) * rb, :], NEG_INF)
        mx = jnp.maximum(jnp.max(s, axis=-1, keepdims=True), sinks[i])
        probs.append(jnp.exp(s - mx))
        maxes.append(mx)
    return jnp.concatenate(probs, axis=0).astype(BF16), maxes


def _attn_values(p_all, vaug):
    return _dot(p_all, vaug)


def _attn_finish(res_all, maxes, sinks):
    rb = res_all.shape[0] // GROUP
    outs = []
    for i in range(GROUP):
        res = res_all[i * rb:(i + 1) * rb, :]
        den = res[:, LANES:] + jnp.exp(sinks[i] - maxes[i])
        outs.append(res[:, :LANES] * (1.0 / den))
    low = _low_half(outs[0].shape)
    return [jnp.where(low, outs[2 * i], outs[2 * i + 1]) for i in range(GROUP // 2)]


def _ada_kernel(c_ref, w_ref, b_ref, o_ref):
    c = c_ref[...]
    o_ref[0] = _dot(_silu(c).astype(BF16), w_ref[0].astype(BF16)) + b_ref[0]


def _ada_call(c, w, b):
    n_layers, d, n = w.shape
    r = c.shape[0]
    return pl.pallas_call(
        _ada_kernel,
        grid=(n_layers, n // ADA_TILE_N),
        in_specs=[pl.BlockSpec((r, d), lambda l, j: (0, 0)),
                  pl.BlockSpec((1, d, ADA_TILE_N), lambda l, j: (l, 0, j)),
                  pl.BlockSpec((1, 1, ADA_TILE_N), lambda l, j: (l, 0, j))],
        out_specs=pl.BlockSpec((1, r, ADA_TILE_N), lambda l, j: (l, 0, j)),
        out_shape=jax.ShapeDtypeStruct((n_layers, r, n), F32),
        compiler_params=pltpu.CompilerParams(
            dimension_semantics=("arbitrary", "arbitrary"), vmem_limit_bytes=VMEM_LIMIT),
        name="ada_mod",
    )(c, w, b.reshape(n_layers, 1, n))


def _prompt_l0_kernel(x_ref, mod_ref, kvmod_ref, rope_ref, ng_ref, w_in_ref, cw_ref, cb_ref, gw_ref,
                      gb_ref, lam_ref, w_out_ref, fin_ref, fout_ref, kvg_ref, wkv_ref,
                      x2_ref, k_ref, v_ref, conv_ref, hl_ref,
                      xr_s, yg_s, act_s, hist_s, hc_s, z_s, xk_s, x2k_s, *, nt, n_tiles):
    tm, d = x_ref.shape
    s = pl.program_id(0)
    n_chunks = fout_ref.shape[0] // MXU_COLS
    b_cur = jnp.minimum(s, n_tiles - 1) // nt
    b_prev = jnp.clip(s - 1, 0, n_tiles - 1) // nt
    b_pp = jnp.maximum(s - 2, 0) // nt
    mod = lambda i: _mod_row(mod_ref, b_cur, i)
    modp = lambda i: _mod_row(mod_ref, b_prev, i)

    @pl.when(s == 0)
    def _():
        z_s[...] = jnp.zeros_like(z_s)
        xk_s[...] = jnp.zeros_like(xk_s)
        x2k_s[...] = jnp.zeros_like(x2k_s)

    @pl.when(lax.rem(jnp.minimum(s, n_tiles - 1), nt) == 0)
    def _():
        hist_s[...] = jnp.zeros_like(hist_s)
        hc_s[...] = jnp.zeros_like(hc_s)

    out_prev = _dot(z_s[...], w_out_ref[...])
    x_prev = xk_s[...]
    x2_pp = x2k_s[...]

    hk = _rms_mod(x2_pp, kvg_ref[...], _mod_row(kvmod_ref, b_pp, 1), _mod_row(kvmod_ref, b_pp, 0)).astype(BF16)
    kv = _dot(hk, wkv_ref[...])
    k_ref[...] = _rope(kv[:, :KV_DIM], rope_ref[0], rope_ref[1], rope_ref[2])
    v_ref[...] = kv[:, KV_DIM:]

    x = x_ref[...]
    h = _rms_mod(x, ng_ref[0:1, :], mod(1), mod(0)).astype(BF16)
    _proj_in(h, w_in_ref, xr_s, yg_s)
    xk_s[...] = x

    x1 = x_prev + modp(2) * out_prev
    h2 = _rms_mod(x1, ng_ref[1:2, :], modp(4), modp(3)).astype(BF16)
    _swiglu_in(h2, fin_ref, act_s, range(0, FFN_SPLIT))

    xr = xr_s[...]
    hist = hist_s[...]
    row8 = lax.broadcasted_iota(jnp.int32, (SUBLANES, d), 0)

    def shifted(k):
        rolled = pltpu.roll(xr, k, 0)
        first = jnp.where(row8 >= k, rolled[0:SUBLANES], pltpu.roll(hist, k, 0))
        return jnp.concatenate([first, rolled[SUBLANES:]], axis=0)

    xc = cb_ref[...]
    for j in range(CONV_WIDTH - 1):
        xc = xc + cw_ref[j:j + 1, :] * shifted(CONV_WIDTH - 1 - j)
    xc = xc + cw_ref[CONV_WIDTH - 1:CONV_WIDTH, :] * xr
    hist_s[...] = xr[tm - SUBLANES:]
    xr_s[...] = xc

    cl = RG_C * _log_sigmoid(lam_ref[...])
    row8b = lax.broadcasted_iota(jnp.int32, (SUBLANES, RNN_BLOCK), 0)

    def recurrent_block(n):
        blk = slice(n * RNN_BLOCK, (n + 1) * RNN_BLOCK)
        xc_blk = xr_s[:, blk]
        gates = _dot(xc_blk.astype(BF16), gw_ref[n]) + gb_ref[n:n + 1, :]
        r = _sigmoid(gates[:, :RNN_BLOCK])
        i = _sigmoid(gates[:, RNN_BLOCK:])
        log_a = cl[:, blk] * r
        a_all = jnp.exp(log_a)
        u_all = jnp.sqrt(-jnp.tanh(log_a) * (a_all * a_all + 1.0)) * (i * xc_blk)
        hprev = hc_s[:, blk]
        hs_groups = []
        for g in range(tm // SUBLANES):
            a = a_all[g * SUBLANES:(g + 1) * SUBLANES, :]
            u = u_all[g * SUBLANES:(g + 1) * SUBLANES, :]
            for step in (1, 2, 4):
                keep = row8b >= step
                a_sh = jnp.where(keep, pltpu.roll(a, step, 0), 1.0)
                u_sh = jnp.where(keep, pltpu.roll(u, step, 0), 0.0)
                u = a * u_sh + u
                a = a * a_sh
            hs = a * hprev + u
            hprev = jnp.broadcast_to(hs[SUBLANES - 1:SUBLANES, :], hs.shape)
            hs_groups.append(hs)
        hc_s[:, blk] = hprev
        z_s[:, blk] = (jnp.concatenate(hs_groups, axis=0) * _gelu_tanh(yg_s[:, blk])).astype(BF16)

    blocks = list(range(N_RNN_BLOCKS))
    for c in range(FFN_SPLIT, n_chunks):
        _swiglu_in(h2, fin_ref, act_s, [c])
        if blocks:
            recurrent_block(blocks.pop(0))
    ffn = _dot(act_s[...], fout_ref[...])
    for n in blocks:
        recurrent_block(n)

    x2 = jnp.where(s <= n_tiles, x1 + modp(5) * ffn, x2_pp)
    x2_ref[...] = x2
    x2k_s[...] = x2

    @pl.when(s < n_tiles)
    def _():
        conv_ref[...] = hist_s[...]
        hl_ref[...] = hc_s[...]


def _resident(arr, lead=None):
    if lead is None:
        index, shape = (0,) * arr.ndim, arr.shape
    else:
        index, shape = (lead,) + (0,) * (arr.ndim - 1), (None,) + arr.shape[1:]
    return pl.BlockSpec(shape, lambda *_: index, pipeline_mode=pl.Buffered(1))


def _prompt_mod_specs(ada, kv_ada, layer, row_block):
    return [_resident_rows(ada, layer, row_block), _resident_rows(kv_ada, 0, row_block)]


def _resident_rows(arr, lead, row_block):
    index = (lead, row_block, 0)
    return pl.BlockSpec((None, SUBLANES, arr.shape[2]), lambda *_: index, pipeline_mode=pl.Buffered(1))


def _mod_row(ref, b, i):
    return ref[pl.ds(b, 1), i * D_MODEL:(i + 1) * D_MODEL]


def _prompt_l0_call(x, ada, kv_ada, mod_row_block, rope, consts):
    nb, t, d = x.shape
    tm = PROMPT_TILE
    nt = t // tm
    n_tiles = nb * nt
    d_ff = consts[9][0].shape[1]
    cur = lambda s: jnp.minimum(s, n_tiles - 1)
    prv = lambda s: jnp.clip(s - 1, 0, n_tiles - 1)
    pp = lambda s: jnp.maximum(s - 2, 0)
    tile_spec = lambda w, f: pl.BlockSpec((None, tm, w), lambda s: (f(s) // nt, lax.rem(f(s), nt), 0))
    state_spec = pl.BlockSpec((None, SUBLANES, d), lambda s: (cur(s) // nt, 0, 0))
    return pl.pallas_call(
        functools.partial(_prompt_l0_kernel, nt=nt, n_tiles=n_tiles),
        grid=(n_tiles + 2,),
        in_specs=[tile_spec(d, cur)] + _prompt_mod_specs(ada, kv_ada, 0, mod_row_block)
                 + [pl.BlockSpec((3, tm, LANES), lambda s: (0, lax.rem(pp(s), nt), 0))]
                 + [_resident(*c) for c in consts],
        out_specs=[tile_spec(d, prv), tile_spec(KV_DIM, pp), tile_spec(KV_DIM, pp),
                   state_spec, state_spec],
        out_shape=[jax.ShapeDtypeStruct((nb, t, d), F32),
                   jax.ShapeDtypeStruct((nb, t, KV_DIM), F32),
                   jax.ShapeDtypeStruct((nb, t, KV_DIM), F32),
                   jax.ShapeDtypeStruct((nb, SUBLANES, d), F32),
                   jax.ShapeDtypeStruct((nb, SUBLANES, d), F32)],
        scratch_shapes=[pltpu.VMEM((tm, d), F32), pltpu.VMEM((tm, d), F32),
                        pltpu.VMEM((tm, d_ff), BF16),
                        pltpu.VMEM((SUBLANES, d), F32), pltpu.VMEM((SUBLANES, d), F32),
                        pltpu.VMEM((tm, d), BF16), pltpu.VMEM((tm, d), F32), pltpu.VMEM((tm, d), F32)],
        compiler_params=pltpu.CompilerParams(
            dimension_semantics=("arbitrary",), vmem_limit_bytes=VMEM_LIMIT),
        name="prompt_layer0",
    )(x, ada, kv_ada, rope, *[c[0] for c in consts])


def _prompt_l1_kernel(sink_ref, x_ref, k_ref, v_ref, mod_ref, rope_ref, ng_ref, wq_ref, wo_ref,
                      fin_ref, fout_ref, fg_ref,
                      y_ref,
                      kw_s, vw_s, attn_s, act_s, xk_s, *, nt, n_tiles):
    tm, d = x_ref.shape
    s = pl.program_id(0)
    n_chunks = fout_ref.shape[0] // MXU_COLS
    cur_tile = jnp.minimum(s, n_tiles - 1)
    t = lax.rem(cur_tile, nt)
    b_cur = cur_tile // nt
    b_prev = jnp.maximum(s - 1, 0) // nt
    mod = lambda i: _mod_row(mod_ref, b_cur, i)
    modp = lambda i: _mod_row(mod_ref, b_prev, i)

    @pl.when(s == 0)
    def _():
        attn_s[...] = jnp.zeros_like(attn_s)
        xk_s[...] = jnp.zeros_like(xk_s)

    @pl.when(t == 0)
    def _():
        kw_s[0:WINDOW, :] = jnp.zeros((WINDOW, kw_s.shape[1]), BF16)
        vw_s[0:WINDOW, :] = jnp.zeros((WINDOW, vw_s.shape[1]), BF16)

    @pl.when(t > 0)
    def _():
        kw_s[0:WINDOW, :] = kw_s[tm:tm + WINDOW, :]
        vw_s[0:WINDOW, :] = vw_s[tm:tm + WINDOW, :]

    out_prev = _dot(attn_s[...], wo_ref[...])
    x_prev = xk_s[...]

    x = x_ref[...]
    h = _rms_mod(x, ng_ref[0:1, :], mod(1), mod(0)).astype(BF16)
    q = _rope(_dot(h, wq_ref[...]), rope_ref[0], rope_ref[1], rope_ref[2]) * (HEAD_DIM ** -0.5)
    q_split = [_split_halves(q[:, p * LANES:(p + 1) * LANES]) for p in range(d // LANES)]
    xk_s[...] = x

    new_rows = slice(WINDOW, WINDOW + tm)
    k, v = k_ref[...], v_ref[...]
    ones = jnp.ones((tm, LANES), BF16)
    for pb in range(KV_DIM // LANES):
        k_dup = _dup_halves(k[:, pb * LANES:(pb + 1) * LANES])
        v_dup = _dup_halves(v[:, pb * LANES:(pb + 1) * LANES])
        for i in range(2):
            g = 2 * pb + i
            kw_s[new_rows, g * LANES:(g + 1) * LANES] = k_dup[i].astype(BF16)
            vw_s[new_rows, 2 * g * LANES:(2 * g + 1) * LANES] = v_dup[i].astype(BF16)
            vw_s[new_rows, (2 * g + 1) * LANES:(2 * g + 2) * LANES] = ones

    x1 = x_prev + modp(2) * out_prev
    h2 = _rms_mod(x1, ng_ref[1:2, :], modp(4), modp(3)).astype(BF16)
    _swiglu_in(h2, fin_ref, act_s, range(0, L1_FFN_SPLITS[0]))

    span = 2 * WINDOW
    qi = lax.broadcasted_iota(jnp.int32, (WINDOW, span), 0)
    si = lax.broadcasted_iota(jnp.int32, (WINDOW, span), 1)
    band = (si >= qi) & (si <= qi + WINDOW)
    masks = [band & (si >= WINDOW - (t * tm + j * WINDOW)) for j in range(tm // WINDOW)]
    units = [(j, g) for j in range(tm // WINDOW) for g in range(N_KV_HEADS)]
    rows = lambda j: slice(j * WINDOW, (j + 1) * WINDOW)
    win = lambda j: slice(j * WINDOW, j * WINDOW + span)
    sinks = [[sink_ref[0, g * GROUP + i] for i in range(GROUP)] for g in range(N_KV_HEADS)]
    scores = [_attn_scores([q_split[2 * g + i // 2][i % 2][rows(j), :] for i in range(GROUP)],
                           kw_s[win(j), g * LANES:(g + 1) * LANES]) for j, g in units]
    _swiglu_in(h2, fin_ref, act_s, range(L1_FFN_SPLITS[0], L1_FFN_SPLITS[1]))
    probs = [_attn_probs(s_all, masks[j], sinks[g]) for s_all, (j, g) in zip(scores, units)]
    values = [_attn_values(p_all, vw_s[win(j), 2 * g * LANES:(2 * g + 2) * LANES])
              for (p_all, _), (j, g) in zip(probs, units)]
    _swiglu_in(h2, fin_ref, act_s, range(L1_FFN_SPLITS[1], n_chunks))
    for res_all, (_, maxes), (j, g) in zip(values, probs, units):
        for i, pair in enumerate(_attn_finish(res_all, maxes, sinks[g])):
            col = (2 * g + i) * LANES
            attn_s[rows(j), col:col + LANES] = pair.astype(BF16)

    x2 = x1 + modp(5) * _dot(act_s[...], fout_ref[...])
    y_ref[...] = _rms(x2) * fg_ref[...]


def _prompt_l1_call(sinks, x, k, v, ada, mod_row_block, rope, consts):
    nb, t, d = x.shape
    tm = PROMPT_TILE
    nt = t // tm
    n_tiles = nb * nt
    d_ff = consts[4][0].shape[1]
    cur = lambda s: jnp.minimum(s, n_tiles - 1)
    prv = lambda s: jnp.maximum(s - 1, 0)
    tile_spec = lambda w, f: pl.BlockSpec((None, tm, w), lambda s: (f(s) // nt, lax.rem(f(s), nt), 0))
    return pl.pallas_call(
        functools.partial(_prompt_l1_kernel, nt=nt, n_tiles=n_tiles),
        grid=(n_tiles + 1,),
        in_specs=[pl.BlockSpec(memory_space=pltpu.SMEM),
                  tile_spec(d, cur), tile_spec(KV_DIM, cur), tile_spec(KV_DIM, cur),
                  _resident_rows(ada, 1, mod_row_block),
                  pl.BlockSpec((3, tm, LANES), lambda s: (0, lax.rem(cur(s), nt), 0))]
                 + [_resident(*c) for c in consts],
        out_specs=tile_spec(d, prv),
        out_shape=jax.ShapeDtypeStruct((nb, t, d), F32),
        scratch_shapes=[pltpu.VMEM((WINDOW + tm, N_KV_HEADS * LANES), BF16),
                        pltpu.VMEM((WINDOW + tm, N_KV_HEADS * 2 * LANES), BF16),
                        pltpu.VMEM((tm, d), BF16),
                        pltpu.VMEM((tm, d_ff), BF16),
                        pltpu.VMEM((tm, d), F32)],
        compiler_params=pltpu.CompilerParams(
            dimension_semantics=("arbitrary",), vmem_limit_bytes=VMEM_LIMIT),
        name="prompt_layer1",
    )(sinks, x, k, v, ada, rope, *[c[0] for c in consts])


def _sample_l0_kernel(x_ref, mod_ref, kvmod_ref, qmod_ref, h0_ref, cst_ref, rope_ref, ng_ref, w_in_ref,
                      cw_ref, cb_ref, gw_ref, gb_ref, lam_ref, w_out_ref, fin_ref, fout_ref, kvg_ref,
                      wkv_ref, qg_ref, wq_ref,
                      x2_ref, k_ref, v_ref, q_ref, conv_ref, hl_ref,
                      xr_s, yg_s, a_s, u_s, o_s, act_s):
    nt, sb, d = x_ref.shape
    rows = nt * sb
    slab = lambda t: slice(t * sb, (t + 1) * sb)
    vec = lambda ref, i: _tile_rows(ref[:, i * d:(i + 1) * d], nt)
    mod = lambda i: vec(mod_ref, i)

    x = x_ref[...].reshape(rows, d)
    h = _rms_mod(x, ng_ref[0:1, :], mod(1), mod(0)).astype(BF16)
    _proj_in(h, w_in_ref, xr_s, yg_s)

    def conv_in(j):
        return cst_ref[j] if j < CONV_WIDTH - 1 else xr_s[slab(j - (CONV_WIDTH - 1)), :]

    xc_slabs = []
    for t in range(nt):
        acc = cb_ref[...]
        for j in range(CONV_WIDTH):
            acc = acc + cw_ref[j:j + 1, :] * conv_in(t + j)
        xc_slabs.append(acc)
    xc = jnp.concatenate(xc_slabs, axis=0)
    for j in range(CONV_WIDTH - 1):
        conv_ref[j] = xr_s[slab(nt - (CONV_WIDTH - 1) + j), :]

    _rglru_gates(xc, gw_ref, gb_ref, lam_ref, a_s, u_s)

    hs = h0_ref[...]
    for t in range(nt):
        hs = a_s[slab(t), :] * hs + u_s[slab(t), :]
        o_s[slab(t), :] = hs
    hl_ref[...] = hs

    z = (o_s[...] * _gelu_tanh(yg_s[...])).astype(BF16)
    x1 = x + mod(2) * _dot(z, w_out_ref[...])

    h2 = _rms_mod(x1, ng_ref[1:2, :], mod(4), mod(3)).astype(BF16)
    x2 = x1 + mod(5) * _swiglu(h2, fin_ref, fout_ref, act_s)
    x2_ref[...] = x2.reshape(nt, sb, d)

    hk = _rms_mod(x2, kvg_ref[...], vec(kvmod_ref, 1), vec(kvmod_ref, 0)).astype(BF16)
    kv = _dot(hk, wkv_ref[...])
    hq = _rms_mod(x2, qg_ref[0:1, :], vec(qmod_ref, 1), vec(qmod_ref, 0)).astype(BF16)
    q = _dot(hq, wq_ref[...])
    for t in range(nt):
        c, s_next, s_prev = rope_ref[0, t:t + 1, :], rope_ref[1, t:t + 1, :], rope_ref[2, t:t + 1, :]
        k_ref[t] = _rope(kv[slab(t), :KV_DIM], c, s_next, s_prev)
        q_ref[t] = _rope(q[slab(t), :], c, s_next, s_prev) * (HEAD_DIM ** -0.5)
    v_ref[...] = kv[:, KV_DIM:].reshape(nt, sb, KV_DIM)


def _sample_mod_spec(arr, lead, sb):
    return pl.BlockSpec((None, sb, arr.shape[2]), lambda i: (lead, i, 0))


def _sample_l0_call(x, ada, kv_ada, h0, cst, consts):
    nt, nb, d = x.shape
    sb = SAMPLE_BATCH_TILE
    d_ff = consts[10][0].shape[1]
    rows = nt * sb
    slab_spec = lambda lead, w: pl.BlockSpec((lead, sb, w), lambda i: (0, i, 0))
    return pl.pallas_call(
        _sample_l0_kernel,
        grid=(nb // sb,),
        in_specs=[slab_spec(nt, d), _sample_mod_spec(ada, 0, sb), _sample_mod_spec(kv_ada, 0, sb),
                  _sample_mod_spec(ada, 1, sb),
                  pl.BlockSpec((sb, d), lambda i: (i, 0)), slab_spec(CONV_WIDTH - 1, d)]
                 + [_resident(*c) for c in consts],
        out_specs=[slab_spec(nt, d), slab_spec(nt, KV_DIM), slab_spec(nt, KV_DIM), slab_spec(nt, d),
                   slab_spec(CONV_WIDTH - 1, d), pl.BlockSpec((sb, d), lambda i: (i, 0))],
        out_shape=[jax.ShapeDtypeStruct((nt, nb, d), F32),
                   jax.ShapeDtypeStruct((nt, nb, KV_DIM), F32),
                   jax.ShapeDtypeStruct((nt, nb, KV_DIM), F32),
                   jax.ShapeDtypeStruct((nt, nb, d), F32),
                   jax.ShapeDtypeStruct((CONV_WIDTH - 1, nb, d), F32),
                   jax.ShapeDtypeStruct((nb, d), F32)],
        scratch_shapes=[pltpu.VMEM((rows, d), F32)] * 5 + [pltpu.VMEM((rows, d_ff), BF16)],
        compiler_params=pltpu.CompilerParams(
            dimension_semantics=("arbitrary",), vmem_limit_bytes=VMEM_LIMIT),
        name="sample_layer0",
    )(x, ada, kv_ada, ada, h0, cst, *[c[0] for c in consts])


def _sample_attn_kernel(sink_ref, q_ref, kn_ref, vn_ref, ck_ref, cv_ref,
                        attn_ref, ko_ref, vo_ref):
    sb, nt, d = q_ref.shape
    span = 2 * WINDOW

    tok = lax.broadcasted_iota(jnp.int32, (nt, span), 0)
    si = lax.broadcasted_iota(jnp.int32, (nt, span), 1)
    mask = (si >= tok) & (si <= tok + WINDOW)
    pad = jnp.zeros((WINDOW - nt, LANES), F32)
    ones = jnp.ones((span, LANES), BF16)

    sinks = [[sink_ref[0, g * GROUP + i] for i in range(GROUP)] for g in range(N_KV_HEADS)]

    def windows(b):
        kn, vn = kn_ref[b], vn_ref[b]
        ck, cv = ck_ref[b], cv_ref[b]
        ko_ref[b, 0:WINDOW - nt, :] = ck[nt:, :]
        ko_ref[b, WINDOW - nt:WINDOW, :] = kn
        vo_ref[b, 0:WINDOW - nt, :] = cv[nt:, :]
        vo_ref[b, WINDOW - nt:WINDOW, :] = vn
        kwin, vaug = [], []
        for pb in range(KV_DIM // LANES):
            blk = slice(pb * LANES, (pb + 1) * LANES)
            (kc, kn_d), (vc, vn_d) = [(_dup_halves(c[:, blk]), _dup_halves(n[:, blk]))
                                      for c, n in ((ck, kn), (cv, vn))]
            for i in range(2):
                kwin.append(jnp.concatenate([kc[i], kn_d[i], pad], axis=0).astype(BF16))
                vcol = jnp.concatenate([vc[i], vn_d[i], pad], axis=0).astype(BF16)
                vaug.append(jnp.concatenate([vcol, ones], axis=1))
        return kwin, vaug

    def sequences(it, carry):
        seqs = [it * SAMPLE_ATTN_UNROLL + u for u in range(SAMPLE_ATTN_UNROLL)]
        wins = [windows(b) for b in seqs]
        units = [(u, g) for u in range(SAMPLE_ATTN_UNROLL) for g in range(N_KV_HEADS)]
        q_split = [[_split_halves(q_ref[b][:, p * LANES:(p + 1) * LANES]) for p in range(d // LANES)]
                   for b in seqs]
        scores = [_attn_scores([q_split[u][2 * g + i // 2][i % 2] for i in range(GROUP)], wins[u][0][g])
                  for u, g in units]
        probs = [_attn_probs(s_all, mask, sinks[g]) for s_all, (u, g) in zip(scores, units)]
        values = [_attn_values(p_all, wins[u][1][g]) for (p_all, _), (u, g) in zip(probs, units)]
        pairs = [[] for _ in seqs]
        for res_all, (_, maxes), (u, g) in zip(values, probs, units):
            pairs[u] += _attn_finish(res_all, maxes, sinks[g])
        for u, b in enumerate(seqs):
            attn_ref[b] = jnp.concatenate(pairs[u], axis=1)
        return carry

    lax.fori_loop(0, sb // SAMPLE_ATTN_UNROLL, sequences, 0)


def _sample_attn_call(sinks, q, kn, vn, ck, cv):
    nb, nt, d = q.shape
    sb = SAMPLE_ATTN_BATCH
    seq_spec = lambda r, w: pl.BlockSpec((sb, r, w), lambda i: (i, 0, 0))
    return pl.pallas_call(
        _sample_attn_kernel,
        grid=(nb // sb,),
        in_specs=[pl.BlockSpec(memory_space=pltpu.SMEM),
                  seq_spec(nt, d), seq_spec(nt, KV_DIM), seq_spec(nt, KV_DIM),
                  seq_spec(WINDOW, KV_DIM), seq_spec(WINDOW, KV_DIM)],
        out_specs=[seq_spec(nt, d), seq_spec(WINDOW, KV_DIM), seq_spec(WINDOW, KV_DIM)],
        out_shape=[jax.ShapeDtypeStruct((nb, nt, d), F32),
                   jax.ShapeDtypeStruct((nb, WINDOW, KV_DIM), F32),
                   jax.ShapeDtypeStruct((nb, WINDOW, KV_DIM), F32)],
        compiler_params=pltpu.CompilerParams(
            dimension_semantics=("arbitrary",), vmem_limit_bytes=VMEM_LIMIT),
        name="sample_attention",
    )(sinks, q, kn, vn, ck, cv)


def _sample_l1_kernel(x_ref, attn_ref, mod_ref, ng_ref, wo_ref, fin_ref, fout_ref, fg_ref,
                      y_ref, act_s):
    nt, sb, d = x_ref.shape
    rows = nt * sb
    mod = lambda i: _tile_rows(mod_ref[:, i * d:(i + 1) * d], nt)
    x = x_ref[...].reshape(rows, d)
    attn = attn_ref[...].reshape(rows, d).astype(BF16)
    x1 = x + mod(2) * _dot(attn, wo_ref[...])
    h2 = _rms_mod(x1, ng_ref[1:2, :], mod(4), mod(3)).astype(BF16)
    x2 = x1 + mod(5) * _swiglu(h2, fin_ref, fout_ref, act_s)
    y_ref[...] = (_rms(x2) * fg_ref[...]).reshape(nt, sb, d)


def _sample_l1_call(x, attn, ada, consts):
    nt, nb, d = x.shape
    sb = SAMPLE_BATCH_TILE
    d_ff = consts[3][0].shape[1]
    slab_spec = lambda lead: pl.BlockSpec((lead, sb, d), lambda i: (0, i, 0))
    return pl.pallas_call(
        _sample_l1_kernel,
        grid=(nb // sb,),
        in_specs=[slab_spec(nt), slab_spec(nt), _sample_mod_spec(ada, 1, sb)]
                 + [_resident(*c) for c in consts],
        out_specs=slab_spec(nt),
        out_shape=jax.ShapeDtypeStruct((nt, nb, d), F32),
        scratch_shapes=[pltpu.VMEM((nt * sb, d_ff), BF16)],
        compiler_params=pltpu.CompilerParams(
            dimension_semantics=("arbitrary",), vmem_limit_bytes=VMEM_LIMIT),
        name="sample_layer1",
    )(x, attn, ada, *[c[0] for c in consts])


def _rope_tables(pos):
    half = ROT_DIM // 2
    inv = ROPE_THETA ** (-jnp.arange(0, ROT_DIM, 2, dtype=F32) / ROT_DIM)
    ang = pos.astype(F32)[:, None] * inv[None, :]
    cos, sin = jnp.cos(ang), jnp.sin(ang)
    n = pos.shape[0]
    rest = HEAD_DIM - ROT_DIM
    c = jnp.concatenate([cos, cos, jnp.ones((n, rest), F32)], axis=1)
    s_next = jnp.concatenate([-sin, jnp.zeros((n, half + rest), F32)], axis=1)
    s_prev = jnp.concatenate([jnp.zeros((n, half), F32), sin, jnp.zeros((n, rest), F32)], axis=1)
    reps = LANES // HEAD_DIM
    return jnp.stack([jnp.tile(c, (1, reps)), jnp.tile(s_next, (1, reps)), jnp.tile(s_prev, (1, reps))])


def kernel(x_prompt, x_sample, c_prompt, c_sample, state_conv, state_h, cache_k, cache_v, ada_w, ada_b, norm_g, rnn_w_in, rnn_conv_w, rnn_conv_b, rnn_gate_w, rnn_gate_b, rnn_lambda, rnn_w_out, kv_ada_w, kv_ada_b, kv_norm_g, w_kv, attn_w_q, attn_sinks, attn_w_o, ffn_w_in, ffn_w_out, final_g):
    nb_p, t_p, d = x_prompt.shape
    nb_s, t_s, _ = x_sample.shape

    assert rnn_w_in.shape[0] == 1 and attn_w_q.shape[0] == 1 and nb_s % SUBLANES == 0

    c_all = jnp.concatenate([c_sample, c_prompt], axis=0)
    ada = _ada_call(c_all, ada_w, ada_b)
    kv_ada = _ada_call(c_all, kv_ada_w[None], kv_ada_b[None])
    prompt_row_block = nb_s // SUBLANES

    bf = lambda w: w.astype(BF16)
    row = lambda v: v.reshape(1, -1)
    ffn_in, ffn_out = bf(ffn_w_in), bf(ffn_w_out)
    wq, wo = (bf(attn_w_q), 0), (bf(attn_w_o), 0)
    l0_consts = [(norm_g, 0), (bf(rnn_w_in), 0), (rnn_conv_w, 0), (rnn_conv_b, None), (bf(rnn_gate_w), 0),
                 (rnn_gate_b, 0), (rnn_lambda, None), (bf(rnn_w_out), 0), (ffn_in, 0), (ffn_out, 0),
                 (row(kv_norm_g), None), (bf(w_kv), None)]
    l1_ffn = [(ffn_in, 1), (ffn_out, 1), (row(final_g), None)]

    rope_p = _rope_tables(jnp.arange(t_p, dtype=jnp.int32))
    x2_p, k_p, v_p, conv_p, hl_p = _prompt_l0_call(x_prompt, ada, kv_ada, prompt_row_block, rope_p, l0_consts)
    y_prompt = _prompt_l1_call(attn_sinks, x2_p, k_p, v_p, ada, prompt_row_block, rope_p,
                               [(norm_g, 1), wq, wo] + l1_ffn)

    rope_s = _rope_tables(PAST_LEN + jnp.arange(t_s, dtype=jnp.int32))
    x2_s, k_s, v_s, q_s, conv_s, hl_s = _sample_l0_call(
        x_sample.transpose(1, 0, 2), ada, kv_ada, state_h[0], state_conv[0].transpose(1, 0, 2),
        [(rope_s, None)] + l0_consts + [(norm_g, 1), wq])
    attn_s, ko_s, vo_s = _sample_attn_call(
        attn_sinks, q_s.transpose(1, 0, 2), k_s.transpose(1, 0, 2), v_s.transpose(1, 0, 2),
        cache_k.reshape(nb_s, WINDOW, KV_DIM), cache_v.reshape(nb_s, WINDOW, KV_DIM))
    y_s = _sample_l1_call(x2_s, attn_s.transpose(1, 0, 2), ada, [(norm_g, 1), wo] + l1_ffn)

    kv_shape = (WINDOW, N_KV_HEADS, HEAD_DIM)
    return (y_prompt,
            y_s.transpose(1, 0, 2),
            conv_p[None, :, SUBLANES - (CONV_WIDTH - 1):, :],
            hl_p[None, :, 0, :],
            k_p[:, t_p - WINDOW:, :].reshape((nb_p,) + kv_shape),
            v_p[:, t_p - WINDOW:, :].reshape((nb_p,) + kv_shape),
            conv_s.transpose(1, 0, 2)[None],
            hl_s[None],
            ko_s.reshape((nb_s,) + kv_shape),
            vo_s.reshape((nb_s,) + kv_shape))
```

```python
import functools

import jax
import jax.numpy as jnp
from jax import lax
from jax.experimental import pallas as pl
from jax.experimental.pallas import tpu as pltpu

F32 = jnp.float32
BF16 = jnp.bfloat16

D_MODEL = 1024
N_RNN_BLOCKS = 8
RNN_BLOCK = D_MODEL // N_RNN_BLOCKS
CONV_WIDTH = 4
RG_C = 8.0
HEAD_DIM = 64
N_HEADS = D_MODEL // HEAD_DIM
N_KV_HEADS = 4
GROUP = N_HEADS // N_KV_HEADS
KV_DIM = N_KV_HEADS * HEAD_DIM
WINDOW = 128
ROT_DIM = HEAD_DIM // 4
ROPE_THETA = 500000.0
EPS = 1e-6
NEG_INF = -1e30
LOG2_E = 1.4426950408889634
PAST_LEN = 16384

LANES = 128
SUBLANES = 8
MXU_COLS = 256
VMEM_LIMIT = 56 * 1024 * 1024

PROMPT_TILE = 256
FFN_SPLIT = 3
L1_FFN_SPLITS = (3, 7)
ADA_TILE_N = 2048
SAMPLE_ATTN_BATCH = 16
SAMPLE_ATTN_UNROLL = 4
SAMPLE_BATCH_TILE = 32


def _dot(a, b):
    return jnp.dot(a, b, preferred_element_type=F32)


def _dot_nt(a, b):
    return lax.dot_general(a, b, (((1,), (1,)), ((), ())), preferred_element_type=F32)


def _sigmoid(x):
    return 1.0 / (1.0 + jnp.exp2(x * (-LOG2_E)))


def _silu(x):
    return x * _sigmoid(x)


def _gelu_tanh(x):
    return x * (0.5 * (1.0 + jnp.tanh(0.7978845608028654 * (x + 0.044715 * (x * x * x)))))


def _log_sigmoid(x):
    return -(jnp.maximum(-x, 0.0) + jnp.log1p(jnp.exp(-jnp.abs(x))))


def _rms(x):
    return x * lax.rsqrt(jnp.mean(x * x, axis=-1, keepdims=True) + EPS)


def _rms_mod(x, g, scale, shift):
    return (_rms(x) * g) * (1.0 + scale) + shift


def _tile_rows(m, reps):
    return jnp.concatenate([m] * reps, axis=0)


def _rope_block(blk, c, s_next, s_prev):
    return blk * c + pltpu.roll(blk, LANES - ROT_DIM // 2, 1) * s_next + pltpu.roll(blk, ROT_DIM // 2, 1) * s_prev


def _rope(x, c, s_next, s_prev):
    blocks = [_rope_block(x[:, j * LANES:(j + 1) * LANES], c, s_next, s_prev)
              for j in range(x.shape[1] // LANES)]
    return jnp.concatenate(blocks, axis=1)


def _proj_in(h, w_in_ref, xr_s, yg_s):
    d = xr_s.shape[1]
    cw = 2 * MXU_COLS
    for c in range(d // cw):
        xr_s[:, c * cw:(c + 1) * cw] = _dot(h, w_in_ref[:, c * cw:(c + 1) * cw])
        yg_s[:, c * cw:(c + 1) * cw] = _dot(h, w_in_ref[:, d + c * cw:d + (c + 1) * cw])


def _rglru_gates(xc, gw_ref, gb_ref, lam_ref, a_s, u_s):
    xcb = xc.astype(BF16)
    cl = RG_C * _log_sigmoid(lam_ref[...])
    for n in range(N_RNN_BLOCKS):
        blk = slice(n * RNN_BLOCK, (n + 1) * RNN_BLOCK)
        g = _dot(xcb[:, blk], gw_ref[n]) + gb_ref[n:n + 1, :]
        r = _sigmoid(g[:, :RNN_BLOCK])
        i = _sigmoid(g[:, RNN_BLOCK:])
        log_a = cl[:, blk] * r
        a = jnp.exp(log_a)
        a_s[:, blk] = a
        u_s[:, blk] = jnp.sqrt(-jnp.tanh(log_a) * (a * a + 1.0)) * (i * xc[:, blk])


def _swiglu_in(h, fin_ref, act_s, chunks):
    d_ff = act_s.shape[1]
    for c in chunks:
        cols = slice(c * MXU_COLS, (c + 1) * MXU_COLS)
        gate = _dot(h, fin_ref[:, cols])
        up = _dot(h, fin_ref[:, d_ff + c * MXU_COLS:d_ff + (c + 1) * MXU_COLS])
        act_s[:, cols] = (_silu(gate) * up).astype(BF16)


def _swiglu(h, fin_ref, fout_ref, act_s):
    _swiglu_in(h, fin_ref, act_s, range(fout_ref.shape[0] // MXU_COLS))
    return _dot(act_s[...], fout_ref[...])


def _low_half(shape):
    return lax.broadcasted_iota(jnp.int32, shape, 1) < LANES // 2


def _dup_halves(blk):
    low = _low_half(blk.shape)
    rot = pltpu.roll(blk, LANES // 2, 1)
    return jnp.where(low, blk, rot), jnp.where(low, rot, blk)


def _split_halves(blk):
    low = _low_half(blk.shape)
    zero = jnp.zeros_like(blk)
    return jnp.where(low, blk, zero), jnp.where(low, zero, blk)


def _attn_scores(q_heads, kwin, keys_on_lanes=False):
    q_all = jnp.concatenate(q_heads, axis=0).astype(BF16)
    return _dot(q_all, kwin) if keys_on_lanes else _dot_nt(q_all, kwin)


def _attn_probs(s_all, mask, sinks):
    rb = s_all.shape[0] // GROUP
    probs, maxes = [], []
    for i in range(GROUP):
        s = jnp.where(mask, s_all[i * rb:(i + 1) * rb, :], NEG_INF)
        mx = jnp.maximum(jnp.max(s, axis=-1, keepdims=True), sinks[i])
        probs.append(jnp.exp(s - mx))
        maxes.append(mx)
    return jnp.concatenate(probs, axis=0).astype(BF16), maxes


def _attn_values(p_all, vaug, keys_on_lanes=False):
    return _dot_nt(p_all, vaug) if keys_on_lanes else _dot(p_all, vaug)


def _attn_finish(res_all, maxes, sinks):
    rb = res_all.shape[0] // GROUP
    outs = []
    for i in range(GROUP):
        res = res_all[i * rb:(i + 1) * rb, :]
        den = res[:, LANES:] + jnp.exp(sinks[i] - maxes[i])
        outs.append(res[:, :LANES] * (1.0 / den))
    low = _low_half(outs[0].shape)
    return [jnp.where(low, outs[2 * i], outs[2 * i + 1]) for i in range(GROUP // 2)]


def _ada_kernel(c_ref, w_ref, b_ref, o_ref):
    c = c_ref[...]
    o_ref[0] = _dot(_silu(c).astype(BF16), w_ref[0].astype(BF16)) + b_ref[0]


def _ada_call(c, w, b):
    n_layers, d, n = w.shape
    r = c.shape[0]
    return pl.pallas_call(
        _ada_kernel,
        grid=(n_layers, n // ADA_TILE_N),
        in_specs=[pl.BlockSpec((r, d), lambda l, j: (0, 0)),
                  pl.BlockSpec((1, d, ADA_TILE_N), lambda l, j: (l, 0, j)),
                  pl.BlockSpec((1, 1, ADA_TILE_N), lambda l, j: (l, 0, j))],
        out_specs=pl.BlockSpec((1, r, ADA_TILE_N), lambda l, j: (l, 0, j)),
        out_shape=jax.ShapeDtypeStruct((n_layers, r, n), F32),
        compiler_params=pltpu.CompilerParams(
            dimension_semantics=("arbitrary", "arbitrary"), vmem_limit_bytes=VMEM_LIMIT),
        name="ada_mod",
    )(c, w, b.reshape(n_layers, 1, n))


def _prompt_l0_kernel(x_ref, mod_ref, kvmod_ref, rope_ref, ng_ref, w_in_ref, cw_ref, cb_ref, gw_ref,
                      gb_ref, lam_ref, w_out_ref, fin_ref, fout_ref, kvg_ref, wkv_ref,
                      x2_ref, k_ref, v_ref, conv_ref, hl_ref,
                      xr_s, yg_s, act_s, hist_s, hc_s, z_s, xk_s, x2k_s, scan_s, *, nt, n_tiles):
    tm, d = x_ref.shape
    s = pl.program_id(0)
    n_chunks = fout_ref.shape[0] // MXU_COLS
    b_cur = jnp.minimum(s, n_tiles - 1) // nt
    b_prev = jnp.clip(s - 1, 0, n_tiles - 1) // nt
    b_pp = jnp.maximum(s - 2, 0) // nt
    mod = lambda i: _mod_row(mod_ref, b_cur, i)
    modp = lambda i: _mod_row(mod_ref, b_prev, i)

    @pl.when(s == 0)
    def _():
        z_s[...] = jnp.zeros_like(z_s)
        xk_s[...] = jnp.zeros_like(xk_s)
        x2k_s[...] = jnp.zeros_like(x2k_s)

    @pl.when(lax.rem(jnp.minimum(s, n_tiles - 1), nt) == 0)
    def _():
        hist_s[...] = jnp.zeros_like(hist_s)
        hc_s[...] = jnp.zeros_like(hc_s)

    out_prev = _dot(z_s[...], w_out_ref[...])
    x_prev = xk_s[...]
    x2_pp = x2k_s[...]

    hk = _rms_mod(x2_pp, kvg_ref[...], _mod_row(kvmod_ref, b_pp, 1), _mod_row(kvmod_ref, b_pp, 0)).astype(BF16)
    kv = _dot(hk, wkv_ref[...])
    k_ref[...] = _rope(kv[:, :KV_DIM], rope_ref[0], rope_ref[1], rope_ref[2])
    v_ref[...] = kv[:, KV_DIM:]

    x = x_ref[...]
    h = _rms_mod(x, ng_ref[0:1, :], mod(1), mod(0)).astype(BF16)
    _proj_in(h, w_in_ref, xr_s, yg_s)
    xk_s[...] = x

    x1 = x_prev + modp(2) * out_prev
    h2 = _rms_mod(x1, ng_ref[1:2, :], modp(4), modp(3)).astype(BF16)
    _swiglu_in(h2, fin_ref, act_s, range(0, FFN_SPLIT))

    xr = xr_s[...]
    hist = hist_s[...]
    row8 = lax.broadcasted_iota(jnp.int32, (SUBLANES, d), 0)

    def shifted(k):
        rolled = pltpu.roll(xr, k, 0)
        first = jnp.where(row8 >= k, rolled[0:SUBLANES], pltpu.roll(hist, k, 0))
        return jnp.concatenate([first, rolled[SUBLANES:]], axis=0)

    xc = cb_ref[...]
    for j in range(CONV_WIDTH - 1):
        xc = xc + cw_ref[j:j + 1, :] * shifted(CONV_WIDTH - 1 - j)
    xc = xc + cw_ref[CONV_WIDTH - 1:CONV_WIDTH, :] * xr
    hist_s[...] = xr[tm - SUBLANES:]
    xr_s[...] = xc

    cl = RG_C * _log_sigmoid(lam_ref[...])
    row8b = lax.broadcasted_iota(jnp.int32, (SUBLANES, RNN_BLOCK), 0)
    chunk = tm // SUBLANES
    pitch = chunk + SUBLANES

    def recurrent_block(n):
        blk = slice(n * RNN_BLOCK, (n + 1) * RNN_BLOCK)
        a_s, u_s, h_s, p_s = (scan_s.at[n % 2, j] for j in range(4))
        xc_blk = xr_s[:, blk]
        gates = _dot(xc_blk.astype(BF16), gw_ref[n]) + gb_ref[n:n + 1, :]
        r = _sigmoid(gates[:, :RNN_BLOCK])
        i = _sigmoid(gates[:, RNN_BLOCK:])
        log_a = cl[:, blk] * r
        a_all = jnp.exp(log_a)
        u_all = jnp.sqrt(-jnp.tanh(log_a) * (a_all * a_all + 1.0)) * (i * xc_blk)
        for c in range(SUBLANES):
            a_s[c * pitch:c * pitch + chunk, :] = a_all[c * chunk:(c + 1) * chunk, :]
            u_s[c * pitch:c * pitch + chunk, :] = u_all[c * chunk:(c + 1) * chunk, :]
        h = jnp.zeros((SUBLANES, RNN_BLOCK), F32)
        p = jnp.ones((SUBLANES, RNN_BLOCK), F32)
        for g in range(chunk):
            step_rows = pl.ds(g, SUBLANES, stride=pitch)
            a = a_s[step_rows, :]
            h = a * h + u_s[step_rows, :]
            p = a * p
            h_s[step_rows, :] = h
            p_s[step_rows, :] = p
        for step in (1, 2, 4):
            keep = row8b >= step
            h = p * jnp.where(keep, pltpu.roll(h, step, 0), 0.0) + h
            p = p * jnp.where(keep, pltpu.roll(p, step, 0), 1.0)
        hprev = hc_s[:, blk]
        ends = h + p * hprev
        incoming = jnp.where(row8b >= 1, pltpu.roll(ends, 1, 0), hprev)
        hc_s[:, blk] = jnp.broadcast_to(ends[SUBLANES - 1:SUBLANES, :], ends.shape)
        for c in range(SUBLANES):
            rows = slice(c * chunk, (c + 1) * chunk)
            local = slice(c * pitch, c * pitch + chunk)
            hs = h_s[local, :] + p_s[local, :] * incoming[c:c + 1, :]
            z_s[rows, blk] = (hs * _gelu_tanh(yg_s[rows, blk])).astype(BF16)

    blocks = list(range(N_RNN_BLOCKS))
    for c in range(FFN_SPLIT, n_chunks):
        _swiglu_in(h2, fin_ref, act_s, [c])
        if blocks:
            recurrent_block(blocks.pop(0))
    ffn = _dot(act_s[...], fout_ref[...])
    for n in blocks:
        recurrent_block(n)

    x2 = jnp.where(s <= n_tiles, x1 + modp(5) * ffn, x2_pp)
    x2_ref[...] = x2
    x2k_s[...] = x2

    @pl.when(s < n_tiles)
    def _():
        conv_ref[...] = hist_s[...]
        hl_ref[...] = hc_s[...]


def _resident(arr, lead=None):
    if lead is None:
        index, shape = (0,) * arr.ndim, arr.shape
    else:
        index, shape = (lead,) + (0,) * (arr.ndim - 1), (None,) + arr.shape[1:]
    return pl.BlockSpec(shape, lambda *_: index, pipeline_mode=pl.Buffered(1))


def _prompt_mod_specs(ada, kv_ada, layer, row_block):
    return [_resident_rows(ada, layer, row_block), _resident_rows(kv_ada, 0, row_block)]


def _resident_rows(arr, lead, row_block):
    index = (lead, row_block, 0)
    return pl.BlockSpec((None, SUBLANES, arr.shape[2]), lambda *_: index, pipeline_mode=pl.Buffered(1))


def _mod_row(ref, b, i):
    return ref[pl.ds(b, 1), i * D_MODEL:(i + 1) * D_MODEL]


def _prompt_l0_call(x, ada, kv_ada, mod_row_block, rope, consts):
    nb, t, d = x.shape
    tm = PROMPT_TILE
    nt = t // tm
    n_tiles = nb * nt
    d_ff = consts[9][0].shape[1]
    cur = lambda s: jnp.minimum(s, n_tiles - 1)
    prv = lambda s: jnp.clip(s - 1, 0, n_tiles - 1)
    pp = lambda s: jnp.maximum(s - 2, 0)
    tile_spec = lambda w, f: pl.BlockSpec((None, tm, w), lambda s: (f(s) // nt, lax.rem(f(s), nt), 0))
    state_spec = pl.BlockSpec((None, SUBLANES, d), lambda s: (cur(s) // nt, 0, 0))
    return pl.pallas_call(
        functools.partial(_prompt_l0_kernel, nt=nt, n_tiles=n_tiles),
        grid=(n_tiles + 2,),
        in_specs=[tile_spec(d, cur)] + _prompt_mod_specs(ada, kv_ada, 0, mod_row_block)
                 + [pl.BlockSpec((3, tm, LANES), lambda s: (0, lax.rem(pp(s), nt), 0))]
                 + [_resident(*c) for c in consts],
        out_specs=[tile_spec(d, prv), tile_spec(KV_DIM, pp), tile_spec(KV_DIM, pp),
                   state_spec, state_spec],
        out_shape=[jax.ShapeDtypeStruct((nb, t, d), F32),
                   jax.ShapeDtypeStruct((nb, t, KV_DIM), F32),
                   jax.ShapeDtypeStruct((nb, t, KV_DIM), F32),
                   jax.ShapeDtypeStruct((nb, SUBLANES, d), F32),
                   jax.ShapeDtypeStruct((nb, SUBLANES, d), F32)],
        scratch_shapes=[pltpu.VMEM((tm, d), F32), pltpu.VMEM((tm, d), F32),
                        pltpu.VMEM((tm, d_ff), BF16),
                        pltpu.VMEM((SUBLANES, d), F32), pltpu.VMEM((SUBLANES, d), F32),
                        pltpu.VMEM((tm, d), BF16), pltpu.VMEM((tm, d), F32), pltpu.VMEM((tm, d), F32),
                        pltpu.VMEM((2, 4, tm + SUBLANES * SUBLANES, RNN_BLOCK), F32)],
        compiler_params=pltpu.CompilerParams(
            dimension_semantics=("arbitrary",), vmem_limit_bytes=VMEM_LIMIT),
        name="prompt_layer0",
    )(x, ada, kv_ada, rope, *[c[0] for c in consts])


def _prompt_l1_kernel(sink_ref, x_ref, k_ref, v_ref, mod_ref, rope_ref, ng_ref, wq_ref, wo_ref,
                      fin_ref, fout_ref, fg_ref,
                      y_ref,
                      kw_s, vw_s, attn_s, act_s, xk_s, *, nt, n_tiles):
    tm, d = x_ref.shape
    s = pl.program_id(0)
    n_chunks = fout_ref.shape[0] // MXU_COLS
    cur_tile = jnp.minimum(s, n_tiles - 1)
    t = lax.rem(cur_tile, nt)
    b_cur = cur_tile // nt
    b_prev = jnp.maximum(s - 1, 0) // nt
    mod = lambda i: _mod_row(mod_ref, b_cur, i)
    modp = lambda i: _mod_row(mod_ref, b_prev, i)

    @pl.when(s == 0)
    def _():
        attn_s[...] = jnp.zeros_like(attn_s)
        xk_s[...] = jnp.zeros_like(xk_s)

    @pl.when(t == 0)
    def _():
        kw_s[0:WINDOW, :] = jnp.zeros((WINDOW, kw_s.shape[1]), BF16)
        vw_s[0:WINDOW, :] = jnp.zeros((WINDOW, vw_s.shape[1]), BF16)

    @pl.when(t > 0)
    def _():
        kw_s[0:WINDOW, :] = kw_s[tm:tm + WINDOW, :]
        vw_s[0:WINDOW, :] = vw_s[tm:tm + WINDOW, :]

    out_prev = _dot(attn_s[...], wo_ref[...])
    x_prev = xk_s[...]

    x = x_ref[...]
    h = _rms_mod(x, ng_ref[0:1, :], mod(1), mod(0)).astype(BF16)
    q = _rope(_dot(h, wq_ref[...]), rope_ref[0], rope_ref[1], rope_ref[2]) * (HEAD_DIM ** -0.5)
    q_split = [_split_halves(q[:, p * LANES:(p + 1) * LANES]) for p in range(d // LANES)]
    xk_s[...] = x

    new_rows = slice(WINDOW, WINDOW + tm)
    k, v = k_ref[...], v_ref[...]
    ones = jnp.ones((tm, LANES), BF16)
    for pb in range(KV_DIM // LANES):
        k_dup = _dup_halves(k[:, pb * LANES:(pb + 1) * LANES])
        v_dup = _dup_halves(v[:, pb * LANES:(pb + 1) * LANES])
        for i in range(2):
            g = 2 * pb + i
            kw_s[new_rows, g * LANES:(g + 1) * LANES] = k_dup[i].astype(BF16)
            vw_s[new_rows, 2 * g * LANES:(2 * g + 1) * LANES] = v_dup[i].astype(BF16)
            vw_s[new_rows, (2 * g + 1) * LANES:(2 * g + 2) * LANES] = ones

    x1 = x_prev + modp(2) * out_prev
    h2 = _rms_mod(x1, ng_ref[1:2, :], modp(4), modp(3)).astype(BF16)
    _swiglu_in(h2, fin_ref, act_s, range(0, L1_FFN_SPLITS[0]))

    span = 2 * WINDOW
    qi = lax.broadcasted_iota(jnp.int32, (WINDOW, span), 0)
    si = lax.broadcasted_iota(jnp.int32, (WINDOW, span), 1)
    band = (si >= qi) & (si <= qi + WINDOW)
    masks = [band & (si >= WINDOW - (t * tm + j * WINDOW)) for j in range(tm // WINDOW)]
    units = [(j, g) for j in range(tm // WINDOW) for g in range(N_KV_HEADS)]
    rows = lambda j: slice(j * WINDOW, (j + 1) * WINDOW)
    win = lambda j: slice(j * WINDOW, j * WINDOW + span)
    sinks = [[sink_ref[0, g * GROUP + i] for i in range(GROUP)] for g in range(N_KV_HEADS)]
    scores = [_attn_scores([q_split[2 * g + i // 2][i % 2][rows(j), :] for i in range(GROUP)],
                           kw_s[win(j), g * LANES:(g + 1) * LANES]) for j, g in units]
    _swiglu_in(h2, fin_ref, act_s, range(L1_FFN_SPLITS[0], L1_FFN_SPLITS[1]))
    probs = [_attn_probs(s_all, masks[j], sinks[g]) for s_all, (j, g) in zip(scores, units)]
    values = [_attn_values(p_all, vw_s[win(j), 2 * g * LANES:(2 * g + 2) * LANES])
              for (p_all, _), (j, g) in zip(probs, units)]
    _swiglu_in(h2, fin_ref, act_s, range(L1_FFN_SPLITS[1], n_chunks))
    for res_all, (_, maxes), (j, g) in zip(values, probs, units):
        for i, pair in enumerate(_attn_finish(res_all, maxes, sinks[g])):
            col = (2 * g + i) * LANES
            attn_s[rows(j), col:col + LANES] = pair.astype(BF16)

    x2 = x1 + modp(5) * _dot(act_s[...], fout_ref[...])
    y_ref[...] = _rms(x2) * fg_ref[...]


def _prompt_l1_call(sinks, x, k, v, ada, mod_row_block, rope, consts):
    nb, t, d = x.shape
    tm = PROMPT_TILE
    nt = t // tm
    n_tiles = nb * nt
    d_ff = consts[4][0].shape[1]
    cur = lambda s: jnp.minimum(s, n_tiles - 1)
    prv = lambda s: jnp.maximum(s - 1, 0)
    tile_spec = lambda w, f: pl.BlockSpec((None, tm, w), lambda s: (f(s) // nt, lax.rem(f(s), nt), 0))
    return pl.pallas_call(
        functools.partial(_prompt_l1_kernel, nt=nt, n_tiles=n_tiles),
        grid=(n_tiles + 1,),
        in_specs=[pl.BlockSpec(memory_space=pltpu.SMEM),
                  tile_spec(d, cur), tile_spec(KV_DIM, cur), tile_spec(KV_DIM, cur),
                  _resident_rows(ada, 1, mod_row_block),
                  pl.BlockSpec((3, tm, LANES), lambda s: (0, lax.rem(cur(s), nt), 0))]
                 + [_resident(*c) for c in consts],
        out_specs=tile_spec(d, prv),
        out_shape=jax.ShapeDtypeStruct((nb, t, d), F32),
        scratch_shapes=[pltpu.VMEM((WINDOW + tm, N_KV_HEADS * LANES), BF16),
                        pltpu.VMEM((WINDOW + tm, N_KV_HEADS * 2 * LANES), BF16),
                        pltpu.VMEM((tm, d), BF16),
                        pltpu.VMEM((tm, d_ff), BF16),
                        pltpu.VMEM((tm, d), F32)],
        compiler_params=pltpu.CompilerParams(
            dimension_semantics=("arbitrary",), vmem_limit_bytes=VMEM_LIMIT),
        name="prompt_layer1",
    )(sinks, x, k, v, ada, rope, *[c[0] for c in consts])


def _sample_l0_kernel(x_ref, mod_ref, kvmod_ref, qmod_ref, h0_ref, cst_ref, rope_ref, ng_ref, w_in_ref,
                      cw_ref, cb_ref, gw_ref, gb_ref, lam_ref, w_out_ref, fin_ref, fout_ref, kvg_ref,
                      wkv_ref, qg_ref, wq_ref,
                      x2_ref, k_ref, v_ref, q_ref, conv_ref, hl_ref,
                      xr_s, yg_s, a_s, u_s, o_s, act_s):
    nt, sb, d = x_ref.shape
    rows = nt * sb
    slab = lambda t: slice(t * sb, (t + 1) * sb)
    vec = lambda ref, i: _tile_rows(ref[:, i * d:(i + 1) * d], nt)
    mod = lambda i: vec(mod_ref, i)

    x = x_ref[...].reshape(rows, d)
    h = _rms_mod(x, ng_ref[0:1, :], mod(1), mod(0)).astype(BF16)
    _proj_in(h, w_in_ref, xr_s, yg_s)

    def conv_in(j):
        return cst_ref[j] if j < CONV_WIDTH - 1 else xr_s[slab(j - (CONV_WIDTH - 1)), :]

    xc_slabs = []
    for t in range(nt):
        acc = cb_ref[...]
        for j in range(CONV_WIDTH):
            acc = acc + cw_ref[j:j + 1, :] * conv_in(t + j)
        xc_slabs.append(acc)
    xc = jnp.concatenate(xc_slabs, axis=0)
    for j in range(CONV_WIDTH - 1):
        conv_ref[j] = xr_s[slab(nt - (CONV_WIDTH - 1) + j), :]

    _rglru_gates(xc, gw_ref, gb_ref, lam_ref, a_s, u_s)

    hs = h0_ref[...]
    for t in range(nt):
        hs = a_s[slab(t), :] * hs + u_s[slab(t), :]
        o_s[slab(t), :] = hs
    hl_ref[...] = hs

    z = (o_s[...] * _gelu_tanh(yg_s[...])).astype(BF16)
    x1 = x + mod(2) * _dot(z, w_out_ref[...])

    h2 = _rms_mod(x1, ng_ref[1:2, :], mod(4), mod(3)).astype(BF16)
    x2 = x1 + mod(5) * _swiglu(h2, fin_ref, fout_ref, act_s)
    x2_ref[...] = x2.reshape(nt, sb, d)

    hk = _rms_mod(x2, kvg_ref[...], vec(kvmod_ref, 1), vec(kvmod_ref, 0)).astype(BF16)
    kv = _dot(hk, wkv_ref[...])
    hq = _rms_mod(x2, qg_ref[0:1, :], vec(qmod_ref, 1), vec(qmod_ref, 0)).astype(BF16)
    q = _dot(hq, wq_ref[...])
    for t in range(nt):
        c, s_next, s_prev = rope_ref[0, t:t + 1, :], rope_ref[1, t:t + 1, :], rope_ref[2, t:t + 1, :]
        k_ref[t] = _rope(kv[slab(t), :KV_DIM], c, s_next, s_prev)
        q_ref[t] = _rope(q[slab(t), :], c, s_next, s_prev) * (HEAD_DIM ** -0.5)
    v_ref[...] = kv[:, KV_DIM:].reshape(nt, sb, KV_DIM)


def _sample_mod_spec(arr, lead, sb):
    return pl.BlockSpec((None, sb, arr.shape[2]), lambda i: (lead, i, 0))


def _sample_l0_call(x, ada, kv_ada, h0, cst, consts):
    nt, nb, d = x.shape
    sb = SAMPLE_BATCH_TILE
    d_ff = consts[10][0].shape[1]
    rows = nt * sb
    slab_spec = lambda lead, w: pl.BlockSpec((lead, sb, w), lambda i: (0, i, 0))
    return pl.pallas_call(
        _sample_l0_kernel,
        grid=(nb // sb,),
        in_specs=[slab_spec(nt, d), _sample_mod_spec(ada, 0, sb), _sample_mod_spec(kv_ada, 0, sb),
                  _sample_mod_spec(ada, 1, sb),
                  pl.BlockSpec((sb, d), lambda i: (i, 0)), slab_spec(CONV_WIDTH - 1, d)]
                 + [_resident(*c) for c in consts],
        out_specs=[slab_spec(nt, d), slab_spec(nt, KV_DIM), slab_spec(nt, KV_DIM), slab_spec(nt, d),
                   slab_spec(CONV_WIDTH - 1, d), pl.BlockSpec((sb, d), lambda i: (i, 0))],
        out_shape=[jax.ShapeDtypeStruct((nt, nb, d), F32),
                   jax.ShapeDtypeStruct((nt, nb, KV_DIM), F32),
                   jax.ShapeDtypeStruct((nt, nb, KV_DIM), F32),
                   jax.ShapeDtypeStruct((nt, nb, d), F32),
                   jax.ShapeDtypeStruct((CONV_WIDTH - 1, nb, d), F32),
                   jax.ShapeDtypeStruct((nb, d), F32)],
        scratch_shapes=[pltpu.VMEM((rows, d), F32)] * 5 + [pltpu.VMEM((rows, d_ff), BF16)],
        compiler_params=pltpu.CompilerParams(
            dimension_semantics=("arbitrary",), vmem_limit_bytes=VMEM_LIMIT),
        name="sample_layer0",
    )(x, ada, kv_ada, ada, h0, cst, *[c[0] for c in consts])


def _sample_attn_kernel(sink_ref, q_ref, kn_ref, vn_ref, ck_ref, cv_ref,
                        attn_ref, ko_ref, vo_ref):
    sb, nt, d = q_ref.shape
    span = 2 * WINDOW

    tok = lax.broadcasted_iota(jnp.int32, (nt, span), 0)
    si = lax.broadcasted_iota(jnp.int32, (nt, span), 1)
    mask = (si >= tok) & (si <= tok + WINDOW)
    pad = jnp.zeros((HEAD_DIM, LANES - nt), F32)
    ones = jnp.ones((LANES, span), BF16)

    sinks = [[sink_ref[0, g * GROUP + i] for i in range(GROUP)] for g in range(N_KV_HEADS)]

    def windows(b):
        knt, vnt = kn_ref[b].T, vn_ref[b].T
        kwin, vaug = [], []
        for g in range(N_KV_HEADS):
            rows = slice(g * HEAD_DIM, (g + 1) * HEAD_DIM)
            tops = []
            for c_ref, nt_new, o_ref in ((ck_ref, knt[rows, :], ko_ref), (cv_ref, vnt[rows, :], vo_ref)):
                cache = c_ref[b, g]
                o_ref[b, g] = pltpu.roll(cache, WINDOW - nt, 1)
                o_ref[b, g, :, WINDOW - nt:WINDOW] = nt_new
                tops.append(jnp.concatenate([cache, nt_new, pad], axis=1))
            kwin.append(jnp.concatenate([tops[0], tops[0]], axis=0).astype(BF16))
            vaug.append(jnp.concatenate([tops[1].astype(BF16), tops[1].astype(BF16), ones], axis=0))
        return kwin, vaug

    def sequences(it, carry):
        seqs = [it * SAMPLE_ATTN_UNROLL + u for u in range(SAMPLE_ATTN_UNROLL)]
        wins = [windows(b) for b in seqs]
        units = [(u, g) for u in range(SAMPLE_ATTN_UNROLL) for g in range(N_KV_HEADS)]
        q_split = [[_split_halves(q_ref[b][:, p * LANES:(p + 1) * LANES]) for p in range(d // LANES)]
                   for b in seqs]
        scores = [_attn_scores([q_split[u][2 * g + i // 2][i % 2] for i in range(GROUP)], wins[u][0][g], True)
                  for u, g in units]
        probs = [_attn_probs(s_all, mask, sinks[g]) for s_all, (u, g) in zip(scores, units)]
        values = [_attn_values(p_all, wins[u][1][g], True) for (p_all, _), (u, g) in zip(probs, units)]
        pairs = [[] for _ in seqs]
        for res_all, (_, maxes), (u, g) in zip(values, probs, units):
            pairs[u] += _attn_finish(res_all, maxes, sinks[g])
        for u, b in enumerate(seqs):
            attn_ref[b] = jnp.concatenate(pairs[u], axis=1)
        return carry

    lax.fori_loop(0, sb // SAMPLE_ATTN_UNROLL, sequences, 0)


def _sample_attn_call(sinks, q, kn, vn, ck, cv):
    nb, nt, d = q.shape
    sb = SAMPLE_ATTN_BATCH
    seq_spec = lambda r, w: pl.BlockSpec((sb, r, w), lambda i: (i, 0, 0))
    cache_spec = pl.BlockSpec((sb,) + ck.shape[1:], lambda i: (i, 0, 0, 0))
    return pl.pallas_call(
        _sample_attn_kernel,
        grid=(nb // sb,),
        in_specs=[pl.BlockSpec(memory_space=pltpu.SMEM),
                  seq_spec(nt, d), seq_spec(nt, KV_DIM), seq_spec(nt, KV_DIM), cache_spec, cache_spec],
        out_specs=[seq_spec(nt, d), cache_spec, cache_spec],
        out_shape=[jax.ShapeDtypeStruct((nb, nt, d), F32),
                   jax.ShapeDtypeStruct(ck.shape, F32),
                   jax.ShapeDtypeStruct(cv.shape, F32)],
        compiler_params=pltpu.CompilerParams(
            dimension_semantics=("arbitrary",), vmem_limit_bytes=VMEM_LIMIT),
        name="sample_attention",
    )(sinks, q, kn, vn, ck, cv)


def _sample_l1_kernel(x_ref, attn_ref, mod_ref, ng_ref, wo_ref, fin_ref, fout_ref, fg_ref,
                      y_ref, act_s):
    nt, sb, d = x_ref.shape
    rows = nt * sb
    mod = lambda i: _tile_rows(mod_ref[:, i * d:(i + 1) * d], nt)
    x = x_ref[...].reshape(rows, d)
    attn = attn_ref[...].reshape(rows, d).astype(BF16)
    x1 = x + mod(2) * _dot(attn, wo_ref[...])
    h2 = _rms_mod(x1, ng_ref[1:2, :], mod(4), mod(3)).astype(BF16)
    x2 = x1 + mod(5) * _swiglu(h2, fin_ref, fout_ref, act_s)
    y_ref[...] = (_rms(x2) * fg_ref[...]).reshape(nt, sb, d)


def _sample_l1_call(x, attn, ada, consts):
    nt, nb, d = x.shape
    sb = SAMPLE_BATCH_TILE
    d_ff = consts[3][0].shape[1]
    slab_spec = lambda lead: pl.BlockSpec((lead, sb, d), lambda i: (0, i, 0))
    return pl.pallas_call(
        _sample_l1_kernel,
        grid=(nb // sb,),
        in_specs=[slab_spec(nt), slab_spec(nt), _sample_mod_spec(ada, 1, sb)]
                 + [_resident(*c) for c in consts],
        out_specs=slab_spec(nt),
        out_shape=jax.ShapeDtypeStruct((nt, nb, d), F32),
        scratch_shapes=[pltpu.VMEM((nt * sb, d_ff), BF16)],
        compiler_params=pltpu.CompilerParams(
            dimension_semantics=("arbitrary",), vmem_limit_bytes=VMEM_LIMIT),
        name="sample_layer1",
    )(x, attn, ada, *[c[0] for c in consts])


def _rope_tables(pos):
    half = ROT_DIM // 2
    inv = ROPE_THETA ** (-jnp.arange(0, ROT_DIM, 2, dtype=F32) / ROT_DIM)
    ang = pos.astype(F32)[:, None] * inv[None, :]
    cos, sin = jnp.cos(ang), jnp.sin(ang)
    n = pos.shape[0]
    rest = HEAD_DIM - ROT_DIM
    c = jnp.concatenate([cos, cos, jnp.ones((n, rest), F32)], axis=1)
    s_next = jnp.concatenate([-sin, jnp.zeros((n, half + rest), F32)], axis=1)
    s_prev = jnp.concatenate([jnp.zeros((n, half), F32), sin, jnp.zeros((n, rest), F32)], axis=1)
    reps = LANES // HEAD_DIM
    return jnp.stack([jnp.tile(c, (1, reps)), jnp.tile(s_next, (1, reps)), jnp.tile(s_prev, (1, reps))])


def kernel(x_prompt, x_sample, c_prompt, c_sample, state_conv, state_h, cache_k, cache_v, ada_w, ada_b, norm_g, rnn_w_in, rnn_conv_w, rnn_conv_b, rnn_gate_w, rnn_gate_b, rnn_lambda, rnn_w_out, kv_ada_w, kv_ada_b, kv_norm_g, w_kv, attn_w_q, attn_sinks, attn_w_o, ffn_w_in, ffn_w_out, final_g):
    nb_p, t_p, d = x_prompt.shape
    nb_s, t_s, _ = x_sample.shape

    assert rnn_w_in.shape[0] == 1 and attn_w_q.shape[0] == 1 and nb_s % SUBLANES == 0

    c_all = jnp.concatenate([c_sample, c_prompt], axis=0)
    ada = _ada_call(c_all, ada_w, ada_b)
    kv_ada = _ada_call(c_all, kv_ada_w[None], kv_ada_b[None])
    prompt_row_block = nb_s // SUBLANES

    bf = lambda w: w.astype(BF16)
    row = lambda v: v.reshape(1, -1)
    ffn_in, ffn_out = bf(ffn_w_in), bf(ffn_w_out)
    wq, wo = (bf(attn_w_q), 0), (bf(attn_w_o), 0)
    l0_consts = [(norm_g, 0), (bf(rnn_w_in), 0), (rnn_conv_w, 0), (rnn_conv_b, None), (bf(rnn_gate_w), 0),
                 (rnn_gate_b, 0), (rnn_lambda, None), (bf(rnn_w_out), 0), (ffn_in, 0), (ffn_out, 0),
                 (row(kv_norm_g), None), (bf(w_kv), None)]
    l1_ffn = [(ffn_in, 1), (ffn_out, 1), (row(final_g), None)]

    rope_p = _rope_tables(jnp.arange(t_p, dtype=jnp.int32))
    x2_p, k_p, v_p, conv_p, hl_p = _prompt_l0_call(x_prompt, ada, kv_ada, prompt_row_block, rope_p, l0_consts)
    y_prompt = _prompt_l1_call(attn_sinks, x2_p, k_p, v_p, ada, prompt_row_block, rope_p,
                               [(norm_g, 1), wq, wo] + l1_ffn)

    rope_s = _rope_tables(PAST_LEN + jnp.arange(t_s, dtype=jnp.int32))
    x2_s, k_s, v_s, q_s, conv_s, hl_s = _sample_l0_call(
        x_sample.transpose(1, 0, 2), ada, kv_ada, state_h[0], state_conv[0].transpose(1, 0, 2),
        [(rope_s, None)] + l0_consts + [(norm_g, 1), wq])
    attn_s, ko_s, vo_s = _sample_attn_call(
        attn_sinks, q_s.transpose(1, 0, 2), k_s.transpose(1, 0, 2), v_s.transpose(1, 0, 2),
        cache_k.transpose(0, 2, 3, 1), cache_v.transpose(0, 2, 3, 1))
    y_s = _sample_l1_call(x2_s, attn_s.transpose(1, 0, 2), ada, [(norm_g, 1), wo] + l1_ffn)

    kv_shape = (WINDOW, N_KV_HEADS, HEAD_DIM)
    return (y_prompt,
            y_s.transpose(1, 0, 2),
            conv_p[None, :, SUBLANES - (CONV_WIDTH - 1):, :],
            hl_p[None, :, 0, :],
            k_p[:, t_p - WINDOW:, :].reshape((nb_p,) + kv_shape),
            v_p[:, t_p - WINDOW:, :].reshape((nb_p,) + kv_shape),
            conv_s.transpose(1, 0, 2)[None],
            hl_s[None],
            ko_s.transpose(0, 3, 1, 2),
            vo_s.transpose(0, 3, 1, 2))
```

```python
import functools

import jax
import jax.numpy as jnp
from jax import lax
from jax.experimental import pallas as pl
from jax.experimental.pallas import tpu as pltpu

F32 = jnp.float32
BF16 = jnp.bfloat16

D_MODEL = 1024
N_RNN_BLOCKS = 8
RNN_BLOCK = D_MODEL // N_RNN_BLOCKS
CONV_WIDTH = 4
RG_C = 8.0
HEAD_DIM = 64
N_HEADS = D_MODEL // HEAD_DIM
N_KV_HEADS = 4
GROUP = N_HEADS // N_KV_HEADS
KV_DIM = N_KV_HEADS * HEAD_DIM
WINDOW = 128
ROT_DIM = HEAD_DIM // 4
ROPE_THETA = 500000.0
EPS = 1e-6
NEG_INF = -1e30
LOG2_E = 1.4426950408889634
PAST_LEN = 16384

LANES = 128
SUBLANES = 8
MXU_COLS = 256
VMEM_LIMIT = 56 * 1024 * 1024

PROMPT_TILE = 256
FFN_SPLIT = 3
L1_FFN_SPLITS = (3, 7)
ADA_TILE_N = 2048
SAMPLE_ATTN_BATCH = 16
SAMPLE_ATTN_UNROLL = 4
SAMPLE_BATCH_TILE = 32


def _dot(a, b):
    return jnp.dot(a, b, preferred_element_type=F32)


def _dot_nt(a, b):
    return lax.dot_general(a, b, (((1,), (1,)), ((), ())), preferred_element_type=F32)


def _sigmoid(x):
    return 1.0 / (1.0 + jnp.exp2(x * (-LOG2_E)))


def _silu(x):
    return x * _sigmoid(x)


def _gelu_tanh(x):
    return x * (0.5 * (1.0 + jnp.tanh(0.7978845608028654 * (x + 0.044715 * (x * x * x)))))


def _log_sigmoid(x):
    return -(jnp.maximum(-x, 0.0) + jnp.log1p(jnp.exp(-jnp.abs(x))))


def _rms(x):
    return x * lax.rsqrt(jnp.mean(x * x, axis=-1, keepdims=True) + EPS)


def _rms_mod(x, g, scale, shift):
    return (_rms(x) * g) * (1.0 + scale) + shift


def _tile_rows(m, reps):
    return jnp.concatenate([m] * reps, axis=0)


def _rope_block(blk, c, s_next, s_prev):
    return blk * c + pltpu.roll(blk, LANES - ROT_DIM // 2, 1) * s_next + pltpu.roll(blk, ROT_DIM // 2, 1) * s_prev


def _rope(x, c, s_next, s_prev):
    blocks = [_rope_block(x[:, j * LANES:(j + 1) * LANES], c, s_next, s_prev)
              for j in range(x.shape[1] // LANES)]
    return jnp.concatenate(blocks, axis=1)


def _proj_in(h, w_in_ref, xr_s, yg_s):
    d = xr_s.shape[1]
    cw = 2 * MXU_COLS
    for c in range(d // cw):
        xr_s[:, c * cw:(c + 1) * cw] = _dot(h, w_in_ref[:, c * cw:(c + 1) * cw])
        yg_s[:, c * cw:(c + 1) * cw] = _dot(h, w_in_ref[:, d + c * cw:d + (c + 1) * cw])


def _rglru_gates(xc, gw_ref, gb_ref, lam_ref, a_s, u_s):
    xcb = xc.astype(BF16)
    cl = RG_C * _log_sigmoid(lam_ref[...])
    for n in range(N_RNN_BLOCKS):
        blk = slice(n * RNN_BLOCK, (n + 1) * RNN_BLOCK)
        g = _dot(xcb[:, blk], gw_ref[n]) + gb_ref[n:n + 1, :]
        r = _sigmoid(g[:, :RNN_BLOCK])
        i = _sigmoid(g[:, RNN_BLOCK:])
        log_a = cl[:, blk] * r
        a = jnp.exp(log_a)
        a_s[:, blk] = a
        u_s[:, blk] = jnp.sqrt(-jnp.tanh(log_a) * (a * a + 1.0)) * (i * xc[:, blk])


def _swiglu_in(h, fin_ref, act_s, chunks):
    d_ff = act_s.shape[1]
    for c in chunks:
        cols = slice(c * MXU_COLS, (c + 1) * MXU_COLS)
        gate = _dot(h, fin_ref[:, cols])
        up = _dot(h, fin_ref[:, d_ff + c * MXU_COLS:d_ff + (c + 1) * MXU_COLS])
        act_s[:, cols] = (_silu(gate) * up).astype(BF16)


def _swiglu(h, fin_ref, fout_ref, act_s):
    _swiglu_in(h, fin_ref, act_s, range(fout_ref.shape[0] // MXU_COLS))
    return _dot(act_s[...], fout_ref[...])


def _low_half(shape):
    return lax.broadcasted_iota(jnp.int32, shape, 1) < LANES // 2


def _dup_halves(blk):
    low = _low_half(blk.shape)
    rot = pltpu.roll(blk, LANES // 2, 1)
    return jnp.where(low, blk, rot), jnp.where(low, rot, blk)


def _split_halves(blk):
    low = _low_half(blk.shape)
    zero = jnp.zeros_like(blk)
    return jnp.where(low, blk, zero), jnp.where(low, zero, blk)


def _attn_scores(q_heads, kwin, keys_on_lanes=False):
    q_all = jnp.concatenate(q_heads, axis=0).astype(BF16)
    return _dot(q_all, kwin) if keys_on_lanes else _dot_nt(q_all, kwin)


def _attn_probs(s_all, mask, sinks):
    rb = s_all.shape[0] // GROUP
    probs, maxes = [], []
    for i in range(GROUP):
        s = jnp.where(mask, s_all[i * rb:(i + 1) * rb, :], NEG_INF)
        mx = jnp.maximum(jnp.max(s, axis=-1, keepdims=True), sinks[i])
        probs.append(jnp.exp(s - mx))
        maxes.append(mx)
    return jnp.concatenate(probs, axis=0).astype(BF16), maxes


def _attn_values(p_all, vaug, keys_on_lanes=False):
    return _dot_nt(p_all, vaug) if keys_on_lanes else _dot(p_all, vaug)


def _attn_finish(res_all, maxes, sinks):
    rb = res_all.shape[0] // GROUP
    outs = []
    for i in range(GROUP):
        res = res_all[i * rb:(i + 1) * rb, :]
        den = res[:, LANES:] + jnp.exp(sinks[i] - maxes[i])
        outs.append(res[:, :LANES] * (1.0 / den))
    low = _low_half(outs[0].shape)
    return [jnp.where(low, outs[2 * i], outs[2 * i + 1]) for i in range(GROUP // 2)]


def _ada_kernel(c_ref, w_ref, b_ref, o_ref):
    c = c_ref[...]
    o_ref[0] = _dot(_silu(c).astype(BF16), w_ref[0].astype(BF16)) + b_ref[0]


def _ada_call(c, w, b):
    n_layers, d, n = w.shape
    r = c.shape[0]
    return pl.pallas_call(
        _ada_kernel,
        grid=(n_layers, n // ADA_TILE_N),
        in_specs=[pl.BlockSpec((r, d), lambda l, j: (0, 0)),
                  pl.BlockSpec((1, d, ADA_TILE_N), lambda l, j: (l, 0, j)),
                  pl.BlockSpec((1, 1, ADA_TILE_N), lambda l, j: (l, 0, j))],
        out_specs=pl.BlockSpec((1, r, ADA_TILE_N), lambda l, j: (l, 0, j)),
        out_shape=jax.ShapeDtypeStruct((n_layers, r, n), F32),
        compiler_params=pltpu.CompilerParams(
            dimension_semantics=("arbitrary", "arbitrary"), vmem_limit_bytes=VMEM_LIMIT),
        name="ada_mod",
    )(c, w, b.reshape(n_layers, 1, n))


def _prompt_l0_kernel(x_ref, mod_ref, kvmod_ref, rope_ref, ng_ref, w_in_ref, cw_ref, cb_ref, gw_ref,
                      gb_ref, lam_ref, w_out_ref, fin_ref, fout_ref, kvg_ref, wkv_ref,
                      x2_ref, k_ref, v_ref, conv_ref, hl_ref,
                      xr_s, yg_s, act_s, hist_s, hc_s, z_s, xk_s, x2k_s, scan_s, *, nt, n_tiles):
    tm, d = x_ref.shape
    s = pl.program_id(0)
    n_chunks = fout_ref.shape[0] // MXU_COLS
    b_cur = jnp.minimum(s, n_tiles - 1) // nt
    b_prev = jnp.clip(s - 1, 0, n_tiles - 1) // nt
    b_pp = jnp.maximum(s - 2, 0) // nt
    mod = lambda i: _mod_row(mod_ref, b_cur, i)
    modp = lambda i: _mod_row(mod_ref, b_prev, i)

    @pl.when(s == 0)
    def _():
        z_s[...] = jnp.zeros_like(z_s)
        xk_s[...] = jnp.zeros_like(xk_s)
        x2k_s[...] = jnp.zeros_like(x2k_s)

    @pl.when(lax.rem(jnp.minimum(s, n_tiles - 1), nt) == 0)
    def _():
        hist_s[...] = jnp.zeros_like(hist_s)
        hc_s[...] = jnp.zeros_like(hc_s)

    out_prev = _dot(z_s[...], w_out_ref[...])
    x_prev = xk_s[...]
    x2_pp = x2k_s[...]

    hk = _rms_mod(x2_pp, kvg_ref[...], _mod_row(kvmod_ref, b_pp, 1), _mod_row(kvmod_ref, b_pp, 0)).astype(BF16)
    kv = _dot(hk, wkv_ref[...])
    k_ref[...] = _rope(kv[:, :KV_DIM], rope_ref[0], rope_ref[1], rope_ref[2])
    v_ref[...] = kv[:, KV_DIM:]

    x = x_ref[...]
    h = _rms_mod(x, ng_ref[0:1, :], mod(1), mod(0)).astype(BF16)
    _proj_in(h, w_in_ref, xr_s, yg_s)
    xk_s[...] = x

    x1 = x_prev + modp(2) * out_prev
    h2 = _rms_mod(x1, ng_ref[1:2, :], modp(4), modp(3)).astype(BF16)
    _swiglu_in(h2, fin_ref, act_s, range(0, FFN_SPLIT))

    xr = xr_s[...]
    hist = hist_s[...]
    row8 = lax.broadcasted_iota(jnp.int32, (SUBLANES, d), 0)

    def shifted(k):
        rolled = pltpu.roll(xr, k, 0)
        first = jnp.where(row8 >= k, rolled[0:SUBLANES], pltpu.roll(hist, k, 0))
        return jnp.concatenate([first, rolled[SUBLANES:]], axis=0)

    xc = cb_ref[...]
    for j in range(CONV_WIDTH - 1):
        xc = xc + cw_ref[j:j + 1, :] * shifted(CONV_WIDTH - 1 - j)
    xc = xc + cw_ref[CONV_WIDTH - 1:CONV_WIDTH, :] * xr
    hist_s[...] = xr[tm - SUBLANES:]
    xr_s[...] = xc

    cl = RG_C * _log_sigmoid(lam_ref[...])
    row8b = lax.broadcasted_iota(jnp.int32, (SUBLANES, RNN_BLOCK), 0)
    chunk = tm // SUBLANES
    pitch = chunk + SUBLANES

    def recurrent_block(n):
        blk = slice(n * RNN_BLOCK, (n + 1) * RNN_BLOCK)
        a_s, u_s, h_s, p_s = (scan_s.at[n % 2, j] for j in range(4))
        xc_blk = xr_s[:, blk]
        gates = _dot(xc_blk.astype(BF16), gw_ref[n]) + gb_ref[n:n + 1, :]
        r = _sigmoid(gates[:, :RNN_BLOCK])
        i = _sigmoid(gates[:, RNN_BLOCK:])
        log_a = cl[:, blk] * r
        a_all = jnp.exp(log_a)
        u_all = jnp.sqrt(-jnp.tanh(log_a) * (a_all * a_all + 1.0)) * (i * xc_blk)
        for c in range(SUBLANES):
            a_s[c * pitch:c * pitch + chunk, :] = a_all[c * chunk:(c + 1) * chunk, :]
            u_s[c * pitch:c * pitch + chunk, :] = u_all[c * chunk:(c + 1) * chunk, :]
        h = jnp.zeros((SUBLANES, RNN_BLOCK), F32)
        p = jnp.ones((SUBLANES, RNN_BLOCK), F32)
        for g in range(chunk):
            step_rows = pl.ds(g, SUBLANES, stride=pitch)
            a = a_s[step_rows, :]
            h = a * h + u_s[step_rows, :]
            p = a * p
            h_s[step_rows, :] = h
            p_s[step_rows, :] = p
        for step in (1, 2, 4):
            keep = row8b >= step
            h = p * jnp.where(keep, pltpu.roll(h, step, 0), 0.0) + h
            p = p * jnp.where(keep, pltpu.roll(p, step, 0), 1.0)
        hprev = hc_s[:, blk]
        ends = h + p * hprev
        incoming = jnp.where(row8b >= 1, pltpu.roll(ends, 1, 0), hprev)
        hc_s[:, blk] = jnp.broadcast_to(ends[SUBLANES - 1:SUBLANES, :], ends.shape)
        for c in range(SUBLANES):
            rows = slice(c * chunk, (c + 1) * chunk)
            local = slice(c * pitch, c * pitch + chunk)
            hs = h_s[local, :] + p_s[local, :] * incoming[c:c + 1, :]
            z_s[rows, blk] = (hs * _gelu_tanh(yg_s[rows, blk])).astype(BF16)

    blocks = list(range(N_RNN_BLOCKS))
    for c in range(FFN_SPLIT, n_chunks):
        _swiglu_in(h2, fin_ref, act_s, [c])
        if blocks:
            recurrent_block(blocks.pop(0))
    ffn = _dot(act_s[...], fout_ref[...])
    for n in blocks:
        recurrent_block(n)

    x2 = jnp.where(s <= n_tiles, x1 + modp(5) * ffn, x2_pp)
    x2_ref[...] = x2
    x2k_s[...] = x2

    @pl.when(s < n_tiles)
    def _():
        conv_ref[...] = hist_s[...]
        hl_ref[...] = hc_s[...]


def _resident(arr, lead=None):
    if lead is None:
        index, shape = (0,) * arr.ndim, arr.shape
    else:
        index, shape = (lead,) + (0,) * (arr.ndim - 1), (None,) + arr.shape[1:]
    return pl.BlockSpec(shape, lambda *_: index, pipeline_mode=pl.Buffered(1))


def _prompt_mod_specs(ada, kv_ada, layer, row_block):
    return [_resident_rows(ada, layer, row_block), _resident_rows(kv_ada, 0, row_block)]


def _resident_rows(arr, lead, row_block):
    index = (lead, row_block, 0)
    return pl.BlockSpec((None, SUBLANES, arr.shape[2]), lambda *_: index, pipeline_mode=pl.Buffered(1))


def _mod_row(ref, b, i):
    return ref[pl.ds(b, 1), i * D_MODEL:(i + 1) * D_MODEL]


def _prompt_l0_call(x, ada, kv_ada, mod_row_block, rope, consts):
    nb, t, d = x.shape
    tm = PROMPT_TILE
    nt = t // tm
    n_tiles = nb * nt
    d_ff = consts[9][0].shape[1]
    cur = lambda s: jnp.minimum(s, n_tiles - 1)
    prv = lambda s: jnp.clip(s - 1, 0, n_tiles - 1)
    pp = lambda s: jnp.maximum(s - 2, 0)
    tile_spec = lambda w, f: pl.BlockSpec((None, tm, w), lambda s: (f(s) // nt, lax.rem(f(s), nt), 0))
    state_spec = pl.BlockSpec((None, SUBLANES, d), lambda s: (cur(s) // nt, 0, 0))
    return pl.pallas_call(
        functools.partial(_prompt_l0_kernel, nt=nt, n_tiles=n_tiles),
        grid=(n_tiles + 2,),
        in_specs=[tile_spec(d, cur)] + _prompt_mod_specs(ada, kv_ada, 0, mod_row_block)
                 + [pl.BlockSpec((3, tm, LANES), lambda s: (0, lax.rem(pp(s), nt), 0))]
                 + [_resident(*c) for c in consts],
        out_specs=[tile_spec(d, prv), tile_spec(KV_DIM, pp), tile_spec(KV_DIM, pp),
                   state_spec, state_spec],
        out_shape=[jax.ShapeDtypeStruct((nb, t, d), F32),
                   jax.ShapeDtypeStruct((nb, t, KV_DIM), F32),
                   jax.ShapeDtypeStruct((nb, t, KV_DIM), F32),
                   jax.ShapeDtypeStruct((nb, SUBLANES, d), F32),
                   jax.ShapeDtypeStruct((nb, SUBLANES, d), F32)],
        scratch_shapes=[pltpu.VMEM((tm, d), F32), pltpu.VMEM((tm, d), F32),
                        pltpu.VMEM((tm, d_ff), BF16),
                        pltpu.VMEM((SUBLANES, d), F32), pltpu.VMEM((SUBLANES, d), F32),
                        pltpu.VMEM((tm, d), BF16), pltpu.VMEM((tm, d), F32), pltpu.VMEM((tm, d), F32),
                        pltpu.VMEM((2, 4, tm + SUBLANES * SUBLANES, RNN_BLOCK), F32)],
        compiler_params=pltpu.CompilerParams(
            dimension_semantics=("arbitrary",), vmem_limit_bytes=VMEM_LIMIT),
        name="prompt_layer0",
    )(x, ada, kv_ada, rope, *[c[0] for c in consts])


def _prompt_l1_kernel(sink_ref, x_ref, k_ref, v_ref, mod_ref, rope_ref, ng_ref, wq_ref, wo_ref,
                      fin_ref, fout_ref, fg_ref,
                      y_ref,
                      kw_s, vw_s, attn_s, act_s, xk_s, *, nt, n_tiles):
    tm, d = x_ref.shape
    s = pl.program_id(0)
    n_chunks = fout_ref.shape[0] // MXU_COLS
    cur_tile = jnp.minimum(s, n_tiles - 1)
    t = lax.rem(cur_tile, nt)
    b_cur = cur_tile // nt
    b_prev = jnp.maximum(s - 1, 0) // nt
    mod = lambda i: _mod_row(mod_ref, b_cur, i)
    modp = lambda i: _mod_row(mod_ref, b_prev, i)

    @pl.when(s == 0)
    def _():
        attn_s[...] = jnp.zeros_like(attn_s)
        xk_s[...] = jnp.zeros_like(xk_s)

    @pl.when(t == 0)
    def _():
        kw_s[0:WINDOW, :] = jnp.zeros((WINDOW, kw_s.shape[1]), BF16)
        vw_s[0:WINDOW, :] = jnp.zeros((WINDOW, vw_s.shape[1]), BF16)

    @pl.when(t > 0)
    def _():
        kw_s[0:WINDOW, :] = kw_s[tm:tm + WINDOW, :]
        vw_s[0:WINDOW, :] = vw_s[tm:tm + WINDOW, :]

    out_prev = _dot(attn_s[...], wo_ref[...])
    x_prev = xk_s[...]

    x = x_ref[...]
    h = _rms_mod(x, ng_ref[0:1, :], mod(1), mod(0)).astype(BF16)
    q = _rope(_dot(h, wq_ref[...]), rope_ref[0], rope_ref[1], rope_ref[2]) * (HEAD_DIM ** -0.5)
    q_split = [_split_halves(q[:, p * LANES:(p + 1) * LANES]) for p in range(d // LANES)]
    xk_s[...] = x

    new_rows = slice(WINDOW, WINDOW + tm)
    k, v = k_ref[...], v_ref[...]
    ones = jnp.ones((tm, LANES), BF16)
    for pb in range(KV_DIM // LANES):
        k_dup = _dup_halves(k[:, pb * LANES:(pb + 1) * LANES])
        v_dup = _dup_halves(v[:, pb * LANES:(pb + 1) * LANES])
        for i in range(2):
            g = 2 * pb + i
            kw_s[new_rows, g * LANES:(g + 1) * LANES] = k_dup[i].astype(BF16)
            vw_s[new_rows, 2 * g * LANES:(2 * g + 1) * LANES] = v_dup[i].astype(BF16)
            vw_s[new_rows, (2 * g + 1) * LANES:(2 * g + 2) * LANES] = ones

    x1 = x_prev + modp(2) * out_prev
    h2 = _rms_mod(x1, ng_ref[1:2, :], modp(4), modp(3)).astype(BF16)
    _swiglu_in(h2, fin_ref, act_s, range(0, L1_FFN_SPLITS[0]))

    span = 2 * WINDOW
    qi = lax.broadcasted_iota(jnp.int32, (WINDOW, span), 0)
    si = lax.broadcasted_iota(jnp.int32, (WINDOW, span), 1)
    band = (si >= qi) & (si <= qi + WINDOW)
    masks = [band & (si >= WINDOW - (t * tm + j * WINDOW)) for j in range(tm // WINDOW)]
    units = [(j, g) for j in range(tm // WINDOW) for g in range(N_KV_HEADS)]
    rows = lambda j: slice(j * WINDOW, (j + 1) * WINDOW)
    win = lambda j: slice(j * WINDOW, j * WINDOW + span)
    sinks = [[sink_ref[0, g * GROUP + i] for i in range(GROUP)] for g in range(N_KV_HEADS)]
    scores = [_attn_scores([q_split[2 * g + i // 2][i % 2][rows(j), :] for i in range(GROUP)],
                           kw_s[win(j), g * LANES:(g + 1) * LANES]) for j, g in units]
    _swiglu_in(h2, fin_ref, act_s, range(L1_FFN_SPLITS[0], L1_FFN_SPLITS[1]))
    probs = [_attn_probs(s_all, masks[j], sinks[g]) for s_all, (j, g) in zip(scores, units)]
    values = [_attn_values(p_all, vw_s[win(j), 2 * g * LANES:(2 * g + 2) * LANES])
              for (p_all, _), (j, g) in zip(probs, units)]
    _swiglu_in(h2, fin_ref, act_s, range(L1_FFN_SPLITS[1], n_chunks))
    for res_all, (_, maxes), (j, g) in zip(values, probs, units):
        for i, pair in enumerate(_attn_finish(res_all, maxes, sinks[g])):
            col = (2 * g + i) * LANES
            attn_s[rows(j), col:col + LANES] = pair.astype(BF16)

    x2 = x1 + modp(5) * _dot(act_s[...], fout_ref[...])
    y_ref[...] = _rms(x2) * fg_ref[...]


def _prompt_l1_call(sinks, x, k, v, ada, mod_row_block, rope, consts):
    nb, t, d = x.shape
    tm = PROMPT_TILE
    nt = t // tm
    n_tiles = nb * nt
    d_ff = consts[4][0].shape[1]
    cur = lambda s: jnp.minimum(s, n_tiles - 1)
    prv = lambda s: jnp.maximum(s - 1, 0)
    tile_spec = lambda w, f: pl.BlockSpec((None, tm, w), lambda s: (f(s) // nt, lax.rem(f(s), nt), 0))
    return pl.pallas_call(
        functools.partial(_prompt_l1_kernel, nt=nt, n_tiles=n_tiles),
        grid=(n_tiles + 1,),
        in_specs=[pl.BlockSpec(memory_space=pltpu.SMEM),
                  tile_spec(d, cur), tile_spec(KV_DIM, cur), tile_spec(KV_DIM, cur),
                  _resident_rows(ada, 1, mod_row_block),
                  pl.BlockSpec((3, tm, LANES), lambda s: (0, lax.rem(cur(s), nt), 0))]
                 + [_resident(*c) for c in consts],
        out_specs=tile_spec(d, prv),
        out_shape=jax.ShapeDtypeStruct((nb, t, d), F32),
        scratch_shapes=[pltpu.VMEM((WINDOW + tm, N_KV_HEADS * LANES), BF16),
                        pltpu.VMEM((WINDOW + tm, N_KV_HEADS * 2 * LANES), BF16),
                        pltpu.VMEM((tm, d), BF16),
                        pltpu.VMEM((tm, d_ff), BF16),
                        pltpu.VMEM((tm, d), F32)],
        compiler_params=pltpu.CompilerParams(
            dimension_semantics=("arbitrary",), vmem_limit_bytes=VMEM_LIMIT),
        name="prompt_layer1",
    )(sinks, x, k, v, ada, rope, *[c[0] for c in consts])


def _sample_l0_kernel(x_ref, mod_ref, kvmod_ref, qmod_ref, h0_ref, cst_ref, rope_ref, ng_ref, w_in_ref,
                      cw_ref, cb_ref, gw_ref, gb_ref, lam_ref, w_out_ref, fin_ref, fout_ref, kvg_ref,
                      wkv_ref, qg_ref, wq_ref,
                      x2_ref, k_ref, v_ref, q_ref, conv_ref, hl_ref,
                      xr_s, yg_s, a_s, u_s, o_s, act_s):
    sb, nt, d = x_ref.shape
    rows = nt * sb
    slab = lambda t: slice(t * sb, (t + 1) * sb)
    vec = lambda ref, i: _tile_rows(ref[:, i * d:(i + 1) * d], nt)
    mod = lambda i: vec(mod_ref, i)
    seq_major = lambda a: jnp.swapaxes(a.reshape(nt, sb, a.shape[-1]), 0, 1)

    x = jnp.swapaxes(x_ref[...], 0, 1).reshape(rows, d)
    h = _rms_mod(x, ng_ref[0:1, :], mod(1), mod(0)).astype(BF16)
    _proj_in(h, w_in_ref, xr_s, yg_s)

    def conv_in(j):
        return cst_ref[j] if j < CONV_WIDTH - 1 else xr_s[slab(j - (CONV_WIDTH - 1)), :]

    xc_slabs = []
    for t in range(nt):
        acc = cb_ref[...]
        for j in range(CONV_WIDTH):
            acc = acc + cw_ref[j:j + 1, :] * conv_in(t + j)
        xc_slabs.append(acc)
    xc = jnp.concatenate(xc_slabs, axis=0)
    for j in range(CONV_WIDTH - 1):
        conv_ref[j] = xr_s[slab(nt - (CONV_WIDTH - 1) + j), :]

    _rglru_gates(xc, gw_ref, gb_ref, lam_ref, a_s, u_s)

    hs = h0_ref[...]
    for t in range(nt):
        hs = a_s[slab(t), :] * hs + u_s[slab(t), :]
        o_s[slab(t), :] = hs
    hl_ref[...] = hs

    z = (o_s[...] * _gelu_tanh(yg_s[...])).astype(BF16)
    x1 = x + mod(2) * _dot(z, w_out_ref[...])

    h2 = _rms_mod(x1, ng_ref[1:2, :], mod(4), mod(3)).astype(BF16)
    x2 = x1 + mod(5) * _swiglu(h2, fin_ref, fout_ref, act_s)
    x2_ref[...] = x2.reshape(nt, sb, d)

    hk = _rms_mod(x2, kvg_ref[...], vec(kvmod_ref, 1), vec(kvmod_ref, 0)).astype(BF16)
    kv = _dot(hk, wkv_ref[...])
    hq = _rms_mod(x2, qg_ref[0:1, :], vec(qmod_ref, 1), vec(qmod_ref, 0)).astype(BF16)
    q = _dot(hq, wq_ref[...])
    k_slabs, q_slabs = [], []
    for t in range(nt):
        c, s_next, s_prev = rope_ref[0, t:t + 1, :], rope_ref[1, t:t + 1, :], rope_ref[2, t:t + 1, :]
        k_slabs.append(_rope(kv[slab(t), :KV_DIM], c, s_next, s_prev))
        q_slabs.append(_rope(q[slab(t), :], c, s_next, s_prev) * (HEAD_DIM ** -0.5))
    k_ref[...] = seq_major(jnp.concatenate(k_slabs, axis=0))
    q_ref[...] = seq_major(jnp.concatenate(q_slabs, axis=0))
    v_ref[...] = seq_major(kv[:, KV_DIM:])


def _sample_mod_spec(arr, lead, sb):
    return pl.BlockSpec((None, sb, arr.shape[2]), lambda i: (lead, i, 0))


def _sample_l0_call(x, ada, kv_ada, h0, cst, consts):
    nb, nt, d = x.shape
    sb = SAMPLE_BATCH_TILE
    d_ff = consts[10][0].shape[1]
    rows = nt * sb
    slab_spec = lambda lead, w: pl.BlockSpec((lead, sb, w), lambda i: (0, i, 0))
    seq_spec = lambda w: pl.BlockSpec((sb, nt, w), lambda i: (i, 0, 0))
    return pl.pallas_call(
        _sample_l0_kernel,
        grid=(nb // sb,),
        in_specs=[seq_spec(d), _sample_mod_spec(ada, 0, sb), _sample_mod_spec(kv_ada, 0, sb),
                  _sample_mod_spec(ada, 1, sb),
                  pl.BlockSpec((sb, d), lambda i: (i, 0)), slab_spec(CONV_WIDTH - 1, d)]
                 + [_resident(*c) for c in consts],
        out_specs=[slab_spec(nt, d), seq_spec(KV_DIM), seq_spec(KV_DIM), seq_spec(d),
                   slab_spec(CONV_WIDTH - 1, d), pl.BlockSpec((sb, d), lambda i: (i, 0))],
        out_shape=[jax.ShapeDtypeStruct((nt, nb, d), F32),
                   jax.ShapeDtypeStruct((nb, nt, KV_DIM), F32),
                   jax.ShapeDtypeStruct((nb, nt, KV_DIM), F32),
                   jax.ShapeDtypeStruct((nb, nt, d), F32),
                   jax.ShapeDtypeStruct((CONV_WIDTH - 1, nb, d), F32),
                   jax.ShapeDtypeStruct((nb, d), F32)],
        scratch_shapes=[pltpu.VMEM((rows, d), F32)] * 5 + [pltpu.VMEM((rows, d_ff), BF16)],
        compiler_params=pltpu.CompilerParams(
            dimension_semantics=("arbitrary",), vmem_limit_bytes=VMEM_LIMIT),
        name="sample_layer0",
    )(x, ada, kv_ada, ada, h0, cst, *[c[0] for c in consts])


def _sample_attn_kernel(sink_ref, q_ref, kn_ref, vn_ref, ck_ref, cv_ref,
                        attn_ref, ko_ref, vo_ref):
    sb, nt, d = q_ref.shape
    span = 2 * WINDOW

    tok = lax.broadcasted_iota(jnp.int32, (nt, span), 0)
    si = lax.broadcasted_iota(jnp.int32, (nt, span), 1)
    mask = (si >= tok) & (si <= tok + WINDOW)
    pad = jnp.zeros((HEAD_DIM, LANES - nt), F32)
    ones = jnp.ones((LANES, span), BF16)

    sinks = [[sink_ref[0, g * GROUP + i] for i in range(GROUP)] for g in range(N_KV_HEADS)]

    def windows(b):
        knt, vnt = kn_ref[b].T, vn_ref[b].T
        kwin, vaug = [], []
        for g in range(N_KV_HEADS):
            rows = slice(g * HEAD_DIM, (g + 1) * HEAD_DIM)
            tops = []
            for c_ref, nt_new, o_ref in ((ck_ref, knt[rows, :], ko_ref), (cv_ref, vnt[rows, :], vo_ref)):
                cache = c_ref[b, g]
                o_ref[b, g] = pltpu.roll(cache, WINDOW - nt, 1)
                o_ref[b, g, :, WINDOW - nt:WINDOW] = nt_new
                tops.append(jnp.concatenate([cache, nt_new, pad], axis=1))
            kwin.append(jnp.concatenate([tops[0], tops[0]], axis=0).astype(BF16))
            vaug.append(jnp.concatenate([tops[1].astype(BF16), tops[1].astype(BF16), ones], axis=0))
        return kwin, vaug

    def sequences(it, carry):
        seqs = [it * SAMPLE_ATTN_UNROLL + u for u in range(SAMPLE_ATTN_UNROLL)]
        wins = [windows(b) for b in seqs]
        units = [(u, g) for u in range(SAMPLE_ATTN_UNROLL) for g in range(N_KV_HEADS)]
        q_split = [[_split_halves(q_ref[b][:, p * LANES:(p + 1) * LANES]) for p in range(d // LANES)]
                   for b in seqs]
        scores = [_attn_scores([q_split[u][2 * g + i // 2][i % 2] for i in range(GROUP)], wins[u][0][g], True)
                  for u, g in units]
        probs = [_attn_probs(s_all, mask, sinks[g]) for s_all, (u, g) in zip(scores, units)]
        values = [_attn_values(p_all, wins[u][1][g], True) for (p_all, _), (u, g) in zip(probs, units)]
        pairs = [[] for _ in seqs]
        for res_all, (_, maxes), (u, g) in zip(values, probs, units):
            pairs[u] += _attn_finish(res_all, maxes, sinks[g])
        for u, b in enumerate(seqs):
            attn_ref[b] = jnp.concatenate(pairs[u], axis=1)
        return carry

    lax.fori_loop(0, sb // SAMPLE_ATTN_UNROLL, sequences, 0)


def _sample_attn_call(sinks, q, kn, vn, ck, cv):
    nb, nt, d = q.shape
    sb = SAMPLE_ATTN_BATCH
    seq_spec = lambda r, w: pl.BlockSpec((sb, r, w), lambda i: (i, 0, 0))
    cache_spec = pl.BlockSpec((sb,) + ck.shape[1:], lambda i: (i, 0, 0, 0))
    return pl.pallas_call(
        _sample_attn_kernel,
        grid=(nb // sb,),
        in_specs=[pl.BlockSpec(memory_space=pltpu.SMEM),
                  seq_spec(nt, d), seq_spec(nt, KV_DIM), seq_spec(nt, KV_DIM), cache_spec, cache_spec],
        out_specs=[seq_spec(nt, d), cache_spec, cache_spec],
        out_shape=[jax.ShapeDtypeStruct((nb, nt, d), F32),
                   jax.ShapeDtypeStruct(ck.shape, F32),
                   jax.ShapeDtypeStruct(cv.shape, F32)],
        compiler_params=pltpu.CompilerParams(
            dimension_semantics=("arbitrary",), vmem_limit_bytes=VMEM_LIMIT),
        name="sample_attention",
    )(sinks, q, kn, vn, ck, cv)


def _sample_l1_kernel(x_ref, attn_ref, mod_ref, ng_ref, wo_ref, fin_ref, fout_ref, fg_ref,
                      y_ref, act_s):
    nt, sb, d = x_ref.shape
    rows = nt * sb
    mod = lambda i: _tile_rows(mod_ref[:, i * d:(i + 1) * d], nt)
    x = x_ref[...].reshape(rows, d)
    attn = jnp.swapaxes(attn_ref[...], 0, 1).reshape(rows, d).astype(BF16)
    x1 = x + mod(2) * _dot(attn, wo_ref[...])
    h2 = _rms_mod(x1, ng_ref[1:2, :], mod(4), mod(3)).astype(BF16)
    x2 = x1 + mod(5) * _swiglu(h2, fin_ref, fout_ref, act_s)
    y_ref[...] = jnp.swapaxes((_rms(x2) * fg_ref[...]).reshape(nt, sb, d), 0, 1)


def _sample_l1_call(x, attn, ada, consts):
    nt, nb, d = x.shape
    sb = SAMPLE_BATCH_TILE
    d_ff = consts[3][0].shape[1]
    slab_spec = pl.BlockSpec((nt, sb, d), lambda i: (0, i, 0))
    seq_spec = pl.BlockSpec((sb, nt, d), lambda i: (i, 0, 0))
    return pl.pallas_call(
        _sample_l1_kernel,
        grid=(nb // sb,),
        in_specs=[slab_spec, seq_spec, _sample_mod_spec(ada, 1, sb)]
                 + [_resident(*c) for c in consts],
        out_specs=seq_spec,
        out_shape=jax.ShapeDtypeStruct((nb, nt, d), F32),
        scratch_shapes=[pltpu.VMEM((nt * sb, d_ff), BF16)],
        compiler_params=pltpu.CompilerParams(
            dimension_semantics=("arbitrary",), vmem_limit_bytes=VMEM_LIMIT),
        name="sample_layer1",
    )(x, attn, ada, *[c[0] for c in consts])


def _rope_tables(pos):
    half = ROT_DIM // 2
    inv = ROPE_THETA ** (-jnp.arange(0, ROT_DIM, 2, dtype=F32) / ROT_DIM)
    ang = pos.astype(F32)[:, None] * inv[None, :]
    cos, sin = jnp.cos(ang), jnp.sin(ang)
    n = pos.shape[0]
    rest = HEAD_DIM - ROT_DIM
    c = jnp.concatenate([cos, cos, jnp.ones((n, rest), F32)], axis=1)
    s_next = jnp.concatenate([-sin, jnp.zeros((n, half + rest), F32)], axis=1)
    s_prev = jnp.concatenate([jnp.zeros((n, half), F32), sin, jnp.zeros((n, rest), F32)], axis=1)
    reps = LANES // HEAD_DIM
    return jnp.stack([jnp.tile(c, (1, reps)), jnp.tile(s_next, (1, reps)), jnp.tile(s_prev, (1, reps))])


def kernel(x_prompt, x_sample, c_prompt, c_sample, state_conv, state_h, cache_k, cache_v, ada_w, ada_b, norm_g, rnn_w_in, rnn_conv_w, rnn_conv_b, rnn_gate_w, rnn_gate_b, rnn_lambda, rnn_w_out, kv_ada_w, kv_ada_b, kv_norm_g, w_kv, attn_w_q, attn_sinks, attn_w_o, ffn_w_in, ffn_w_out, final_g):
    nb_p, t_p, d = x_prompt.shape
    nb_s, t_s, _ = x_sample.shape

    assert rnn_w_in.shape[0] == 1 and attn_w_q.shape[0] == 1 and nb_s % SUBLANES == 0

    c_all = jnp.concatenate([c_sample, c_prompt], axis=0)
    ada = _ada_call(c_all, ada_w, ada_b)
    kv_ada = _ada_call(c_all, kv_ada_w[None], kv_ada_b[None])
    prompt_row_block = nb_s // SUBLANES

    bf = lambda w: w.astype(BF16)
    row = lambda v: v.reshape(1, -1)
    ffn_in, ffn_out = bf(ffn_w_in), bf(ffn_w_out)
    wq, wo = (bf(attn_w_q), 0), (bf(attn_w_o), 0)
    l0_consts = [(norm_g, 0), (bf(rnn_w_in), 0), (rnn_conv_w, 0), (rnn_conv_b, None), (bf(rnn_gate_w), 0),
                 (rnn_gate_b, 0), (rnn_lambda, None), (bf(rnn_w_out), 0), (ffn_in, 0), (ffn_out, 0),
                 (row(kv_norm_g), None), (bf(w_kv), None)]
    l1_ffn = [(ffn_in, 1), (ffn_out, 1), (row(final_g), None)]

    rope_p = _rope_tables(jnp.arange(t_p, dtype=jnp.int32))
    x2_p, k_p, v_p, conv_p, hl_p = _prompt_l0_call(x_prompt, ada, kv_ada, prompt_row_block, rope_p, l0_consts)
    y_prompt = _prompt_l1_call(attn_sinks, x2_p, k_p, v_p, ada, prompt_row_block, rope_p,
                               [(norm_g, 1), wq, wo] + l1_ffn)

    rope_s = _rope_tables(PAST_LEN + jnp.arange(t_s, dtype=jnp.int32))
    x2_s, k_s, v_s, q_s, conv_s, hl_s = _sample_l0_call(
        x_sample, ada, kv_ada, state_h[0], state_conv[0].transpose(1, 0, 2),
        [(rope_s, None)] + l0_consts + [(norm_g, 1), wq])
    attn_s, ko_s, vo_s = _sample_attn_call(
        attn_sinks, q_s, k_s, v_s, cache_k.transpose(0, 2, 3, 1), cache_v.transpose(0, 2, 3, 1))
    y_s = _sample_l1_call(x2_s, attn_s, ada, [(norm_g, 1), wo] + l1_ffn)

    kv_shape = (WINDOW, N_KV_HEADS, HEAD_DIM)
    return (y_prompt,
            y_s,
            conv_p[None, :, SUBLANES - (CONV_WIDTH - 1):, :],
            hl_p[None, :, 0, :],
            k_p[:, t_p - WINDOW:, :].reshape((nb_p,) + kv_shape),
            v_p[:, t_p - WINDOW:, :].reshape((nb_p,) + kv_shape),
            conv_s.transpose(1, 0, 2)[None],
            hl_s[None],
            ko_s.transpose(0, 3, 1, 2),
            vo_s.transpose(0, 3, 1, 2))
```

```python
import functools

import jax
import jax.numpy as jnp
from jax import lax
from jax.experimental import pallas as pl
from jax.experimental.pallas import tpu as pltpu

F32 = jnp.float32
BF16 = jnp.bfloat16

D_MODEL = 1024
N_RNN_BLOCKS = 8
RNN_BLOCK = D_MODEL // N_RNN_BLOCKS
CONV_WIDTH = 4
RG_C = 8.0
HEAD_DIM = 64
N_HEADS = D_MODEL // HEAD_DIM
N_KV_HEADS = 4
GROUP = N_HEADS // N_KV_HEADS
KV_DIM = N_KV_HEADS * HEAD_DIM
WINDOW = 128
ROT_DIM = HEAD_DIM // 4
ROPE_THETA = 500000.0
EPS = 1e-6
NEG_INF = -1e30
LOG2_E = 1.4426950408889634
GELU_C = 0.7978845608028654
GELU_K = 0.044715
PAST_LEN = 16384

LANES = 128
SUBLANES = 8
MXU_COLS = 256
VMEM_LIMIT = 56 * 1024 * 1024

PROMPT_TILE = 256
FFN_SPLIT = 3
L1_FFN_SPLITS = (3, 7)
ADA_TILE_N = 1024
SAMPLE_ATTN_BATCH = 16
SAMPLE_ATTN_UNROLL = 4
SAMPLE_BATCH_TILE = 32


def _dot(a, b):
    return jnp.dot(a, b, preferred_element_type=F32)


def _dot_nt(a, b):
    return lax.dot_general(a, b, (((1,), (1,)), ((), ())), preferred_element_type=F32)


def _sigmoid(x):
    return 1.0 / (1.0 + jnp.exp2(x * (-LOG2_E)))


def _silu(x):
    return x * _sigmoid(x)


def _gelu_tanh(x):
    half = 0.5 * x
    return half + half * jnp.tanh(x * (GELU_C + (GELU_C * GELU_K) * (x * x)))


def _log_sigmoid(x):
    return -(jnp.maximum(-x, 0.0) + jnp.log1p(jnp.exp(-jnp.abs(x))))


def _rms(x):
    return x * lax.rsqrt(jnp.mean(x * x, axis=-1, keepdims=True) + EPS)


def _rms_mod(x, g, scale, shift):
    return _rms(x) * (g * (1.0 + scale)) + shift


def _tile_rows(m, reps):
    return jnp.concatenate([m] * reps, axis=0)


def _rope_block(blk, c, s_next, s_prev):
    return blk * c + pltpu.roll(blk, LANES - ROT_DIM // 2, 1) * s_next + pltpu.roll(blk, ROT_DIM // 2, 1) * s_prev


def _rope(x, c, s_next, s_prev):
    blocks = [_rope_block(x[:, j * LANES:(j + 1) * LANES], c, s_next, s_prev)
              for j in range(x.shape[1] // LANES)]
    return jnp.concatenate(blocks, axis=1)


def _proj_in(h, w_in_ref, xr_s, yg_s):
    d = xr_s.shape[1]
    cw = 2 * MXU_COLS
    for c in range(d // cw):
        xr_s[:, c * cw:(c + 1) * cw] = _dot(h, w_in_ref[:, c * cw:(c + 1) * cw])
        yg_s[:, c * cw:(c + 1) * cw] = _dot(h, w_in_ref[:, d + c * cw:d + (c + 1) * cw])


def _rglru_gates(xc, gw_ref, gb_ref, lam_ref, a_s, u_s):
    xcb = xc.astype(BF16)
    cl = RG_C * _log_sigmoid(lam_ref[...])
    for n in range(N_RNN_BLOCKS):
        blk = slice(n * RNN_BLOCK, (n + 1) * RNN_BLOCK)
        g = _dot(xcb[:, blk], gw_ref[n]) + gb_ref[n:n + 1, :]
        r = _sigmoid(g[:, :RNN_BLOCK])
        i = _sigmoid(g[:, RNN_BLOCK:])
        log_a = cl[:, blk] * r
        a = jnp.exp(log_a)
        a_s[:, blk] = a
        u_s[:, blk] = jnp.sqrt(-jnp.tanh(log_a) * (a * a + 1.0)) * (i * xc[:, blk])


def _swiglu_in(h, fin_ref, act_s, chunks):
    d_ff = act_s.shape[1]
    for c in chunks:
        cols = slice(c * MXU_COLS, (c + 1) * MXU_COLS)
        gate = _dot(h, fin_ref[:, cols])
        up = _dot(h, fin_ref[:, d_ff + c * MXU_COLS:d_ff + (c + 1) * MXU_COLS])
        act_s[:, cols] = (_silu(gate) * up).astype(BF16)


def _swiglu(h, fin_ref, fout_ref, act_s):
    _swiglu_in(h, fin_ref, act_s, range(fout_ref.shape[0] // MXU_COLS))
    return _dot(act_s[...], fout_ref[...])


def _low_half(shape):
    return lax.broadcasted_iota(jnp.int32, shape, 1) < LANES // 2


def _dup_halves(blk):
    low = _low_half(blk.shape)
    rot = pltpu.roll(blk, LANES // 2, 1)
    return jnp.where(low, blk, rot), jnp.where(low, rot, blk)


def _split_halves(blk):
    low = _low_half(blk.shape)
    zero = jnp.zeros_like(blk)
    return jnp.where(low, blk, zero), jnp.where(low, zero, blk)


def _attn_scores(q_heads, kwin, keys_on_lanes=False):
    q_all = jnp.concatenate(q_heads, axis=0).astype(BF16)
    return _dot(q_all, kwin) if keys_on_lanes else _dot_nt(q_all, kwin)


def _attn_probs(s_all, mask, sinks):
    rb = s_all.shape[0] // GROUP
    probs, maxes = [], []
    for i in range(GROUP):
        s = jnp.where(mask, s_all[i * rb:(i + 1) * rb, :], NEG_INF)
        mx = jnp.maximum(jnp.max(s, axis=-1, keepdims=True), sinks[i])
        probs.append(jnp.exp(s - mx))
        maxes.append(mx)
    return jnp.concatenate(probs, axis=0).astype(BF16), maxes


def _attn_values(p_all, vaug, keys_on_lanes=False):
    return _dot_nt(p_all, vaug) if keys_on_lanes else _dot(p_all, vaug)


def _attn_finish(res_all, maxes, sinks):
    rb = res_all.shape[0] // GROUP
    outs = []
    for i in range(GROUP):
        res = res_all[i * rb:(i + 1) * rb, :]
        den = res[:, LANES:] + jnp.exp(sinks[i] - maxes[i])
        outs.append(res[:, :LANES] * (1.0 / den))
    low = _low_half(outs[0].shape)
    return [jnp.where(low, outs[2 * i], outs[2 * i + 1]) for i in range(GROUP // 2)]


def _ada_kernel(c_ref, w_ref, b_ref, o_ref):
    c = c_ref[...]
    o_ref[0] = _dot(_silu(c).astype(BF16), w_ref[0].astype(BF16)) + b_ref[0]


def _ada_call(c, w, b):
    n_layers, d, n = w.shape
    r = c.shape[0]
    return pl.pallas_call(
        _ada_kernel,
        grid=(n_layers, n // ADA_TILE_N),
        in_specs=[pl.BlockSpec((r, d), lambda l, j: (0, 0)),
                  pl.BlockSpec((1, d, ADA_TILE_N), lambda l, j: (l, 0, j)),
                  pl.BlockSpec((1, 1, ADA_TILE_N), lambda l, j: (l, 0, j))],
        out_specs=pl.BlockSpec((1, r, ADA_TILE_N), lambda l, j: (l, 0, j)),
        out_shape=jax.ShapeDtypeStruct((n_layers, r, n), F32),
        compiler_params=pltpu.CompilerParams(
            dimension_semantics=("arbitrary", "arbitrary"), vmem_limit_bytes=VMEM_LIMIT),
        name="ada_mod",
    )(c, w, b.reshape(n_layers, 1, n))


def _prompt_l0_kernel(x_ref, mod_ref, kvmod_ref, rope_ref, ng_ref, w_in_ref, cw_ref, cb_ref, gw_ref,
                      gb_ref, lam_ref, w_out_ref, fin_ref, fout_ref, kvg_ref, wkv_ref,
                      x2_ref, k_ref, v_ref, conv_ref, hl_ref,
                      xr_s, yg_s, act_s, hist_s, hc_s, z_s, xk_s, x2k_s, scan_s, *, nt, n_tiles):
    tm, d = x_ref.shape
    s = pl.program_id(0)
    n_chunks = fout_ref.shape[0] // MXU_COLS
    b_cur = jnp.minimum(s, n_tiles - 1) // nt
    b_prev = jnp.clip(s - 1, 0, n_tiles - 1) // nt
    b_pp = jnp.maximum(s - 2, 0) // nt
    mod = lambda i: _mod_row(mod_ref, b_cur, i)
    modp = lambda i: _mod_row(mod_ref, b_prev, i)

    @pl.when(s == 0)
    def _():
        z_s[...] = jnp.zeros_like(z_s)
        xk_s[...] = jnp.zeros_like(xk_s)
        x2k_s[...] = jnp.zeros_like(x2k_s)

    @pl.when(lax.rem(jnp.minimum(s, n_tiles - 1), nt) == 0)
    def _():
        hist_s[...] = jnp.zeros_like(hist_s)
        hc_s[...] = jnp.zeros_like(hc_s)

    out_prev = _dot(z_s[...], w_out_ref[...])
    x_prev = xk_s[...]
    x2_pp = x2k_s[...]

    hk = _rms_mod(x2_pp, kvg_ref[...], _mod_row(kvmod_ref, b_pp, 1), _mod_row(kvmod_ref, b_pp, 0)).astype(BF16)
    kv = _dot(hk, wkv_ref[...])
    k_ref[...] = _rope(kv[:, :KV_DIM], rope_ref[0], rope_ref[1], rope_ref[2])
    v_ref[...] = kv[:, KV_DIM:]

    x = x_ref[...]
    h = _rms_mod(x, ng_ref[0:1, :], mod(1), mod(0)).astype(BF16)
    _proj_in(h, w_in_ref, xr_s, yg_s)
    xk_s[...] = x

    x1 = x_prev + modp(2) * out_prev
    h2 = _rms_mod(x1, ng_ref[1:2, :], modp(4), modp(3)).astype(BF16)
    _swiglu_in(h2, fin_ref, act_s, range(0, FFN_SPLIT))

    xr = xr_s[...]
    hist = hist_s[...]
    row8 = lax.broadcasted_iota(jnp.int32, (SUBLANES, d), 0)

    def shifted(k):
        rolled = pltpu.roll(xr, k, 0)
        first = jnp.where(row8 >= k, rolled[0:SUBLANES], pltpu.roll(hist, k, 0))
        return jnp.concatenate([first, rolled[SUBLANES:]], axis=0)

    xc = cb_ref[...]
    for j in range(CONV_WIDTH - 1):
        xc = xc + cw_ref[j:j + 1, :] * shifted(CONV_WIDTH - 1 - j)
    xc = xc + cw_ref[CONV_WIDTH - 1:CONV_WIDTH, :] * xr
    hist_s[...] = xr[tm - SUBLANES:]
    xr_s[...] = xc

    ncl = -RG_C * _log_sigmoid(lam_ref[...])
    row8b = lax.broadcasted_iota(jnp.int32, (SUBLANES, RNN_BLOCK), 0)
    chunk = tm // SUBLANES
    pitch = chunk + SUBLANES

    def recurrent_block(n):
        blk = slice(n * RNN_BLOCK, (n + 1) * RNN_BLOCK)
        a_s, u_s, h_s, p_s = (scan_s.at[n % 2, j] for j in range(4))
        xc_blk = xr_s[:, blk]
        gates = _dot(xc_blk.astype(BF16), gw_ref[n]) + gb_ref[n:n + 1, :]
        r = _sigmoid(gates[:, :RNN_BLOCK])
        i = _sigmoid(gates[:, RNN_BLOCK:])
        neg_log_a = ncl[:, blk] * r
        a_all = jnp.exp2(neg_log_a * (-LOG2_E))
        u_all = jnp.sqrt(jnp.tanh(neg_log_a) * (a_all * a_all + 1.0)) * (i * xc_blk)
        for c in range(SUBLANES):
            a_s[c * pitch:c * pitch + chunk, :] = a_all[c * chunk:(c + 1) * chunk, :]
            u_s[c * pitch:c * pitch + chunk, :] = u_all[c * chunk:(c + 1) * chunk, :]
        h = jnp.zeros((SUBLANES, RNN_BLOCK), F32)
        p = jnp.ones((SUBLANES, RNN_BLOCK), F32)
        for g in range(chunk):
            step_rows = pl.ds(g, SUBLANES, stride=pitch)
            a = a_s[step_rows, :]
            h = a * h + u_s[step_rows, :]
            p = a * p
            h_s[step_rows, :] = h
            p_s[step_rows, :] = p
        for step in (1, 2, 4):
            keep = row8b >= step
            h = p * jnp.where(keep, pltpu.roll(h, step, 0), 0.0) + h
            p = p * jnp.where(keep, pltpu.roll(p, step, 0), 1.0)
        hprev = hc_s[:, blk]
        ends = h + p * hprev
        incoming = jnp.where(row8b >= 1, pltpu.roll(ends, 1, 0), hprev)
        hc_s[:, blk] = jnp.broadcast_to(ends[SUBLANES - 1:SUBLANES, :], ends.shape)
        for c in range(SUBLANES):
            rows = slice(c * chunk, (c + 1) * chunk)
            local = slice(c * pitch, c * pitch + chunk)
            hs = h_s[local, :] + p_s[local, :] * incoming[c:c + 1, :]
            z_s[rows, blk] = (hs * _gelu_tanh(yg_s[rows, blk])).astype(BF16)

    blocks = list(range(N_RNN_BLOCKS))
    for c in range(FFN_SPLIT, n_chunks):
        _swiglu_in(h2, fin_ref, act_s, [c])
        if blocks:
            recurrent_block(blocks.pop(0))
    ffn = _dot(act_s[...], fout_ref[...])
    for n in blocks:
        recurrent_block(n)

    x2 = jnp.where(s <= n_tiles, x1 + modp(5) * ffn, x2_pp)
    x2_ref[...] = x2
    x2k_s[...] = x2

    @pl.when(s < n_tiles)
    def _():
        conv_ref[...] = hist_s[...]
        hl_ref[...] = hc_s[...]


def _resident(arr, lead=None):
    if lead is None:
        index, shape = (0,) * arr.ndim, arr.shape
    else:
        index, shape = (lead,) + (0,) * (arr.ndim - 1), (None,) + arr.shape[1:]
    return pl.BlockSpec(shape, lambda *_: index, pipeline_mode=pl.Buffered(1))


def _prompt_mod_specs(ada, kv_ada, layer, row_block):
    return [_resident_rows(ada, layer, row_block), _resident_rows(kv_ada, 0, row_block)]


def _resident_rows(arr, lead, row_block):
    index = (lead, row_block, 0)
    return pl.BlockSpec((None, SUBLANES, arr.shape[2]), lambda *_: index, pipeline_mode=pl.Buffered(1))


def _mod_row(ref, b, i):
    return ref[pl.ds(b, 1), i * D_MODEL:(i + 1) * D_MODEL]


def _prompt_l0_call(x, ada, kv_ada, mod_row_block, rope, consts):
    nb, t, d = x.shape
    tm = PROMPT_TILE
    nt = t // tm
    n_tiles = nb * nt
    d_ff = consts[9][0].shape[1]
    cur = lambda s: jnp.minimum(s, n_tiles - 1)
    prv = lambda s: jnp.clip(s - 1, 0, n_tiles - 1)
    pp = lambda s: jnp.maximum(s - 2, 0)
    tile_spec = lambda w, f: pl.BlockSpec((None, tm, w), lambda s: (f(s) // nt, lax.rem(f(s), nt), 0))
    state_spec = pl.BlockSpec((None, SUBLANES, d), lambda s: (cur(s) // nt, 0, 0))
    return pl.pallas_call(
        functools.partial(_prompt_l0_kernel, nt=nt, n_tiles=n_tiles),
        grid=(n_tiles + 2,),
        in_specs=[tile_spec(d, cur)] + _prompt_mod_specs(ada, kv_ada, 0, mod_row_block)
                 + [pl.BlockSpec((3, tm, LANES), lambda s: (0, lax.rem(pp(s), nt), 0))]
                 + [_resident(*c) for c in consts],
        out_specs=[tile_spec(d, prv), tile_spec(KV_DIM, pp), tile_spec(KV_DIM, pp),
                   state_spec, state_spec],
        out_shape=[jax.ShapeDtypeStruct((nb, t, d), F32),
                   jax.ShapeDtypeStruct((nb, t, KV_DIM), F32),
                   jax.ShapeDtypeStruct((nb, t, KV_DIM), F32),
                   jax.ShapeDtypeStruct((nb, SUBLANES, d), F32),
                   jax.ShapeDtypeStruct((nb, SUBLANES, d), F32)],
        scratch_shapes=[pltpu.VMEM((tm, d), F32), pltpu.VMEM((tm, d), F32),
                        pltpu.VMEM((tm, d_ff), BF16),
                        pltpu.VMEM((SUBLANES, d), F32), pltpu.VMEM((SUBLANES, d), F32),
                        pltpu.VMEM((tm, d), BF16), pltpu.VMEM((tm, d), F32), pltpu.VMEM((tm, d), F32),
                        pltpu.VMEM((2, 4, tm + SUBLANES * SUBLANES, RNN_BLOCK), F32)],
        compiler_params=pltpu.CompilerParams(
            dimension_semantics=("arbitrary",), vmem_limit_bytes=VMEM_LIMIT),
        name="prompt_layer0",
    )(x, ada, kv_ada, rope, *[c[0] for c in consts])


def _prompt_l1_kernel(sink_ref, x_ref, k_ref, v_ref, mod_ref, rope_ref, ng_ref, wq_ref, wo_ref,
                      fin_ref, fout_ref, fg_ref,
                      y_ref,
                      kw_s, vw_s, attn_s, act_s, xk_s, *, nt, n_tiles):
    tm, d = x_ref.shape
    s = pl.program_id(0)
    n_chunks = fout_ref.shape[0] // MXU_COLS
    cur_tile = jnp.minimum(s, n_tiles - 1)
    t = lax.rem(cur_tile, nt)
    b_cur = cur_tile // nt
    b_prev = jnp.maximum(s - 1, 0) // nt
    mod = lambda i: _mod_row(mod_ref, b_cur, i)
    modp = lambda i: _mod_row(mod_ref, b_prev, i)

    @pl.when(s == 0)
    def _():
        attn_s[...] = jnp.zeros_like(attn_s)
        xk_s[...] = jnp.zeros_like(xk_s)

    @pl.when(t == 0)
    def _():
        kw_s[0:WINDOW, :] = jnp.zeros((WINDOW, kw_s.shape[1]), BF16)
        vw_s[0:WINDOW, :] = jnp.zeros((WINDOW, vw_s.shape[1]), BF16)

    @pl.when(t > 0)
    def _():
        kw_s[0:WINDOW, :] = kw_s[tm:tm + WINDOW, :]
        vw_s[0:WINDOW, :] = vw_s[tm:tm + WINDOW, :]

    out_prev = _dot(attn_s[...], wo_ref[...])
    x_prev = xk_s[...]

    x = x_ref[...]
    h = _rms_mod(x, ng_ref[0:1, :], mod(1), mod(0)).astype(BF16)
    q = _rope(_dot(h, wq_ref[...]), rope_ref[0], rope_ref[1], rope_ref[2]) * (HEAD_DIM ** -0.5)
    q_split = [_split_halves(q[:, p * LANES:(p + 1) * LANES]) for p in range(d // LANES)]
    xk_s[...] = x

    new_rows = slice(WINDOW, WINDOW + tm)
    k, v = k_ref[...], v_ref[...]
    ones = jnp.ones((tm, LANES), BF16)
    for pb in range(KV_DIM // LANES):
        k_dup = _dup_halves(k[:, pb * LANES:(pb + 1) * LANES])
        v_dup = _dup_halves(v[:, pb * LANES:(pb + 1) * LANES])
        for i in range(2):
            g = 2 * pb + i
            kw_s[new_rows, g * LANES:(g + 1) * LANES] = k_dup[i].astype(BF16)
            vw_s[new_rows, 2 * g * LANES:(2 * g + 1) * LANES] = v_dup[i].astype(BF16)
            vw_s[new_rows, (2 * g + 1) * LANES:(2 * g + 2) * LANES] = ones

    x1 = x_prev + modp(2) * out_prev
    h2 = _rms_mod(x1, ng_ref[1:2, :], modp(4), modp(3)).astype(BF16)
    _swiglu_in(h2, fin_ref, act_s, range(0, L1_FFN_SPLITS[0]))

    span = 2 * WINDOW
    qi = lax.broadcasted_iota(jnp.int32, (WINDOW, span), 0)
    si = lax.broadcasted_iota(jnp.int32, (WINDOW, span), 1)
    band = (si >= qi) & (si <= qi + WINDOW)
    masks = [band & (si >= WINDOW - (t * tm + j * WINDOW)) for j in range(tm // WINDOW)]
    units = [(j, g) for j in range(tm // WINDOW) for g in range(N_KV_HEADS)]
    rows = lambda j: slice(j * WINDOW, (j + 1) * WINDOW)
    win = lambda j: slice(j * WINDOW, j * WINDOW + span)
    sinks = [[sink_ref[0, g * GROUP + i] for i in range(GROUP)] for g in range(N_KV_HEADS)]
    scores = [_attn_scores([q_split[2 * g + i // 2][i % 2][rows(j), :] for i in range(GROUP)],
                           kw_s[win(j), g * LANES:(g + 1) * LANES]) for j, g in units]
    _swiglu_in(h2, fin_ref, act_s, range(L1_FFN_SPLITS[0], L1_FFN_SPLITS[1]))
    probs = [_attn_probs(s_all, masks[j], sinks[g]) for s_all, (j, g) in zip(scores, units)]
    values = [_attn_values(p_all, vw_s[win(j), 2 * g * LANES:(2 * g + 2) * LANES])
              for (p_all, _), (j, g) in zip(probs, units)]
    _swiglu_in(h2, fin_ref, act_s, range(L1_FFN_SPLITS[1], n_chunks))
    for res_all, (_, maxes), (j, g) in zip(values, probs, units):
        for i, pair in enumerate(_attn_finish(res_all, maxes, sinks[g])):
            col = (2 * g + i) * LANES
            attn_s[rows(j), col:col + LANES] = pair.astype(BF16)

    x2 = x1 + modp(5) * _dot(act_s[...], fout_ref[...])
    y_ref[...] = _rms(x2) * fg_ref[...]


def _prompt_l1_call(sinks, x, k, v, ada, mod_row_block, rope, consts):
    nb, t, d = x.shape
    tm = PROMPT_TILE
    nt = t // tm
    n_tiles = nb * nt
    d_ff = consts[4][0].shape[1]
    cur = lambda s: jnp.minimum(s, n_tiles - 1)
    prv = lambda s: jnp.maximum(s - 1, 0)
    tile_spec = lambda w, f: pl.BlockSpec((None, tm, w), lambda s: (f(s) // nt, lax.rem(f(s), nt), 0))
    return pl.pallas_call(
        functools.partial(_prompt_l1_kernel, nt=nt, n_tiles=n_tiles),
        grid=(n_tiles + 1,),
        in_specs=[pl.BlockSpec(memory_space=pltpu.SMEM),
                  tile_spec(d, cur), tile_spec(KV_DIM, cur), tile_spec(KV_DIM, cur),
                  _resident_rows(ada, 1, mod_row_block),
                  pl.BlockSpec((3, tm, LANES), lambda s: (0, lax.rem(cur(s), nt), 0))]
                 + [_resident(*c) for c in consts],
        out_specs=tile_spec(d, prv),
        out_shape=jax.ShapeDtypeStruct((nb, t, d), F32),
        scratch_shapes=[pltpu.VMEM((WINDOW + tm, N_KV_HEADS * LANES), BF16),
                        pltpu.VMEM((WINDOW + tm, N_KV_HEADS * 2 * LANES), BF16),
                        pltpu.VMEM((tm, d), BF16),
                        pltpu.VMEM((tm, d_ff), BF16),
                        pltpu.VMEM((tm, d), F32)],
        compiler_params=pltpu.CompilerParams(
            dimension_semantics=("arbitrary",), vmem_limit_bytes=VMEM_LIMIT),
        name="prompt_layer1",
    )(sinks, x, k, v, ada, rope, *[c[0] for c in consts])


def _sample_l0_kernel(x_ref, mod_ref, kvmod_ref, qmod_ref, h0_ref, cst_ref, rope_ref, ng_ref, w_in_ref,
                      cw_ref, cb_ref, gw_ref, gb_ref, lam_ref, w_out_ref, fin_ref, fout_ref, kvg_ref,
                      wkv_ref, qg_ref, wq_ref,
                      x2_ref, k_ref, v_ref, q_ref, conv_ref, hl_ref,
                      xr_s, yg_s, a_s, u_s, o_s, act_s):
    sb, nt, d = x_ref.shape
    rows = nt * sb
    slab = lambda t: slice(t * sb, (t + 1) * sb)
    vec = lambda ref, i: _tile_rows(ref[:, i * d:(i + 1) * d], nt)
    mod = lambda i: vec(mod_ref, i)
    seq_major = lambda a: jnp.swapaxes(a.reshape(nt, sb, a.shape[-1]), 0, 1)

    x = jnp.swapaxes(x_ref[...], 0, 1).reshape(rows, d)
    h = _rms_mod(x, ng_ref[0:1, :], mod(1), mod(0)).astype(BF16)
    _proj_in(h, w_in_ref, xr_s, yg_s)

    def conv_in(j):
        return cst_ref[j] if j < CONV_WIDTH - 1 else xr_s[slab(j - (CONV_WIDTH - 1)), :]

    xc_slabs = []
    for t in range(nt):
        acc = cb_ref[...]
        for j in range(CONV_WIDTH):
            acc = acc + cw_ref[j:j + 1, :] * conv_in(t + j)
        xc_slabs.append(acc)
    xc = jnp.concatenate(xc_slabs, axis=0)
    for j in range(CONV_WIDTH - 1):
        conv_ref[j] = xr_s[slab(nt - (CONV_WIDTH - 1) + j), :]

    _rglru_gates(xc, gw_ref, gb_ref, lam_ref, a_s, u_s)

    hs = h0_ref[...]
    for t in range(nt):
        hs = a_s[slab(t), :] * hs + u_s[slab(t), :]
        o_s[slab(t), :] = hs
    hl_ref[...] = hs

    z = (o_s[...] * _gelu_tanh(yg_s[...])).astype(BF16)
    x1 = x + mod(2) * _dot(z, w_out_ref[...])

    h2 = _rms_mod(x1, ng_ref[1:2, :], mod(4), mod(3)).astype(BF16)
    x2 = x1 + mod(5) * _swiglu(h2, fin_ref, fout_ref, act_s)
    x2_ref[...] = x2.reshape(nt, sb, d)

    hk = _rms_mod(x2, kvg_ref[...], vec(kvmod_ref, 1), vec(kvmod_ref, 0)).astype(BF16)
    kv = _dot(hk, wkv_ref[...])
    hq = _rms_mod(x2, qg_ref[0:1, :], vec(qmod_ref, 1), vec(qmod_ref, 0)).astype(BF16)
    q = _dot(hq, wq_ref[...])
    k_slabs, q_slabs = [], []
    for t in range(nt):
        c, s_next, s_prev = rope_ref[0, t:t + 1, :], rope_ref[1, t:t + 1, :], rope_ref[2, t:t + 1, :]
        k_slabs.append(_rope(kv[slab(t), :KV_DIM], c, s_next, s_prev))
        q_slabs.append(_rope(q[slab(t), :], c, s_next, s_prev) * (HEAD_DIM ** -0.5))
    k_ref[...] = seq_major(jnp.concatenate(k_slabs, axis=0))
    q_ref[...] = seq_major(jnp.concatenate(q_slabs, axis=0))
    v_ref[...] = seq_major(kv[:, KV_DIM:])


def _sample_mod_spec(arr, lead, sb):
    return pl.BlockSpec((None, sb, arr.shape[2]), lambda i: (lead, i, 0))


def _sample_l0_call(x, ada, kv_ada, h0, cst, consts):
    nb, nt, d = x.shape
    sb = SAMPLE_BATCH_TILE
    d_ff = consts[10][0].shape[1]
    rows = nt * sb
    slab_spec = lambda lead, w: pl.BlockSpec((lead, sb, w), lambda i: (0, i, 0))
    seq_spec = lambda w: pl.BlockSpec((sb, nt, w), lambda i: (i, 0, 0))
    return pl.pallas_call(
        _sample_l0_kernel,
        grid=(nb // sb,),
        in_specs=[seq_spec(d), _sample_mod_spec(ada, 0, sb), _sample_mod_spec(kv_ada, 0, sb),
                  _sample_mod_spec(ada, 1, sb),
                  pl.BlockSpec((sb, d), lambda i: (i, 0)), slab_spec(CONV_WIDTH - 1, d)]
                 + [_resident(*c) for c in consts],
        out_specs=[slab_spec(nt, d), seq_spec(KV_DIM), seq_spec(KV_DIM), seq_spec(d),
                   slab_spec(CONV_WIDTH - 1, d), pl.BlockSpec((sb, d), lambda i: (i, 0))],
        out_shape=[jax.ShapeDtypeStruct((nt, nb, d), F32),
                   jax.ShapeDtypeStruct((nb, nt, KV_DIM), F32),
                   jax.ShapeDtypeStruct((nb, nt, KV_DIM), F32),
                   jax.ShapeDtypeStruct((nb, nt, d), F32),
                   jax.ShapeDtypeStruct((CONV_WIDTH - 1, nb, d), F32),
                   jax.ShapeDtypeStruct((nb, d), F32)],
        scratch_shapes=[pltpu.VMEM((rows, d), F32)] * 5 + [pltpu.VMEM((rows, d_ff), BF16)],
        compiler_params=pltpu.CompilerParams(
            dimension_semantics=("arbitrary",), vmem_limit_bytes=VMEM_LIMIT),
        name="sample_layer0",
    )(x, ada, kv_ada, ada, h0, cst, *[c[0] for c in consts])


def _sample_attn_kernel(sink_ref, q_ref, kn_ref, vn_ref, ck_ref, cv_ref,
                        attn_ref, ko_ref, vo_ref):
    sb, nt, d = q_ref.shape
    span = 2 * WINDOW

    tok = lax.broadcasted_iota(jnp.int32, (nt, span), 0)
    si = lax.broadcasted_iota(jnp.int32, (nt, span), 1)
    mask = (si >= tok) & (si <= tok + WINDOW)
    pad = jnp.zeros((HEAD_DIM, LANES - nt), F32)
    ones = jnp.ones((LANES, span), BF16)

    sinks = [[sink_ref[0, g * GROUP + i] for i in range(GROUP)] for g in range(N_KV_HEADS)]

    def windows(b):
        knt, vnt = kn_ref[b].T, vn_ref[b].T
        kwin, vaug = [], []
        for g in range(N_KV_HEADS):
            rows = slice(g * HEAD_DIM, (g + 1) * HEAD_DIM)
            tops = []
            for c_ref, nt_new, o_ref in ((ck_ref, knt[rows, :], ko_ref), (cv_ref, vnt[rows, :], vo_ref)):
                cache = c_ref[b, g]
                o_ref[b, g] = pltpu.roll(cache, WINDOW - nt, 1)
                o_ref[b, g, :, WINDOW - nt:WINDOW] = nt_new
                tops.append(jnp.concatenate([cache, nt_new, pad], axis=1))
            kwin.append(jnp.concatenate([tops[0], tops[0]], axis=0).astype(BF16))
            vaug.append(jnp.concatenate([tops[1].astype(BF16), tops[1].astype(BF16), ones], axis=0))
        return kwin, vaug

    def sequences(it, carry):
        seqs = [it * SAMPLE_ATTN_UNROLL + u for u in range(SAMPLE_ATTN_UNROLL)]
        wins = [windows(b) for b in seqs]
        units = [(u, g) for u in range(SAMPLE_ATTN_UNROLL) for g in range(N_KV_HEADS)]
        q_split = [[_split_halves(q_ref[b][:, p * LANES:(p + 1) * LANES]) for p in range(d // LANES)]
                   for b in seqs]
        scores = [_attn_scores([q_split[u][2 * g + i // 2][i % 2] for i in range(GROUP)], wins[u][0][g], True)
                  for u, g in units]
        probs = [_attn_probs(s_all, mask, sinks[g]) for s_all, (u, g) in zip(scores, units)]
        values = [_attn_values(p_all, wins[u][1][g], True) for (p_all, _), (u, g) in zip(probs, units)]
        pairs = [[] for _ in seqs]
        for res_all, (_, maxes), (u, g) in zip(values, probs, units):
            pairs[u] += _attn_finish(res_all, maxes, sinks[g])
        for u, b in enumerate(seqs):
            attn_ref[b] = jnp.concatenate(pairs[u], axis=1)
        return carry

    lax.fori_loop(0, sb // SAMPLE_ATTN_UNROLL, sequences, 0)


def _sample_attn_call(sinks, q, kn, vn, ck, cv):
    nb, nt, d = q.shape
    sb = SAMPLE_ATTN_BATCH
    seq_spec = lambda r, w: pl.BlockSpec((sb, r, w), lambda i: (i, 0, 0))
    cache_spec = pl.BlockSpec((sb,) + ck.shape[1:], lambda i: (i, 0, 0, 0))
    return pl.pallas_call(
        _sample_attn_kernel,
        grid=(nb // sb,),
        in_specs=[pl.BlockSpec(memory_space=pltpu.SMEM),
                  seq_spec(nt, d), seq_spec(nt, KV_DIM), seq_spec(nt, KV_DIM), cache_spec, cache_spec],
        out_specs=[seq_spec(nt, d), cache_spec, cache_spec],
        out_shape=[jax.ShapeDtypeStruct((nb, nt, d), F32),
                   jax.ShapeDtypeStruct(ck.shape, F32),
                   jax.ShapeDtypeStruct(cv.shape, F32)],
        compiler_params=pltpu.CompilerParams(
            dimension_semantics=("arbitrary",), vmem_limit_bytes=VMEM_LIMIT),
        name="sample_attention",
    )(sinks, q, kn, vn, ck, cv)


def _sample_l1_kernel(x_ref, attn_ref, mod_ref, ng_ref, wo_ref, fin_ref, fout_ref, fg_ref,
                      y_ref, act_s):
    nt, sb, d = x_ref.shape
    rows = nt * sb
    mod = lambda i: _tile_rows(mod_ref[:, i * d:(i + 1) * d], nt)
    x = x_ref[...].reshape(rows, d)
    attn = jnp.swapaxes(attn_ref[...], 0, 1).reshape(rows, d).astype(BF16)
    x1 = x + mod(2) * _dot(attn, wo_ref[...])
    h2 = _rms_mod(x1, ng_ref[1:2, :], mod(4), mod(3)).astype(BF16)
    x2 = x1 + mod(5) * _swiglu(h2, fin_ref, fout_ref, act_s)
    y_ref[...] = jnp.swapaxes((_rms(x2) * fg_ref[...]).reshape(nt, sb, d), 0, 1)


def _sample_l1_call(x, attn, ada, consts):
    nt, nb, d = x.shape
    sb = SAMPLE_BATCH_TILE
    d_ff = consts[3][0].shape[1]
    slab_spec = pl.BlockSpec((nt, sb, d), lambda i: (0, i, 0))
    seq_spec = pl.BlockSpec((sb, nt, d), lambda i: (i, 0, 0))
    return pl.pallas_call(
        _sample_l1_kernel,
        grid=(nb // sb,),
        in_specs=[slab_spec, seq_spec, _sample_mod_spec(ada, 1, sb)]
                 + [_resident(*c) for c in consts],
        out_specs=seq_spec,
        out_shape=jax.ShapeDtypeStruct((nb, nt, d), F32),
        scratch_shapes=[pltpu.VMEM((nt * sb, d_ff), BF16)],
        compiler_params=pltpu.CompilerParams(
            dimension_semantics=("arbitrary",), vmem_limit_bytes=VMEM_LIMIT),
        name="sample_layer1",
    )(x, attn, ada, *[c[0] for c in consts])


def _rope_tables(pos):
    half = ROT_DIM // 2
    inv = ROPE_THETA ** (-jnp.arange(0, ROT_DIM, 2, dtype=F32) / ROT_DIM)
    ang = pos.astype(F32)[:, None] * inv[None, :]
    cos, sin = jnp.cos(ang), jnp.sin(ang)
    n = pos.shape[0]
    rest = HEAD_DIM - ROT_DIM
    c = jnp.concatenate([cos, cos, jnp.ones((n, rest), F32)], axis=1)
    s_next = jnp.concatenate([-sin, jnp.zeros((n, half + rest), F32)], axis=1)
    s_prev = jnp.concatenate([jnp.zeros((n, half), F32), sin, jnp.zeros((n, rest), F32)], axis=1)
    reps = LANES // HEAD_DIM
    return jnp.stack([jnp.tile(c, (1, reps)), jnp.tile(s_next, (1, reps)), jnp.tile(s_prev, (1, reps))])


def kernel(x_prompt, x_sample, c_prompt, c_sample, state_conv, state_h, cache_k, cache_v, ada_w, ada_b, norm_g, rnn_w_in, rnn_conv_w, rnn_conv_b, rnn_gate_w, rnn_gate_b, rnn_lambda, rnn_w_out, kv_ada_w, kv_ada_b, kv_norm_g, w_kv, attn_w_q, attn_sinks, attn_w_o, ffn_w_in, ffn_w_out, final_g):
    nb_p, t_p, d = x_prompt.shape
    nb_s, t_s, _ = x_sample.shape

    assert rnn_w_in.shape[0] == 1 and attn_w_q.shape[0] == 1 and nb_s % SUBLANES == 0

    c_all = jnp.concatenate([c_sample, c_prompt], axis=0)
    ada = _ada_call(c_all, ada_w, ada_b)
    kv_ada = _ada_call(c_all, kv_ada_w[None], kv_ada_b[None])
    prompt_row_block = nb_s // SUBLANES

    bf = lambda w: w.astype(BF16)
    row = lambda v: v.reshape(1, -1)
    ffn_in, ffn_out = bf(ffn_w_in), bf(ffn_w_out)
    wq, wo = (bf(attn_w_q), 0), (bf(attn_w_o), 0)
    l0_consts = [(norm_g, 0), (bf(rnn_w_in), 0), (rnn_conv_w, 0), (rnn_conv_b, None), (bf(rnn_gate_w), 0),
                 (rnn_gate_b, 0), (rnn_lambda, None), (bf(rnn_w_out), 0), (ffn_in, 0), (ffn_out, 0),
                 (row(kv_norm_g), None), (bf(w_kv), None)]
    l1_ffn = [(ffn_in, 1), (ffn_out, 1), (row(final_g), None)]

    rope_p = _rope_tables(jnp.arange(t_p, dtype=jnp.int32))
    x2_p, k_p, v_p, conv_p, hl_p = _prompt_l0_call(x_prompt, ada, kv_ada, prompt_row_block, rope_p, l0_consts)
    y_prompt = _prompt_l1_call(attn_sinks, x2_p, k_p, v_p, ada, prompt_row_block, rope_p,
                               [(norm_g, 1), wq, wo] + l1_ffn)

    rope_s = _rope_tables(PAST_LEN + jnp.arange(t_s, dtype=jnp.int32))
    x2_s, k_s, v_s, q_s, conv_s, hl_s = _sample_l0_call(
        x_sample, ada, kv_ada, state_h[0], state_conv[0].transpose(1, 0, 2),
        [(rope_s, None)] + l0_consts + [(norm_g, 1), wq])
    attn_s, ko_s, vo_s = _sample_attn_call(
        attn_sinks, q_s, k_s, v_s, cache_k.transpose(0, 2, 3, 1), cache_v.transpose(0, 2, 3, 1))
    y_s = _sample_l1_call(x2_s, attn_s, ada, [(norm_g, 1), wo] + l1_ffn)

    kv_shape = (WINDOW, N_KV_HEADS, HEAD_DIM)
    return (y_prompt,
            y_s,
            conv_p[None, :, SUBLANES - (CONV_WIDTH - 1):, :],
            hl_p[None, :, 0, :],
            k_p[:, t_p - WINDOW:, :].reshape((nb_p,) + kv_shape),
            v_p[:, t_p - WINDOW:, :].reshape((nb_p,) + kv_shape),
            conv_s.transpose(1, 0, 2)[None],
            hl_s[None],
            ko_s.transpose(0, 3, 1, 2),
            vo_s.transpose(0, 3, 1, 2))
```

```python
import functools

import jax
import jax.numpy as jnp
from jax import lax
from jax.experimental import pallas as pl
from jax.experimental.pallas import tpu as pltpu

F32 = jnp.float32
BF16 = jnp.bfloat16

D_MODEL = 1024
N_RNN_BLOCKS = 8
RNN_BLOCK = D_MODEL // N_RNN_BLOCKS
CONV_WIDTH = 4
RG_C = 8.0
HEAD_DIM = 64
N_HEADS = D_MODEL // HEAD_DIM
N_KV_HEADS = 4
GROUP = N_HEADS // N_KV_HEADS
KV_DIM = N_KV_HEADS * HEAD_DIM
WINDOW = 128
ROT_DIM = HEAD_DIM // 4
ROPE_THETA = 500000.0
EPS = 1e-6
NEG_INF = -1e30
LOG2_E = 1.4426950408889634
GELU_C = 0.7978845608028654
GELU_K = 0.044715
PAST_LEN = 16384

LANES = 128
SUBLANES = 8
MXU_COLS = 256
VMEM_LIMIT = 56 * 1024 * 1024

PROMPT_TILE = 256
FFN_SPLIT = 3
L1_FFN_SPLITS = (3, 7)
ADA_TILE_N = 2048
SAMPLE_ATTN_BATCH = 16
SAMPLE_ATTN_UNROLL = 4
SAMPLE_BATCH_TILE = 32


def _dot(a, b):
    return jnp.dot(a, b, preferred_element_type=F32)


def _dot_nt(a, b):
    return lax.dot_general(a, b, (((1,), (1,)), ((), ())), preferred_element_type=F32)


def _sigmoid(x):
    return 1.0 / (1.0 + jnp.exp2(x * (-LOG2_E)))


def _silu(x):
    return x * _sigmoid(x)


def _gelu_tanh(x):
    half = 0.5 * x
    return half + half * jnp.tanh(x * (GELU_C + (GELU_C * GELU_K) * (x * x)))


def _log_sigmoid(x):
    return -(jnp.maximum(-x, 0.0) + jnp.log1p(jnp.exp(-jnp.abs(x))))


def _rms(x):
    return x * lax.rsqrt(jnp.mean(x * x, axis=-1, keepdims=True) + EPS)


def _rms_mod(x, g, scale, shift):
    return _rms(x) * (g * (1.0 + scale)) + shift


def _tile_rows(m, reps):
    return jnp.concatenate([m] * reps, axis=0)


def _rope_block(blk, c, s_next, s_prev):
    return blk * c + pltpu.roll(blk, LANES - ROT_DIM // 2, 1) * s_next + pltpu.roll(blk, ROT_DIM // 2, 1) * s_prev


def _rope(x, c, s_next, s_prev):
    blocks = [_rope_block(x[:, j * LANES:(j + 1) * LANES], c, s_next, s_prev)
              for j in range(x.shape[1] // LANES)]
    return jnp.concatenate(blocks, axis=1)


def _proj_in(h, w_in_ref, xr_s, yg_s):
    d = xr_s.shape[1]
    cw = 2 * MXU_COLS
    for c in range(d // cw):
        xr_s[:, c * cw:(c + 1) * cw] = _dot(h, w_in_ref[:, c * cw:(c + 1) * cw])
        yg_s[:, c * cw:(c + 1) * cw] = _dot(h, w_in_ref[:, d + c * cw:d + (c + 1) * cw])


def _rglru_gates(xc, gw_ref, gb_ref, lam_ref, a_s, u_s):
    xcb = xc.astype(BF16)
    cl = RG_C * _log_sigmoid(lam_ref[...])
    for n in range(N_RNN_BLOCKS):
        blk = slice(n * RNN_BLOCK, (n + 1) * RNN_BLOCK)
        g = _dot(xcb[:, blk], gw_ref[n]) + gb_ref[n:n + 1, :]
        r = _sigmoid(g[:, :RNN_BLOCK])
        i = _sigmoid(g[:, RNN_BLOCK:])
        log_a = cl[:, blk] * r
        a = jnp.exp(log_a)
        a_s[:, blk] = a
        u_s[:, blk] = jnp.sqrt(-jnp.tanh(log_a) * (a * a + 1.0)) * (i * xc[:, blk])


def _swiglu_in(h, fin_ref, act_s, chunks):
    d_ff = act_s.shape[1]
    for c in chunks:
        cols = slice(c * MXU_COLS, (c + 1) * MXU_COLS)
        gate = _dot(h, fin_ref[:, cols])
        up = _dot(h, fin_ref[:, d_ff + c * MXU_COLS:d_ff + (c + 1) * MXU_COLS])
        act_s[:, cols] = (_silu(gate) * up).astype(BF16)


def _swiglu(h, fin_ref, fout_ref, act_s):
    _swiglu_in(h, fin_ref, act_s, range(fout_ref.shape[0] // MXU_COLS))
    return _dot(act_s[...], fout_ref[...])


def _low_half(shape):
    return lax.broadcasted_iota(jnp.int32, shape, 1) < LANES // 2


def _dup_halves(blk):
    low = _low_half(blk.shape)
    rot = pltpu.roll(blk, LANES // 2, 1)
    return jnp.where(low, blk, rot), jnp.where(low, rot, blk)


def _split_halves(blk):
    low = _low_half(blk.shape)
    zero = jnp.zeros_like(blk)
    return jnp.where(low, blk, zero), jnp.where(low, zero, blk)


def _attn_scores(q_heads, kwin, keys_on_lanes=False):
    q_all = jnp.concatenate(q_heads, axis=0).astype(BF16)
    return _dot(q_all, kwin) if keys_on_lanes else _dot_nt(q_all, kwin)


def _attn_probs(s_all, mask, sinks):
    rb = s_all.shape[0] // GROUP
    probs, maxes = [], []
    for i in range(GROUP):
        s = jnp.where(mask, s_all[i * rb:(i + 1) * rb, :], NEG_INF)
        mx = jnp.maximum(jnp.max(s, axis=-1, keepdims=True), sinks[i])
        probs.append(jnp.exp(s - mx))
        maxes.append(mx)
    return jnp.concatenate(probs, axis=0).astype(BF16), maxes


def _attn_values(p_all, vaug, keys_on_lanes=False):
    return _dot_nt(p_all, vaug) if keys_on_lanes else _dot(p_all, vaug)


def _attn_finish(res_all, maxes, sinks):
    rb = res_all.shape[0] // GROUP
    outs = []
    for i in range(GROUP):
        res = res_all[i * rb:(i + 1) * rb, :]
        den = res[:, LANES:] + jnp.exp(sinks[i] - maxes[i])
        outs.append(res[:, :LANES] * (1.0 / den))
    low = _low_half(outs[0].shape)
    return [jnp.where(low, outs[2 * i], outs[2 * i + 1]) for i in range(GROUP // 2)]


def _ada_kernel(c_ref, w_ref, b_ref, o_ref):
    c = c_ref[...]
    o_ref[0] = _dot(_silu(c).astype(BF16), w_ref[0].astype(BF16)) + b_ref[0]


def _ada_call(c, w, b):
    n_layers, d, n = w.shape
    r = c.shape[0]
    return pl.pallas_call(
        _ada_kernel,
        grid=(n_layers, n // ADA_TILE_N),
        in_specs=[pl.BlockSpec((r, d), lambda l, j: (0, 0)),
                  pl.BlockSpec((1, d, ADA_TILE_N), lambda l, j: (l, 0, j)),
                  pl.BlockSpec((1, 1, ADA_TILE_N), lambda l, j: (l, 0, j))],
        out_specs=pl.BlockSpec((1, r, ADA_TILE_N), lambda l, j: (l, 0, j)),
        out_shape=jax.ShapeDtypeStruct((n_layers, r, n), F32),
        compiler_params=pltpu.CompilerParams(
            dimension_semantics=("arbitrary", "arbitrary"), vmem_limit_bytes=VMEM_LIMIT),
        name="ada_mod",
    )(c, w, b.reshape(n_layers, 1, n))


def _prompt_l0_kernel(x_ref, mod_ref, kvmod_ref, rope_ref, ng_ref, w_in_ref, cw_ref, cb_ref, gw_ref,
                      gb_ref, lam_ref, w_out_ref, fin_ref, fout_ref, kvg_ref, wkv_ref,
                      x2_ref, k_ref, v_ref, conv_ref, hl_ref,
                      xr_s, yg_s, act_s, hist_s, hc_s, z_s, xk_s, x2k_s, scan_s, *, nt, n_tiles):
    tm, d = x_ref.shape
    s = pl.program_id(0)
    n_chunks = fout_ref.shape[0] // MXU_COLS
    b_cur = jnp.minimum(s, n_tiles - 1) // nt
    b_prev = jnp.clip(s - 1, 0, n_tiles - 1) // nt
    b_pp = jnp.maximum(s - 2, 0) // nt
    mod = lambda i: _mod_row(mod_ref, b_cur, i)
    modp = lambda i: _mod_row(mod_ref, b_prev, i)

    @pl.when(s == 0)
    def _():
        z_s[...] = jnp.zeros_like(z_s)
        xk_s[...] = jnp.zeros_like(xk_s)
        x2k_s[...] = jnp.zeros_like(x2k_s)

    @pl.when(lax.rem(jnp.minimum(s, n_tiles - 1), nt) == 0)
    def _():
        hist_s[...] = jnp.zeros_like(hist_s)
        hc_s[...] = jnp.zeros_like(hc_s)

    out_prev = _dot(z_s[...], w_out_ref[...])
    x_prev = xk_s[...]
    x2_pp = x2k_s[...]

    hk = _rms_mod(x2_pp, kvg_ref[...], _mod_row(kvmod_ref, b_pp, 1), _mod_row(kvmod_ref, b_pp, 0)).astype(BF16)
    kv = _dot(hk, wkv_ref[...])
    k_ref[...] = _rope(kv[:, :KV_DIM], rope_ref[0], rope_ref[1], rope_ref[2])
    v_ref[...] = kv[:, KV_DIM:]

    x = x_ref[...]
    h = _rms_mod(x, ng_ref[0:1, :], mod(1), mod(0)).astype(BF16)
    _proj_in(h, w_in_ref, xr_s, yg_s)
    xk_s[...] = x

    x1 = x_prev + modp(2) * out_prev
    h2 = _rms_mod(x1, ng_ref[1:2, :], modp(4), modp(3)).astype(BF16)
    _swiglu_in(h2, fin_ref, act_s, range(0, FFN_SPLIT))

    xr = xr_s[...]
    hist = hist_s[...]
    row8 = lax.broadcasted_iota(jnp.int32, (SUBLANES, d), 0)

    def shifted(k):
        rolled = pltpu.roll(xr, k, 0)
        first = jnp.where(row8 >= k, rolled[0:SUBLANES], pltpu.roll(hist, k, 0))
        return jnp.concatenate([first, rolled[SUBLANES:]], axis=0)

    xc = cb_ref[...]
    for j in range(CONV_WIDTH - 1):
        xc = xc + cw_ref[j:j + 1, :] * shifted(CONV_WIDTH - 1 - j)
    xc = xc + cw_ref[CONV_WIDTH - 1:CONV_WIDTH, :] * xr
    hist_s[...] = xr[tm - SUBLANES:]
    xr_s[...] = xc

    ncl = -RG_C * _log_sigmoid(lam_ref[...])
    row8b = lax.broadcasted_iota(jnp.int32, (SUBLANES, RNN_BLOCK), 0)
    chunk = tm // SUBLANES
    pitch = chunk + 4

    def recurrent_block(n):
        blk = slice(n * RNN_BLOCK, (n + 1) * RNN_BLOCK)
        a_s, u_s, h_s, p_s = (scan_s.at[n % 2, j] for j in range(4))
        xc_blk = xr_s[:, blk]
        gates = _dot(xc_blk.astype(BF16), gw_ref[n]) + gb_ref[n:n + 1, :]
        r = _sigmoid(gates[:, :RNN_BLOCK])
        i = _sigmoid(gates[:, RNN_BLOCK:])
        neg_log_a = ncl[:, blk] * r
        a_all = jnp.exp2(neg_log_a * (-LOG2_E))
        w = jnp.tanh(neg_log_a) * (a_all * a_all + 1.0)
        u_all = jnp.where(w == 0.0, 0.0, w * lax.rsqrt(w)) * (i * xc_blk)
        for c in range(SUBLANES):
            a_s[c * pitch:c * pitch + chunk, :] = a_all[c * chunk:(c + 1) * chunk, :]
            u_s[c * pitch:c * pitch + chunk, :] = u_all[c * chunk:(c + 1) * chunk, :]
        h = jnp.zeros((SUBLANES, RNN_BLOCK), F32)
        p = jnp.ones((SUBLANES, RNN_BLOCK), F32)
        for g in range(chunk):
            step_rows = pl.ds(g, SUBLANES, stride=pitch)
            a = a_s[step_rows, :]
            h = a * h + u_s[step_rows, :]
            p = a * p
            h_s[step_rows, :] = h
            p_s[step_rows, :] = p
        for step in (1, 2, 4):
            keep = row8b >= step
            h = p * jnp.where(keep, pltpu.roll(h, step, 0), 0.0) + h
            p = p * jnp.where(keep, pltpu.roll(p, step, 0), 1.0)
        hprev = hc_s[:, blk]
        ends = h + p * hprev
        incoming = jnp.where(row8b >= 1, pltpu.roll(ends, 1, 0), hprev)
        hc_s[:, blk] = jnp.broadcast_to(ends[SUBLANES - 1:SUBLANES, :], ends.shape)
        for c in range(SUBLANES):
            rows = slice(c * chunk, (c + 1) * chunk)
            local = slice(c * pitch, c * pitch + chunk)
            hs = h_s[local, :] + p_s[local, :] * incoming[c:c + 1, :]
            z_s[rows, blk] = (hs * _gelu_tanh(yg_s[rows, blk])).astype(BF16)

    blocks = list(range(N_RNN_BLOCKS))
    for c in range(FFN_SPLIT, n_chunks):
        _swiglu_in(h2, fin_ref, act_s, [c])
        if blocks:
            recurrent_block(blocks.pop(0))
    ffn = _dot(act_s[...], fout_ref[...])
    for n in blocks:
        recurrent_block(n)

    x2 = jnp.where(s <= n_tiles, x1 + modp(5) * ffn, x2_pp)
    x2_ref[...] = x2
    x2k_s[...] = x2

    @pl.when(s < n_tiles)
    def _():
        conv_ref[...] = hist_s[...]
        hl_ref[...] = hc_s[...]


def _resident(arr, lead=None):
    if lead is None:
        index, shape = (0,) * arr.ndim, arr.shape
    else:
        index, shape = (lead,) + (0,) * (arr.ndim - 1), (None,) + arr.shape[1:]
    return pl.BlockSpec(shape, lambda *_: index, pipeline_mode=pl.Buffered(1))


def _prompt_mod_specs(ada, kv_ada, layer, row_block):
    return [_resident_rows(ada, layer, row_block), _resident_rows(kv_ada, 0, row_block)]


def _resident_rows(arr, lead, row_block):
    index = (lead, row_block, 0)
    return pl.BlockSpec((None, SUBLANES, arr.shape[2]), lambda *_: index, pipeline_mode=pl.Buffered(1))


def _mod_row(ref, b, i):
    return ref[pl.ds(b, 1), i * D_MODEL:(i + 1) * D_MODEL]


def _prompt_l0_call(x, ada, kv_ada, mod_row_block, rope, consts):
    nb, t, d = x.shape
    tm = PROMPT_TILE
    nt = t // tm
    n_tiles = nb * nt
    d_ff = consts[9][0].shape[1]
    cur = lambda s: jnp.minimum(s, n_tiles - 1)
    prv = lambda s: jnp.clip(s - 1, 0, n_tiles - 1)
    pp = lambda s: jnp.maximum(s - 2, 0)
    tile_spec = lambda w, f: pl.BlockSpec((None, tm, w), lambda s: (f(s) // nt, lax.rem(f(s), nt), 0))
    state_spec = pl.BlockSpec((None, SUBLANES, d), lambda s: (cur(s) // nt, 0, 0))
    return pl.pallas_call(
        functools.partial(_prompt_l0_kernel, nt=nt, n_tiles=n_tiles),
        grid=(n_tiles + 2,),
        in_specs=[tile_spec(d, cur)] + _prompt_mod_specs(ada, kv_ada, 0, mod_row_block)
                 + [pl.BlockSpec((3, tm, LANES), lambda s: (0, lax.rem(pp(s), nt), 0))]
                 + [_resident(*c) for c in consts],
        out_specs=[tile_spec(d, prv), tile_spec(KV_DIM, pp), tile_spec(KV_DIM, pp),
                   state_spec, state_spec],
        out_shape=[jax.ShapeDtypeStruct((nb, t, d), F32),
                   jax.ShapeDtypeStruct((nb, t, KV_DIM), F32),
                   jax.ShapeDtypeStruct((nb, t, KV_DIM), F32),
                   jax.ShapeDtypeStruct((nb, SUBLANES, d), F32),
                   jax.ShapeDtypeStruct((nb, SUBLANES, d), F32)],
        scratch_shapes=[pltpu.VMEM((tm, d), F32), pltpu.VMEM((tm, d), F32),
                        pltpu.VMEM((tm, d_ff), BF16),
                        pltpu.VMEM((SUBLANES, d), F32), pltpu.VMEM((SUBLANES, d), F32),
                        pltpu.VMEM((tm, d), BF16), pltpu.VMEM((tm, d), F32), pltpu.VMEM((tm, d), F32),
                        pltpu.VMEM((2, 4, tm + SUBLANES * SUBLANES, RNN_BLOCK), F32)],
        compiler_params=pltpu.CompilerParams(
            dimension_semantics=("arbitrary",), vmem_limit_bytes=VMEM_LIMIT),
        name="prompt_layer0",
    )(x, ada, kv_ada, rope, *[c[0] for c in consts])


def _prompt_l1_kernel(sink_ref, x_ref, k_ref, v_ref, mod_ref, rope_ref, ng_ref, wq_ref, wo_ref,
                      fin_ref, fout_ref, fg_ref,
                      y_ref,
                      kw_s, vw_s, attn_s, act_s, xk_s, *, nt, n_tiles):
    tm, d = x_ref.shape
    s = pl.program_id(0)
    n_chunks = fout_ref.shape[0] // MXU_COLS
    cur_tile = jnp.minimum(s, n_tiles - 1)
    t = lax.rem(cur_tile, nt)
    b_cur = cur_tile // nt
    b_prev = jnp.maximum(s - 1, 0) // nt
    mod = lambda i: _mod_row(mod_ref, b_cur, i)
    modp = lambda i: _mod_row(mod_ref, b_prev, i)

    @pl.when(s == 0)
    def _():
        attn_s[...] = jnp.zeros_like(attn_s)
        xk_s[...] = jnp.zeros_like(xk_s)

    @pl.when(t == 0)
    def _():
        kw_s[0:WINDOW, :] = jnp.zeros((WINDOW, kw_s.shape[1]), BF16)
        vw_s[0:WINDOW, :] = jnp.zeros((WINDOW, vw_s.shape[1]), BF16)

    @pl.when(t > 0)
    def _():
        kw_s[0:WINDOW, :] = kw_s[tm:tm + WINDOW, :]
        vw_s[0:WINDOW, :] = vw_s[tm:tm + WINDOW, :]

    out_prev = _dot(attn_s[...], wo_ref[...])
    x_prev = xk_s[...]

    x = x_ref[...]
    h = _rms_mod(x, ng_ref[0:1, :], mod(1), mod(0)).astype(BF16)
    q = _rope(_dot(h, wq_ref[...]), rope_ref[0], rope_ref[1], rope_ref[2]) * (HEAD_DIM ** -0.5)
    q_split = [_split_halves(q[:, p * LANES:(p + 1) * LANES]) for p in range(d // LANES)]
    xk_s[...] = x

    new_rows = slice(WINDOW, WINDOW + tm)
    k, v = k_ref[...], v_ref[...]
    ones = jnp.ones((tm, LANES), BF16)
    for pb in range(KV_DIM // LANES):
        k_dup = _dup_halves(k[:, pb * LANES:(pb + 1) * LANES])
        v_dup = _dup_halves(v[:, pb * LANES:(pb + 1) * LANES])
        for i in range(2):
            g = 2 * pb + i
            kw_s[new_rows, g * LANES:(g + 1) * LANES] = k_dup[i].astype(BF16)
            vw_s[new_rows, 2 * g * LANES:(2 * g + 1) * LANES] = v_dup[i].astype(BF16)
            vw_s[new_rows, (2 * g + 1) * LANES:(2 * g + 2) * LANES] = ones

    x1 = x_prev + modp(2) * out_prev
    h2 = _rms_mod(x1, ng_ref[1:2, :], modp(4), modp(3)).astype(BF16)
    _swiglu_in(h2, fin_ref, act_s, range(0, L1_FFN_SPLITS[0]))

    span = 2 * WINDOW
    qi = lax.broadcasted_iota(jnp.int32, (WINDOW, span), 0)
    si = lax.broadcasted_iota(jnp.int32, (WINDOW, span), 1)
    band = (si >= qi) & (si <= qi + WINDOW)
    masks = [band & (si >= WINDOW - (t * tm + j * WINDOW)) for j in range(tm // WINDOW)]
    units = [(j, g) for j in range(tm // WINDOW) for g in range(N_KV_HEADS)]
    rows = lambda j: slice(j * WINDOW, (j + 1) * WINDOW)
    win = lambda j: slice(j * WINDOW, j * WINDOW + span)
    sinks = [[sink_ref[0, g * GROUP + i] for i in range(GROUP)] for g in range(N_KV_HEADS)]
    scores = [_attn_scores([q_split[2 * g + i // 2][i % 2][rows(j), :] for i in range(GROUP)],
                           kw_s[win(j), g * LANES:(g + 1) * LANES]) for j, g in units]
    _swiglu_in(h2, fin_ref, act_s, range(L1_FFN_SPLITS[0], L1_FFN_SPLITS[1]))
    probs = [_attn_probs(s_all, masks[j], sinks[g]) for s_all, (j, g) in zip(scores, units)]
    values = [_attn_values(p_all, vw_s[win(j), 2 * g * LANES:(2 * g + 2) * LANES])
              for (p_all, _), (j, g) in zip(probs, units)]
    _swiglu_in(h2, fin_ref, act_s, range(L1_FFN_SPLITS[1], n_chunks))
    for res_all, (_, maxes), (j, g) in zip(values, probs, units):
        for i, pair in enumerate(_attn_finish(res_all, maxes, sinks[g])):
            col = (2 * g + i) * LANES
            attn_s[rows(j), col:col + LANES] = pair.astype(BF16)

    x2 = x1 + modp(5) * _dot(act_s[...], fout_ref[...])
    y_ref[...] = _rms(x2) * fg_ref[...]


def _prompt_l1_call(sinks, x, k, v, ada, mod_row_block, rope, consts):
    nb, t, d = x.shape
    tm = PROMPT_TILE
    nt = t // tm
    n_tiles = nb * nt
    d_ff = consts[4][0].shape[1]
    cur = lambda s: jnp.minimum(s, n_tiles - 1)
    prv = lambda s: jnp.maximum(s - 1, 0)
    tile_spec = lambda w, f: pl.BlockSpec((None, tm, w), lambda s: (f(s) // nt, lax.rem(f(s), nt), 0))
    return pl.pallas_call(
        functools.partial(_prompt_l1_kernel, nt=nt, n_tiles=n_tiles),
        grid=(n_tiles + 1,),
        in_specs=[pl.BlockSpec(memory_space=pltpu.SMEM),
                  tile_spec(d, cur), tile_spec(KV_DIM, cur), tile_spec(KV_DIM, cur),
                  _resident_rows(ada, 1, mod_row_block),
                  pl.BlockSpec((3, tm, LANES), lambda s: (0, lax.rem(cur(s), nt), 0))]
                 + [_resident(*c) for c in consts],
        out_specs=tile_spec(d, prv),
        out_shape=jax.ShapeDtypeStruct((nb, t, d), F32),
        scratch_shapes=[pltpu.VMEM((WINDOW + tm, N_KV_HEADS * LANES), BF16),
                        pltpu.VMEM((WINDOW + tm, N_KV_HEADS * 2 * LANES), BF16),
                        pltpu.VMEM((tm, d), BF16),
                        pltpu.VMEM((tm, d_ff), BF16),
                        pltpu.VMEM((tm, d), F32)],
        compiler_params=pltpu.CompilerParams(
            dimension_semantics=("arbitrary",), vmem_limit_bytes=VMEM_LIMIT),
        name="prompt_layer1",
    )(sinks, x, k, v, ada, rope, *[c[0] for c in consts])


def _sample_l0_kernel(x_ref, mod_ref, kvmod_ref, qmod_ref, h0_ref, cst_ref, rope_ref, ng_ref, w_in_ref,
                      cw_ref, cb_ref, gw_ref, gb_ref, lam_ref, w_out_ref, fin_ref, fout_ref, kvg_ref,
                      wkv_ref, qg_ref, wq_ref,
                      x2_ref, k_ref, v_ref, q_ref, conv_ref, hl_ref,
                      xr_s, yg_s, a_s, u_s, o_s, act_s):
    sb, nt, d = x_ref.shape
    rows = nt * sb
    slab = lambda t: slice(t * sb, (t + 1) * sb)
    vec = lambda ref, i: _tile_rows(ref[:, i * d:(i + 1) * d], nt)
    mod = lambda i: vec(mod_ref, i)
    seq_major = lambda a: jnp.swapaxes(a.reshape(nt, sb, a.shape[-1]), 0, 1)

    x = jnp.swapaxes(x_ref[...], 0, 1).reshape(rows, d)
    h = _rms_mod(x, ng_ref[0:1, :], mod(1), mod(0)).astype(BF16)
    _proj_in(h, w_in_ref, xr_s, yg_s)

    def conv_in(j):
        return cst_ref[j] if j < CONV_WIDTH - 1 else xr_s[slab(j - (CONV_WIDTH - 1)), :]

    xc_slabs = []
    for t in range(nt):
        acc = cb_ref[...]
        for j in range(CONV_WIDTH):
            acc = acc + cw_ref[j:j + 1, :] * conv_in(t + j)
        xc_slabs.append(acc)
    xc = jnp.concatenate(xc_slabs, axis=0)
    for j in range(CONV_WIDTH - 1):
        conv_ref[j] = xr_s[slab(nt - (CONV_WIDTH - 1) + j), :]

    _rglru_gates(xc, gw_ref, gb_ref, lam_ref, a_s, u_s)

    hs = h0_ref[...]
    for t in range(nt):
        hs = a_s[slab(t), :] * hs + u_s[slab(t), :]
        o_s[slab(t), :] = hs
    hl_ref[...] = hs

    z = (o_s[...] * _gelu_tanh(yg_s[...])).astype(BF16)
    x1 = x + mod(2) * _dot(z, w_out_ref[...])

    h2 = _rms_mod(x1, ng_ref[1:2, :], mod(4), mod(3)).astype(BF16)
    x2 = x1 + mod(5) * _swiglu(h2, fin_ref, fout_ref, act_s)
    x2_ref[...] = x2.reshape(nt, sb, d)

    hk = _rms_mod(x2, kvg_ref[...], vec(kvmod_ref, 1), vec(kvmod_ref, 0)).astype(BF16)
    kv = _dot(hk, wkv_ref[...])
    hq = _rms_mod(x2, qg_ref[0:1, :], vec(qmod_ref, 1), vec(qmod_ref, 0)).astype(BF16)
    q = _dot(hq, wq_ref[...])
    k_slabs, q_slabs = [], []
    for t in range(nt):
        c, s_next, s_prev = rope_ref[0, t:t + 1, :], rope_ref[1, t:t + 1, :], rope_ref[2, t:t + 1, :]
        k_slabs.append(_rope(kv[slab(t), :KV_DIM], c, s_next, s_prev))
        q_slabs.append(_rope(q[slab(t), :], c, s_next, s_prev) * (HEAD_DIM ** -0.5))
    k_ref[...] = seq_major(jnp.concatenate(k_slabs, axis=0))
    q_ref[...] = seq_major(jnp.concatenate(q_slabs, axis=0))
    v_ref[...] = seq_major(kv[:, KV_DIM:])


def _sample_mod_spec(arr, lead, sb):
    return pl.BlockSpec((None, sb, arr.shape[2]), lambda i: (lead, i, 0))


def _sample_l0_call(x, ada, kv_ada, h0, cst, consts):
    nb, nt, d = x.shape
    sb = SAMPLE_BATCH_TILE
    d_ff = consts[10][0].shape[1]
    rows = nt * sb
    slab_spec = lambda lead, w: pl.BlockSpec((lead, sb, w), lambda i: (0, i, 0))
    seq_spec = lambda w: pl.BlockSpec((sb, nt, w), lambda i: (i, 0, 0))
    return pl.pallas_call(
        _sample_l0_kernel,
        grid=(nb // sb,),
        in_specs=[seq_spec(d), _sample_mod_spec(ada, 0, sb), _sample_mod_spec(kv_ada, 0, sb),
                  _sample_mod_spec(ada, 1, sb),
                  pl.BlockSpec((sb, d), lambda i: (i, 0)), slab_spec(CONV_WIDTH - 1, d)]
                 + [_resident(*c) for c in consts],
        out_specs=[slab_spec(nt, d), seq_spec(KV_DIM), seq_spec(KV_DIM), seq_spec(d),
                   slab_spec(CONV_WIDTH - 1, d), pl.BlockSpec((sb, d), lambda i: (i, 0))],
        out_shape=[jax.ShapeDtypeStruct((nt, nb, d), F32),
                   jax.ShapeDtypeStruct((nb, nt, KV_DIM), F32),
                   jax.ShapeDtypeStruct((nb, nt, KV_DIM), F32),
                   jax.ShapeDtypeStruct((nb, nt, d), F32),
                   jax.ShapeDtypeStruct((CONV_WIDTH - 1, nb, d), F32),
                   jax.ShapeDtypeStruct((nb, d), F32)],
        scratch_shapes=[pltpu.VMEM((rows, d), F32)] * 5 + [pltpu.VMEM((rows, d_ff), BF16)],
        compiler_params=pltpu.CompilerParams(
            dimension_semantics=("arbitrary",), vmem_limit_bytes=VMEM_LIMIT),
        name="sample_layer0",
    )(x, ada, kv_ada, ada, h0, cst, *[c[0] for c in consts])


def _sample_attn_kernel(sink_ref, q_ref, kn_ref, vn_ref, ck_ref, cv_ref,
                        attn_ref, ko_ref, vo_ref):
    sb, nt, d = q_ref.shape
    span = 2 * WINDOW

    tok = lax.broadcasted_iota(jnp.int32, (nt, span), 0)
    si = lax.broadcasted_iota(jnp.int32, (nt, span), 1)
    mask = (si >= tok) & (si <= tok + WINDOW)
    pad = jnp.zeros((HEAD_DIM, LANES - nt), F32)
    ones = jnp.ones((LANES, span), BF16)

    sinks = [[sink_ref[0, g * GROUP + i] for i in range(GROUP)] for g in range(N_KV_HEADS)]

    def windows(b):
        knt, vnt = kn_ref[b].T, vn_ref[b].T
        kwin, vaug = [], []
        for g in range(N_KV_HEADS):
            rows = slice(g * HEAD_DIM, (g + 1) * HEAD_DIM)
            tops = []
            for c_ref, nt_new, o_ref in ((ck_ref, knt[rows, :], ko_ref), (cv_ref, vnt[rows, :], vo_ref)):
                cache = c_ref[b, g]
                o_ref[b, g] = pltpu.roll(cache, WINDOW - nt, 1)
                o_ref[b, g, :, WINDOW - nt:WINDOW] = nt_new
                tops.append(jnp.concatenate([cache, nt_new, pad], axis=1))
            kwin.append(jnp.concatenate([tops[0], tops[0]], axis=0).astype(BF16))
            vaug.append(jnp.concatenate([tops[1].astype(BF16), tops[1].astype(BF16), ones], axis=0))
        return kwin, vaug

    def sequences(it, carry):
        seqs = [it * SAMPLE_ATTN_UNROLL + u for u in range(SAMPLE_ATTN_UNROLL)]
        wins = [windows(b) for b in seqs]
        units = [(u, g) for u in range(SAMPLE_ATTN_UNROLL) for g in range(N_KV_HEADS)]
        q_split = [[_split_halves(q_ref[b][:, p * LANES:(p + 1) * LANES]) for p in range(d // LANES)]
                   for b in seqs]
        scores = [_attn_scores([q_split[u][2 * g + i // 2][i % 2] for i in range(GROUP)], wins[u][0][g], True)
                  for u, g in units]
        probs = [_attn_probs(s_all, mask, sinks[g]) for s_all, (u, g) in zip(scores, units)]
        values = [_attn_values(p_all, wins[u][1][g], True) for (p_all, _), (u, g) in zip(probs, units)]
        pairs = [[] for _ in seqs]
        for res_all, (_, maxes), (u, g) in zip(values, probs, units):
            pairs[u] += _attn_finish(res_all, maxes, sinks[g])
        for u, b in enumerate(seqs):
            attn_ref[b] = jnp.concatenate(pairs[u], axis=1)
        return carry

    lax.fori_loop(0, sb // SAMPLE_ATTN_UNROLL, sequences, 0)


def _sample_attn_call(sinks, q, kn, vn, ck, cv):
    nb, nt, d = q.shape
    sb = SAMPLE_ATTN_BATCH
    seq_spec = lambda r, w: pl.BlockSpec((sb, r, w), lambda i: (i, 0, 0))
    cache_spec = pl.BlockSpec((sb,) + ck.shape[1:], lambda i: (i, 0, 0, 0))
    return pl.pallas_call(
        _sample_attn_kernel,
        grid=(nb // sb,),
        in_specs=[pl.BlockSpec(memory_space=pltpu.SMEM),
                  seq_spec(nt, d), seq_spec(nt, KV_DIM), seq_spec(nt, KV_DIM), cache_spec, cache_spec],
        out_specs=[seq_spec(nt, d), cache_spec, cache_spec],
        out_shape=[jax.ShapeDtypeStruct((nb, nt, d), F32),
                   jax.ShapeDtypeStruct(ck.shape, F32),
                   jax.ShapeDtypeStruct(cv.shape, F32)],
        compiler_params=pltpu.CompilerParams(
            dimension_semantics=("arbitrary",), vmem_limit_bytes=VMEM_LIMIT),
        name="sample_attention",
    )(sinks, q, kn, vn, ck, cv)


def _sample_l1_kernel(x_ref, attn_ref, mod_ref, ng_ref, wo_ref, fin_ref, fout_ref, fg_ref,
                      y_ref, act_s):
    nt, sb, d = x_ref.shape
    rows = nt * sb
    mod = lambda i: _tile_rows(mod_ref[:, i * d:(i + 1) * d], nt)
    x = x_ref[...].reshape(rows, d)
    attn = jnp.swapaxes(attn_ref[...], 0, 1).reshape(rows, d).astype(BF16)
    x1 = x + mod(2) * _dot(attn, wo_ref[...])
    h2 = _rms_mod(x1, ng_ref[1:2, :], mod(4), mod(3)).astype(BF16)
    x2 = x1 + mod(5) * _swiglu(h2, fin_ref, fout_ref, act_s)
    y_ref[...] = jnp.swapaxes((_rms(x2) * fg_ref[...]).reshape(nt, sb, d), 0, 1)


def _sample_l1_call(x, attn, ada, consts):
    nt, nb, d = x.shape
    sb = SAMPLE_BATCH_TILE
    d_ff = consts[3][0].shape[1]
    slab_spec = pl.BlockSpec((nt, sb, d), lambda i: (0, i, 0))
    seq_spec = pl.BlockSpec((sb, nt, d), lambda i: (i, 0, 0))
    return pl.pallas_call(
        _sample_l1_kernel,
        grid=(nb // sb,),
        in_specs=[slab_spec, seq_spec, _sample_mod_spec(ada, 1, sb)]
                 + [_resident(*c) for c in consts],
        out_specs=seq_spec,
        out_shape=jax.ShapeDtypeStruct((nb, nt, d), F32),
        scratch_shapes=[pltpu.VMEM((nt * sb, d_ff), BF16)],
        compiler_params=pltpu.CompilerParams(
            dimension_semantics=("arbitrary",), vmem_limit_bytes=VMEM_LIMIT),
        name="sample_layer1",
    )(x, attn, ada, *[c[0] for c in consts])


def _rope_tables(pos):
    half = ROT_DIM // 2
    inv = ROPE_THETA ** (-jnp.arange(0, ROT_DIM, 2, dtype=F32) / ROT_DIM)
    ang = pos.astype(F32)[:, None] * inv[None, :]
    cos, sin = jnp.cos(ang), jnp.sin(ang)
    n = pos.shape[0]
    rest = HEAD_DIM - ROT_DIM
    c = jnp.concatenate([cos, cos, jnp.ones((n, rest), F32)], axis=1)
    s_next = jnp.concatenate([-sin, jnp.zeros((n, half + rest), F32)], axis=1)
    s_prev = jnp.concatenate([jnp.zeros((n, half), F32), sin, jnp.zeros((n, rest), F32)], axis=1)
    reps = LANES // HEAD_DIM
    return jnp.stack([jnp.tile(c, (1, reps)), jnp.tile(s_next, (1, reps)), jnp.tile(s_prev, (1, reps))])


def kernel(x_prompt, x_sample, c_prompt, c_sample, state_conv, state_h, cache_k, cache_v, ada_w, ada_b, norm_g, rnn_w_in, rnn_conv_w, rnn_conv_b, rnn_gate_w, rnn_gate_b, rnn_lambda, rnn_w_out, kv_ada_w, kv_ada_b, kv_norm_g, w_kv, attn_w_q, attn_sinks, attn_w_o, ffn_w_in, ffn_w_out, final_g):
    nb_p, t_p, d = x_prompt.shape
    nb_s, t_s, _ = x_sample.shape

    assert rnn_w_in.shape[0] == 1 and attn_w_q.shape[0] == 1 and nb_s % SUBLANES == 0

    c_all = jnp.concatenate([c_sample, c_prompt], axis=0)
    ada = _ada_call(c_all, ada_w, ada_b)
    kv_ada = _ada_call(c_all, kv_ada_w[None], kv_ada_b[None])
    prompt_row_block = nb_s // SUBLANES

    bf = lambda w: w.astype(BF16)
    row = lambda v: v.reshape(1, -1)
    ffn_in, ffn_out = bf(ffn_w_in), bf(ffn_w_out)
    wq, wo = (bf(attn_w_q), 0), (bf(attn_w_o), 0)
    l0_consts = [(norm_g, 0), (bf(rnn_w_in), 0), (rnn_conv_w, 0), (rnn_conv_b, None), (bf(rnn_gate_w), 0),
                 (rnn_gate_b, 0), (rnn_lambda, None), (bf(rnn_w_out), 0), (ffn_in, 0), (ffn_out, 0),
                 (row(kv_norm_g), None), (bf(w_kv), None)]
    l1_ffn = [(ffn_in, 1), (ffn_out, 1), (row(final_g), None)]

    rope_p = _rope_tables(jnp.arange(t_p, dtype=jnp.int32))
    x2_p, k_p, v_p, conv_p, hl_p = _prompt_l0_call(x_prompt, ada, kv_ada, prompt_row_block, rope_p, l0_consts)
    y_prompt = _prompt_l1_call(attn_sinks, x2_p, k_p, v_p, ada, prompt_row_block, rope_p,
                               [(norm_g, 1), wq, wo] + l1_ffn)

    rope_s = _rope_tables(PAST_LEN + jnp.arange(t_s, dtype=jnp.int32))
    x2_s, k_s, v_s, q_s, conv_s, hl_s = _sample_l0_call(
        x_sample, ada, kv_ada, state_h[0], state_conv[0].transpose(1, 0, 2),
        [(rope_s, None)] + l0_consts + [(norm_g, 1), wq])
    attn_s, ko_s, vo_s = _sample_attn_call(
        attn_sinks, q_s, k_s, v_s, cache_k.transpose(0, 2, 3, 1), cache_v.transpose(0, 2, 3, 1))
    y_s = _sample_l1_call(x2_s, attn_s, ada, [(norm_g, 1), wo] + l1_ffn)

    kv_shape = (WINDOW, N_KV_HEADS, HEAD_DIM)
    return (y_prompt,
            y_s,
            conv_p[None, :, SUBLANES - (CONV_WIDTH - 1):, :],
            hl_p[None, :, 0, :],
            k_p[:, t_p - WINDOW:, :].reshape((nb_p,) + kv_shape),
            v_p[:, t_p - WINDOW:, :].reshape((nb_p,) + kv_shape),
            conv_s.transpose(1, 0, 2)[None],
            hl_s[None],
            ko_s.transpose(0, 3, 1, 2),
            vo_s.transpose(0, 3, 1, 2))
```

```python
import functools

import jax
import jax.numpy as jnp
from jax import lax
from jax.experimental import pallas as pl
from jax.experimental.pallas import tpu as pltpu

F32 = jnp.float32
BF16 = jnp.bfloat16

D_MODEL = 1024
N_RNN_BLOCKS = 8
RNN_BLOCK = D_MODEL // N_RNN_BLOCKS
CONV_WIDTH = 4
RG_C = 8.0
HEAD_DIM = 64
N_HEADS = D_MODEL // HEAD_DIM
N_KV_HEADS = 4
GROUP = N_HEADS // N_KV_HEADS
KV_DIM = N_KV_HEADS * HEAD_DIM
WINDOW = 128
ROT_DIM = HEAD_DIM // 4
ROPE_THETA = 500000.0
EPS = 1e-6
NEG_INF = -1e30
LOG2_E = 1.4426950408889634
QUERY_SCALE = HEAD_DIM ** -0.5 * LOG2_E
GELU_C = 0.7978845608028654
GELU_K = 0.044715
PAST_LEN = 16384

LANES = 128
SUBLANES = 8
MXU_COLS = 256
VMEM_LIMIT = 56 * 1024 * 1024

PROMPT_TILE = 256
FFN_SPLIT = 3
L1_FFN_SPLITS = (3, 7)
ADA_TILE_N = 2048
SAMPLE_ATTN_BATCH = 16
SAMPLE_ATTN_UNROLL = 4
SAMPLE_BATCH_TILE = 32


def _dot(a, b):
    return jnp.dot(a, b, preferred_element_type=F32)


def _dot_nt(a, b):
    return lax.dot_general(a, b, (((1,), (1,)), ((), ())), preferred_element_type=F32)


def _sigmoid(x):
    return 1.0 / (1.0 + jnp.exp2(x * (-LOG2_E)))


def _silu(x):
    return x * _sigmoid(x)


def _gelu_tanh(x):
    half = 0.5 * x
    return half + half * jnp.tanh(x * (GELU_C + (GELU_C * GELU_K) * (x * x)))


def _log_sigmoid(x):
    return -(jnp.maximum(-x, 0.0) + jnp.log1p(jnp.exp(-jnp.abs(x))))


def _rms(x):
    return x * lax.rsqrt(jnp.mean(x * x, axis=-1, keepdims=True) + EPS)


def _rms_mod(x, g, scale, shift):
    return _rms(x) * (g * (1.0 + scale)) + shift


def _tile_rows(m, reps):
    return jnp.concatenate([m] * reps, axis=0)


def _rope_block(blk, c, s_next, s_prev):
    return blk * c + pltpu.roll(blk, LANES - ROT_DIM // 2, 1) * s_next + pltpu.roll(blk, ROT_DIM // 2, 1) * s_prev


def _rope(x, c, s_next, s_prev):
    blocks = [_rope_block(x[:, j * LANES:(j + 1) * LANES], c, s_next, s_prev)
              for j in range(x.shape[1] // LANES)]
    return jnp.concatenate(blocks, axis=1)


def _proj_in(h, w_in_ref, xr_s, yg_s):
    d = xr_s.shape[1]
    cw = 2 * MXU_COLS
    for c in range(d // cw):
        xr_s[:, c * cw:(c + 1) * cw] = _dot(h, w_in_ref[:, c * cw:(c + 1) * cw])
        yg_s[:, c * cw:(c + 1) * cw] = _dot(h, w_in_ref[:, d + c * cw:d + (c + 1) * cw])


def _rglru_gates(xc, gw_ref, gb_ref, lam_ref, a_s, u_s):
    xcb = xc.astype(BF16)
    cl = RG_C * _log_sigmoid(lam_ref[...])
    for n in range(N_RNN_BLOCKS):
        blk = slice(n * RNN_BLOCK, (n + 1) * RNN_BLOCK)
        g = _dot(xcb[:, blk], gw_ref[n]) + gb_ref[n:n + 1, :]
        r = _sigmoid(g[:, :RNN_BLOCK])
        i = _sigmoid(g[:, RNN_BLOCK:])
        log_a = cl[:, blk] * r
        a = jnp.exp(log_a)
        a_s[:, blk] = a
        u_s[:, blk] = jnp.sqrt(-jnp.tanh(log_a) * (a * a + 1.0)) * (i * xc[:, blk])


def _swiglu_in(h, fin_ref, act_s, chunks):
    d_ff = act_s.shape[1]
    for c in chunks:
        cols = slice(c * MXU_COLS, (c + 1) * MXU_COLS)
        gate = _dot(h, fin_ref[:, cols])
        up = _dot(h, fin_ref[:, d_ff + c * MXU_COLS:d_ff + (c + 1) * MXU_COLS])
        act_s[:, cols] = (_silu(gate) * up).astype(BF16)


def _swiglu(h, fin_ref, fout_ref, act_s):
    _swiglu_in(h, fin_ref, act_s, range(fout_ref.shape[0] // MXU_COLS))
    return _dot(act_s[...], fout_ref[...])


def _low_half(shape):
    return lax.broadcasted_iota(jnp.int32, shape, 1) < LANES // 2


def _dup_halves(blk):
    low = _low_half(blk.shape)
    rot = pltpu.roll(blk, LANES // 2, 1)
    return jnp.where(low, blk, rot), jnp.where(low, rot, blk)


def _split_halves(blk):
    low = _low_half(blk.shape)
    zero = jnp.zeros_like(blk)
    return jnp.where(low, blk, zero), jnp.where(low, zero, blk)


def _attn_scores(q_heads, kwin, keys_on_lanes=False):
    q_all = jnp.concatenate(q_heads, axis=0).astype(BF16)
    return _dot(q_all, kwin) if keys_on_lanes else _dot_nt(q_all, kwin)


def _attn_probs(s_all, mask, sinks):
    rb = s_all.shape[0] // GROUP
    probs, maxes = [], []
    for i in range(GROUP):
        s = jnp.where(mask, s_all[i * rb:(i + 1) * rb, :], NEG_INF)
        mx = jnp.maximum(jnp.max(s, axis=-1, keepdims=True), sinks[i])
        probs.append(jnp.exp2(s - mx))
        maxes.append(mx)
    return jnp.concatenate(probs, axis=0).astype(BF16), maxes


def _attn_values(p_all, vaug, keys_on_lanes=False):
    return _dot_nt(p_all, vaug) if keys_on_lanes else _dot(p_all, vaug)


def _attn_finish(res_all, maxes, sinks):
    rb = res_all.shape[0] // GROUP
    outs = []
    for i in range(GROUP):
        res = res_all[i * rb:(i + 1) * rb, :]
        den = res[:, LANES:] + jnp.exp2(sinks[i] - maxes[i])
        outs.append(res[:, :LANES] * (1.0 / den))
    low = _low_half(outs[0].shape)
    return [jnp.where(low, outs[2 * i], outs[2 * i + 1]) for i in range(GROUP // 2)]


def _ada_kernel(c_ref, w_ref, b_ref, o_ref):
    c = c_ref[...]
    o_ref[0] = _dot(_silu(c).astype(BF16), w_ref[0].astype(BF16)) + b_ref[0]


def _ada_call(c, w, b):
    n_layers, d, n = w.shape
    r = c.shape[0]
    return pl.pallas_call(
        _ada_kernel,
        grid=(n_layers, n // ADA_TILE_N),
        in_specs=[pl.BlockSpec((r, d), lambda l, j: (0, 0)),
                  pl.BlockSpec((1, d, ADA_TILE_N), lambda l, j: (l, 0, j)),
                  pl.BlockSpec((1, 1, ADA_TILE_N), lambda l, j: (l, 0, j))],
        out_specs=pl.BlockSpec((1, r, ADA_TILE_N), lambda l, j: (l, 0, j)),
        out_shape=jax.ShapeDtypeStruct((n_layers, r, n), F32),
        compiler_params=pltpu.CompilerParams(
            dimension_semantics=("arbitrary", "arbitrary"), vmem_limit_bytes=VMEM_LIMIT),
        name="ada_mod",
    )(c, w, b.reshape(n_layers, 1, n))


def _prompt_l0_kernel(x_ref, mod_ref, kvmod_ref, rope_ref, ng_ref, w_in_ref, cw_ref, cb_ref, gw_ref,
                      gb_ref, lam_ref, w_out_ref, fin_ref, fout_ref, kvg_ref, wkv_ref,
                      x2_ref, k_ref, v_ref, conv_ref, hl_ref,
                      xr_s, yg_s, act_s, hist_s, hc_s, z_s, xk_s, x2k_s, scan_s, *, nt, n_tiles):
    tm, d = x_ref.shape
    s = pl.program_id(0)
    n_chunks = fout_ref.shape[0] // MXU_COLS
    b_cur = jnp.minimum(s, n_tiles - 1) // nt
    b_prev = jnp.clip(s - 1, 0, n_tiles - 1) // nt
    b_pp = jnp.maximum(s - 2, 0) // nt
    mod = lambda i: _mod_row(mod_ref, b_cur, i)
    modp = lambda i: _mod_row(mod_ref, b_prev, i)

    @pl.when(s == 0)
    def _():
        z_s[...] = jnp.zeros_like(z_s)
        xk_s[...] = jnp.zeros_like(xk_s)
        x2k_s[...] = jnp.zeros_like(x2k_s)

    @pl.when(lax.rem(jnp.minimum(s, n_tiles - 1), nt) == 0)
    def _():
        hist_s[...] = jnp.zeros_like(hist_s)
        hc_s[...] = jnp.zeros_like(hc_s)

    out_prev = _dot(z_s[...], w_out_ref[...])
    x_prev = xk_s[...]
    x2_pp = x2k_s[...]

    hk = _rms_mod(x2_pp, kvg_ref[...], _mod_row(kvmod_ref, b_pp, 1), _mod_row(kvmod_ref, b_pp, 0)).astype(BF16)
    kv = _dot(hk, wkv_ref[...])
    k_ref[...] = _rope(kv[:, :KV_DIM], rope_ref[0], rope_ref[1], rope_ref[2])
    v_ref[...] = kv[:, KV_DIM:]

    x = x_ref[...]
    h = _rms_mod(x, ng_ref[0:1, :], mod(1), mod(0)).astype(BF16)
    _proj_in(h, w_in_ref, xr_s.at[SUBLANES:SUBLANES + tm], yg_s)
    xk_s[...] = x

    x1 = x_prev + modp(2) * out_prev
    h2 = _rms_mod(x1, ng_ref[1:2, :], modp(4), modp(3)).astype(BF16)
    _swiglu_in(h2, fin_ref, act_s, range(0, FFN_SPLIT))

    cur = slice(SUBLANES, SUBLANES + tm)
    xr_s[0:SUBLANES, :] = hist_s[...]
    xc = cb_ref[...]
    for j in range(CONV_WIDTH):
        k = CONV_WIDTH - 1 - j
        xc = xc + cw_ref[j:j + 1, :] * xr_s[SUBLANES - k:SUBLANES - k + tm, :]
    hist_s[...] = xr_s[tm:tm + SUBLANES, :]
    xr_s[cur, :] = xc

    ncl = -RG_C * _log_sigmoid(lam_ref[...])
    row8b = lax.broadcasted_iota(jnp.int32, (SUBLANES, RNN_BLOCK), 0)
    chunk = tm // SUBLANES
    pitch = chunk + 4

    def recurrent_block(n):
        blk = slice(n * RNN_BLOCK, (n + 1) * RNN_BLOCK)
        a_s, u_s, h_s, p_s = (scan_s.at[n % 2, j] for j in range(4))
        xc_blk = xr_s[cur, blk]
        gates = _dot(xc_blk.astype(BF16), gw_ref[n]) + gb_ref[n:n + 1, :]
        r = _sigmoid(gates[:, :RNN_BLOCK])
        i = _sigmoid(gates[:, RNN_BLOCK:])
        neg_log_a = ncl[:, blk] * r
        a_all = jnp.exp2(neg_log_a * (-LOG2_E))
        w = jnp.tanh(neg_log_a) * (a_all * a_all + 1.0)
        u_all = jnp.where(w == 0.0, 0.0, w * lax.rsqrt(w)) * (i * xc_blk)
        for c in range(SUBLANES):
            a_s[c * pitch:c * pitch + chunk, :] = a_all[c * chunk:(c + 1) * chunk, :]
            u_s[c * pitch:c * pitch + chunk, :] = u_all[c * chunk:(c + 1) * chunk, :]
        h = jnp.zeros((SUBLANES, RNN_BLOCK), F32)
        p = jnp.ones((SUBLANES, RNN_BLOCK), F32)
        for g in range(chunk):
            step_rows = pl.ds(g, SUBLANES, stride=pitch)
            a = a_s[step_rows, :]
            h = a * h + u_s[step_rows, :]
            p = a * p
            h_s[step_rows, :] = h
            p_s[step_rows, :] = p
        for step in (1, 2, 4):
            keep = row8b >= step
            h = p * jnp.where(keep, pltpu.roll(h, step, 0), 0.0) + h
            p = p * jnp.where(keep, pltpu.roll(p, step, 0), 1.0)
        hprev = hc_s[:, blk]
        ends = h + p * hprev
        incoming = jnp.where(row8b >= 1, pltpu.roll(ends, 1, 0), hprev)
        hc_s[:, blk] = jnp.broadcast_to(ends[SUBLANES - 1:SUBLANES, :], ends.shape)
        for c in range(SUBLANES):
            rows = slice(c * chunk, (c + 1) * chunk)
            local = slice(c * pitch, c * pitch + chunk)
            hs = h_s[local, :] + p_s[local, :] * incoming[c:c + 1, :]
            z_s[rows, blk] = (hs * _gelu_tanh(yg_s[rows, blk])).astype(BF16)

    blocks = list(range(N_RNN_BLOCKS))
    for c in range(FFN_SPLIT, n_chunks):
        _swiglu_in(h2, fin_ref, act_s, [c])
        if blocks:
            recurrent_block(blocks.pop(0))
    ffn = _dot(act_s[...], fout_ref[...])
    for n in blocks:
        recurrent_block(n)

    x2 = jnp.where(s <= n_tiles, x1 + modp(5) * ffn, x2_pp)
    x2_ref[...] = x2
    x2k_s[...] = x2

    @pl.when(s < n_tiles)
    def _():
        conv_ref[...] = hist_s[...]
        hl_ref[...] = hc_s[...]


def _resident(arr, lead=None):
    if lead is None:
        index, shape = (0,) * arr.ndim, arr.shape
    else:
        index, shape = (lead,) + (0,) * (arr.ndim - 1), (None,) + arr.shape[1:]
    return pl.BlockSpec(shape, lambda *_: index, pipeline_mode=pl.Buffered(1))


def _prompt_mod_specs(ada, kv_ada, layer, row_block):
    return [_resident_rows(ada, layer, row_block), _resident_rows(kv_ada, 0, row_block)]


def _resident_rows(arr, lead, row_block):
    index = (lead, row_block, 0)
    return pl.BlockSpec((None, SUBLANES, arr.shape[2]), lambda *_: index, pipeline_mode=pl.Buffered(1))


def _mod_row(ref, b, i):
    return ref[pl.ds(b, 1), i * D_MODEL:(i + 1) * D_MODEL]


def _prompt_l0_call(x, ada, kv_ada, mod_row_block, rope, consts):
    nb, t, d = x.shape
    tm = PROMPT_TILE
    nt = t // tm
    n_tiles = nb * nt
    d_ff = consts[9][0].shape[1]
    cur = lambda s: jnp.minimum(s, n_tiles - 1)
    prv = lambda s: jnp.clip(s - 1, 0, n_tiles - 1)
    pp = lambda s: jnp.maximum(s - 2, 0)
    tile_spec = lambda w, f: pl.BlockSpec((None, tm, w), lambda s: (f(s) // nt, lax.rem(f(s), nt), 0))
    state_spec = pl.BlockSpec((None, SUBLANES, d), lambda s: (cur(s) // nt, 0, 0))
    return pl.pallas_call(
        functools.partial(_prompt_l0_kernel, nt=nt, n_tiles=n_tiles),
        grid=(n_tiles + 2,),
        in_specs=[tile_spec(d, cur)] + _prompt_mod_specs(ada, kv_ada, 0, mod_row_block)
                 + [pl.BlockSpec((3, tm, LANES), lambda s: (0, lax.rem(pp(s), nt), 0))]
                 + [_resident(*c) for c in consts],
        out_specs=[tile_spec(d, prv), tile_spec(KV_DIM, pp), tile_spec(KV_DIM, pp),
                   state_spec, state_spec],
        out_shape=[jax.ShapeDtypeStruct((nb, t, d), F32),
                   jax.ShapeDtypeStruct((nb, t, KV_DIM), F32),
                   jax.ShapeDtypeStruct((nb, t, KV_DIM), F32),
                   jax.ShapeDtypeStruct((nb, SUBLANES, d), F32),
                   jax.ShapeDtypeStruct((nb, SUBLANES, d), F32)],
        scratch_shapes=[pltpu.VMEM((SUBLANES + tm, d), F32), pltpu.VMEM((tm, d), F32),
                        pltpu.VMEM((tm, d_ff), BF16),
                        pltpu.VMEM((SUBLANES, d), F32), pltpu.VMEM((SUBLANES, d), F32),
                        pltpu.VMEM((tm, d), BF16), pltpu.VMEM((tm, d), F32), pltpu.VMEM((tm, d), F32),
                        pltpu.VMEM((2, 4, tm + SUBLANES * SUBLANES, RNN_BLOCK), F32)],
        compiler_params=pltpu.CompilerParams(
            dimension_semantics=("arbitrary",), vmem_limit_bytes=VMEM_LIMIT),
        name="prompt_layer0",
    )(x, ada, kv_ada, rope, *[c[0] for c in consts])


def _prompt_l1_kernel(sink_ref, x_ref, k_ref, v_ref, mod_ref, rope_ref, ng_ref, wq_ref, wo_ref,
                      fin_ref, fout_ref, fg_ref,
                      y_ref,
                      kw_s, vw_s, attn_s, act_s, xk_s, *, nt, n_tiles):
    tm, d = x_ref.shape
    s = pl.program_id(0)
    n_chunks = fout_ref.shape[0] // MXU_COLS
    cur_tile = jnp.minimum(s, n_tiles - 1)
    t = lax.rem(cur_tile, nt)
    b_cur = cur_tile // nt
    b_prev = jnp.maximum(s - 1, 0) // nt
    mod = lambda i: _mod_row(mod_ref, b_cur, i)
    modp = lambda i: _mod_row(mod_ref, b_prev, i)

    @pl.when(s == 0)
    def _():
        attn_s[...] = jnp.zeros_like(attn_s)
        xk_s[...] = jnp.zeros_like(xk_s)

    @pl.when(t == 0)
    def _():
        kw_s[0:WINDOW, :] = jnp.zeros((WINDOW, kw_s.shape[1]), BF16)
        vw_s[0:WINDOW, :] = jnp.zeros((WINDOW, vw_s.shape[1]), BF16)

    @pl.when(t > 0)
    def _():
        kw_s[0:WINDOW, :] = kw_s[tm:tm + WINDOW, :]
        vw_s[0:WINDOW, :] = vw_s[tm:tm + WINDOW, :]

    out_prev = _dot(attn_s[...], wo_ref[...])
    x_prev = xk_s[...]

    x = x_ref[...]
    h = _rms_mod(x, ng_ref[0:1, :], mod(1), mod(0)).astype(BF16)
    q = _rope(_dot(h, wq_ref[...]), rope_ref[0], rope_ref[1], rope_ref[2]) * QUERY_SCALE
    q_split = [_split_halves(q[:, p * LANES:(p + 1) * LANES]) for p in range(d // LANES)]
    xk_s[...] = x

    new_rows = slice(WINDOW, WINDOW + tm)
    k, v = k_ref[...], v_ref[...]
    ones = jnp.ones((tm, LANES), BF16)
    for pb in range(KV_DIM // LANES):
        k_dup = _dup_halves(k[:, pb * LANES:(pb + 1) * LANES])
        v_dup = _dup_halves(v[:, pb * LANES:(pb + 1) * LANES])
        for i in range(2):
            g = 2 * pb + i
            kw_s[new_rows, g * LANES:(g + 1) * LANES] = k_dup[i].astype(BF16)
            vw_s[new_rows, 2 * g * LANES:(2 * g + 1) * LANES] = v_dup[i].astype(BF16)
            vw_s[new_rows, (2 * g + 1) * LANES:(2 * g + 2) * LANES] = ones

    x1 = x_prev + modp(2) * out_prev
    h2 = _rms_mod(x1, ng_ref[1:2, :], modp(4), modp(3)).astype(BF16)
    _swiglu_in(h2, fin_ref, act_s, range(0, L1_FFN_SPLITS[0]))

    span = 2 * WINDOW
    qi = lax.broadcasted_iota(jnp.int32, (WINDOW, span), 0)
    si = lax.broadcasted_iota(jnp.int32, (WINDOW, span), 1)
    band = (si >= qi) & (si <= qi + WINDOW)
    masks = [band & (si >= WINDOW - (t * tm + j * WINDOW)) for j in range(tm // WINDOW)]
    units = [(j, g) for j in range(tm // WINDOW) for g in range(N_KV_HEADS)]
    rows = lambda j: slice(j * WINDOW, (j + 1) * WINDOW)
    win = lambda j: slice(j * WINDOW, j * WINDOW + span)
    sinks = [[sink_ref[0, g * GROUP + i] * LOG2_E for i in range(GROUP)] for g in range(N_KV_HEADS)]
    scores = [_attn_scores([q_split[2 * g + i // 2][i % 2][rows(j), :] for i in range(GROUP)],
                           kw_s[win(j), g * LANES:(g + 1) * LANES]) for j, g in units]
    _swiglu_in(h2, fin_ref, act_s, range(L1_FFN_SPLITS[0], L1_FFN_SPLITS[1]))
    probs = [_attn_probs(s_all, masks[j], sinks[g]) for s_all, (j, g) in zip(scores, units)]
    values = [_attn_values(p_all, vw_s[win(j), 2 * g * LANES:(2 * g + 2) * LANES])
              for (p_all, _), (j, g) in zip(probs, units)]
    _swiglu_in(h2, fin_ref, act_s, range(L1_FFN_SPLITS[1], n_chunks))
    for res_all, (_, maxes), (j, g) in zip(values, probs, units):
        for i, pair in enumerate(_attn_finish(res_all, maxes, sinks[g])):
            col = (2 * g + i) * LANES
            attn_s[rows(j), col:col + LANES] = pair.astype(BF16)

    x2 = x1 + modp(5) * _dot(act_s[...], fout_ref[...])
    y_ref[...] = _rms(x2) * fg_ref[...]


def _prompt_l1_call(sinks, x, k, v, ada, mod_row_block, rope, consts):
    nb, t, d = x.shape
    tm = PROMPT_TILE
    nt = t // tm
    n_tiles = nb * nt
    d_ff = consts[4][0].shape[1]
    cur = lambda s: jnp.minimum(s, n_tiles - 1)
    prv = lambda s: jnp.maximum(s - 1, 0)
    tile_spec = lambda w, f: pl.BlockSpec((None, tm, w), lambda s: (f(s) // nt, lax.rem(f(s), nt), 0))
    return pl.pallas_call(
        functools.partial(_prompt_l1_kernel, nt=nt, n_tiles=n_tiles),
        grid=(n_tiles + 1,),
        in_specs=[pl.BlockSpec(memory_space=pltpu.SMEM),
                  tile_spec(d, cur), tile_spec(KV_DIM, cur), tile_spec(KV_DIM, cur),
                  _resident_rows(ada, 1, mod_row_block),
                  pl.BlockSpec((3, tm, LANES), lambda s: (0, lax.rem(cur(s), nt), 0))]
                 + [_resident(*c) for c in consts],
        out_specs=tile_spec(d, prv),
        out_shape=jax.ShapeDtypeStruct((nb, t, d), F32),
        scratch_shapes=[pltpu.VMEM((WINDOW + tm, N_KV_HEADS * LANES), BF16),
                        pltpu.VMEM((WINDOW + tm, N_KV_HEADS * 2 * LANES), BF16),
                        pltpu.VMEM((tm, d), BF16),
                        pltpu.VMEM((tm, d_ff), BF16),
                        pltpu.VMEM((tm, d), F32)],
        compiler_params=pltpu.CompilerParams(
            dimension_semantics=("arbitrary",), vmem_limit_bytes=VMEM_LIMIT),
        name="prompt_layer1",
    )(sinks, x, k, v, ada, rope, *[c[0] for c in consts])


def _sample_l0_kernel(x_ref, mod_ref, kvmod_ref, qmod_ref, h0_ref, cst_ref, rope_ref, ng_ref, w_in_ref,
                      cw_ref, cb_ref, gw_ref, gb_ref, lam_ref, w_out_ref, fin_ref, fout_ref, kvg_ref,
                      wkv_ref, qg_ref, wq_ref,
                      x2_ref, k_ref, v_ref, q_ref, conv_ref, hl_ref,
                      xr_s, yg_s, a_s, u_s, o_s, act_s):
    sb, nt, d = x_ref.shape
    rows = nt * sb
    slab = lambda t: slice(t * sb, (t + 1) * sb)
    vec = lambda ref, i: _tile_rows(ref[:, i * d:(i + 1) * d], nt)
    mod = lambda i: vec(mod_ref, i)
    seq_major = lambda a: jnp.swapaxes(a.reshape(nt, sb, a.shape[-1]), 0, 1)

    x = jnp.swapaxes(x_ref[...], 0, 1).reshape(rows, d)
    h = _rms_mod(x, ng_ref[0:1, :], mod(1), mod(0)).astype(BF16)
    _proj_in(h, w_in_ref, xr_s, yg_s)

    def conv_in(j):
        return cst_ref[j] if j < CONV_WIDTH - 1 else xr_s[slab(j - (CONV_WIDTH - 1)), :]

    xc_slabs = []
    for t in range(nt):
        acc = cb_ref[...]
        for j in range(CONV_WIDTH):
            acc = acc + cw_ref[j:j + 1, :] * conv_in(t + j)
        xc_slabs.append(acc)
    xc = jnp.concatenate(xc_slabs, axis=0)
    for j in range(CONV_WIDTH - 1):
        conv_ref[j] = xr_s[slab(nt - (CONV_WIDTH - 1) + j), :]

    _rglru_gates(xc, gw_ref, gb_ref, lam_ref, a_s, u_s)

    hs = h0_ref[...]
    for t in range(nt):
        hs = a_s[slab(t), :] * hs + u_s[slab(t), :]
        o_s[slab(t), :] = hs
    hl_ref[...] = hs

    z = (o_s[...] * _gelu_tanh(yg_s[...])).astype(BF16)
    x1 = x + mod(2) * _dot(z, w_out_ref[...])

    h2 = _rms_mod(x1, ng_ref[1:2, :], mod(4), mod(3)).astype(BF16)
    x2 = x1 + mod(5) * _swiglu(h2, fin_ref, fout_ref, act_s)
    x2_ref[...] = x2.reshape(nt, sb, d)

    hk = _rms_mod(x2, kvg_ref[...], vec(kvmod_ref, 1), vec(kvmod_ref, 0)).astype(BF16)
    kv = _dot(hk, wkv_ref[...])
    hq = _rms_mod(x2, qg_ref[0:1, :], vec(qmod_ref, 1), vec(qmod_ref, 0)).astype(BF16)
    q = _dot(hq, wq_ref[...])
    k_slabs, q_slabs = [], []
    for t in range(nt):
        c, s_next, s_prev = rope_ref[0, t:t + 1, :], rope_ref[1, t:t + 1, :], rope_ref[2, t:t + 1, :]
        k_slabs.append(_rope(kv[slab(t), :KV_DIM], c, s_next, s_prev))
        q_slabs.append(_rope(q[slab(t), :], c, s_next, s_prev) * QUERY_SCALE)
    k_ref[...] = seq_major(jnp.concatenate(k_slabs, axis=0))
    q_ref[...] = seq_major(jnp.concatenate(q_slabs, axis=0))
    v_ref[...] = seq_major(kv[:, KV_DIM:])


def _sample_mod_spec(arr, lead, sb):
    return pl.BlockSpec((None, sb, arr.shape[2]), lambda i: (lead, i, 0))


def _sample_l0_call(x, ada, kv_ada, h0, cst, consts):
    nb, nt, d = x.shape
    sb = SAMPLE_BATCH_TILE
    d_ff = consts[10][0].shape[1]
    rows = nt * sb
    slab_spec = lambda lead, w: pl.BlockSpec((lead, sb, w), lambda i: (0, i, 0))
    seq_spec = lambda w: pl.BlockSpec((sb, nt, w), lambda i: (i, 0, 0))
    return pl.pallas_call(
        _sample_l0_kernel,
        grid=(nb // sb,),
        in_specs=[seq_spec(d), _sample_mod_spec(ada, 0, sb), _sample_mod_spec(kv_ada, 0, sb),
                  _sample_mod_spec(ada, 1, sb),
                  pl.BlockSpec((sb, d), lambda i: (i, 0)), slab_spec(CONV_WIDTH - 1, d)]
                 + [_resident(*c) for c in consts],
        out_specs=[slab_spec(nt, d), seq_spec(KV_DIM), seq_spec(KV_DIM), seq_spec(d),
                   slab_spec(CONV_WIDTH - 1, d), pl.BlockSpec((sb, d), lambda i: (i, 0))],
        out_shape=[jax.ShapeDtypeStruct((nt, nb, d), F32),
                   jax.ShapeDtypeStruct((nb, nt, KV_DIM), F32),
                   jax.ShapeDtypeStruct((nb, nt, KV_DIM), F32),
                   jax.ShapeDtypeStruct((nb, nt, d), F32),
                   jax.ShapeDtypeStruct((CONV_WIDTH - 1, nb, d), F32),
                   jax.ShapeDtypeStruct((nb, d), F32)],
        scratch_shapes=[pltpu.VMEM((rows, d), F32)] * 5 + [pltpu.VMEM((rows, d_ff), BF16)],
        compiler_params=pltpu.CompilerParams(
            dimension_semantics=("arbitrary",), vmem_limit_bytes=VMEM_LIMIT),
        name="sample_layer0",
    )(x, ada, kv_ada, ada, h0, cst, *[c[0] for c in consts])


def _sample_attn_kernel(sink_ref, q_ref, kn_ref, vn_ref, ck_ref, cv_ref,
                        attn_ref, ko_ref, vo_ref):
    sb, nt, d = q_ref.shape
    span = 2 * WINDOW

    tok = lax.broadcasted_iota(jnp.int32, (nt, span), 0)
    si = lax.broadcasted_iota(jnp.int32, (nt, span), 1)
    mask = (si >= tok) & (si <= tok + WINDOW)
    pad = jnp.zeros((HEAD_DIM, LANES - nt), F32)
    ones = jnp.ones((LANES, span), BF16)

    sinks = [[sink_ref[0, g * GROUP + i] * LOG2_E for i in range(GROUP)] for g in range(N_KV_HEADS)]

    def windows(b):
        knt, vnt = kn_ref[b].T, vn_ref[b].T
        kwin, vaug = [], []
        for g in range(N_KV_HEADS):
            rows = slice(g * HEAD_DIM, (g + 1) * HEAD_DIM)
            tops = []
            for c_ref, nt_new, o_ref in ((ck_ref, knt[rows, :], ko_ref), (cv_ref, vnt[rows, :], vo_ref)):
                cache = c_ref[b, g]
                o_ref[b, g] = pltpu.roll(cache, WINDOW - nt, 1)
                o_ref[b, g, :, WINDOW - nt:WINDOW] = nt_new
                tops.append(jnp.concatenate([cache, nt_new, pad], axis=1))
            kwin.append(jnp.concatenate([tops[0], tops[0]], axis=0).astype(BF16))
            vaug.append(jnp.concatenate([tops[1].astype(BF16), tops[1].astype(BF16), ones], axis=0))
        return kwin, vaug

    def sequences(it, carry):
        seqs = [it * SAMPLE_ATTN_UNROLL + u for u in range(SAMPLE_ATTN_UNROLL)]
        wins = [windows(b) for b in seqs]
        units = [(u, g) for u in range(SAMPLE_ATTN_UNROLL) for g in range(N_KV_HEADS)]
        q_split = [[_split_halves(q_ref[b][:, p * LANES:(p + 1) * LANES]) for p in range(d // LANES)]
                   for b in seqs]
        scores = [_attn_scores([q_split[u][2 * g + i // 2][i % 2] for i in range(GROUP)], wins[u][0][g], True)
                  for u, g in units]
        probs = [_attn_probs(s_all, mask, sinks[g]) for s_all, (u, g) in zip(scores, units)]
        values = [_attn_values(p_all, wins[u][1][g], True) for (p_all, _), (u, g) in zip(probs, units)]
        pairs = [[] for _ in seqs]
        for res_all, (_, maxes), (u, g) in zip(values, probs, units):
            pairs[u] += _attn_finish(res_all, maxes, sinks[g])
        for u, b in enumerate(seqs):
            attn_ref[b] = jnp.concatenate(pairs[u], axis=1)
        return carry

    lax.fori_loop(0, sb // SAMPLE_ATTN_UNROLL, sequences, 0)


def _sample_attn_call(sinks, q, kn, vn, ck, cv):
    nb, nt, d = q.shape
    sb = SAMPLE_ATTN_BATCH
    seq_spec = lambda r, w: pl.BlockSpec((sb, r, w), lambda i: (i, 0, 0))
    cache_spec = pl.BlockSpec((sb,) + ck.shape[1:], lambda i: (i, 0, 0, 0))
    return pl.pallas_call(
        _sample_attn_kernel,
        grid=(nb // sb,),
        in_specs=[pl.BlockSpec(memory_space=pltpu.SMEM),
                  seq_spec(nt, d), seq_spec(nt, KV_DIM), seq_spec(nt, KV_DIM), cache_spec, cache_spec],
        out_specs=[seq_spec(nt, d), cache_spec, cache_spec],
        out_shape=[jax.ShapeDtypeStruct((nb, nt, d), F32),
                   jax.ShapeDtypeStruct(ck.shape, F32),
                   jax.ShapeDtypeStruct(cv.shape, F32)],
        compiler_params=pltpu.CompilerParams(
            dimension_semantics=("arbitrary",), vmem_limit_bytes=VMEM_LIMIT),
        name="sample_attention",
    )(sinks, q, kn, vn, ck, cv)


def _sample_l1_kernel(x_ref, attn_ref, mod_ref, ng_ref, wo_ref, fin_ref, fout_ref, fg_ref,
                      y_ref, act_s):
    nt, sb, d = x_ref.shape
    rows = nt * sb
    mod = lambda i: _tile_rows(mod_ref[:, i * d:(i + 1) * d], nt)
    x = x_ref[...].reshape(rows, d)
    attn = jnp.swapaxes(attn_ref[...], 0, 1).reshape(rows, d).astype(BF16)
    x1 = x + mod(2) * _dot(attn, wo_ref[...])
    h2 = _rms_mod(x1, ng_ref[1:2, :], mod(4), mod(3)).astype(BF16)
    x2 = x1 + mod(5) * _swiglu(h2, fin_ref, fout_ref, act_s)
    y_ref[...] = jnp.swapaxes((_rms(x2) * fg_ref[...]).reshape(nt, sb, d), 0, 1)


def _sample_l1_call(x, attn, ada, consts):
    nt, nb, d = x.shape
    sb = SAMPLE_BATCH_TILE
    d_ff = consts[3][0].shape[1]
    slab_spec = pl.BlockSpec((nt, sb, d), lambda i: (0, i, 0))
    seq_spec = pl.BlockSpec((sb, nt, d), lambda i: (i, 0, 0))
    return pl.pallas_call(
        _sample_l1_kernel,
        grid=(nb // sb,),
        in_specs=[slab_spec, seq_spec, _sample_mod_spec(ada, 1, sb)]
                 + [_resident(*c) for c in consts],
        out_specs=seq_spec,
        out_shape=jax.ShapeDtypeStruct((nb, nt, d), F32),
        scratch_shapes=[pltpu.VMEM((nt * sb, d_ff), BF16)],
        compiler_params=pltpu.CompilerParams(
            dimension_semantics=("arbitrary",), vmem_limit_bytes=VMEM_LIMIT),
        name="sample_layer1",
    )(x, attn, ada, *[c[0] for c in consts])


def _rope_tables(pos):
    half = ROT_DIM // 2
    inv = ROPE_THETA ** (-jnp.arange(0, ROT_DIM, 2, dtype=F32) / ROT_DIM)
    ang = pos.astype(F32)[:, None] * inv[None, :]
    cos, sin = jnp.cos(ang), jnp.sin(ang)
    n = pos.shape[0]
    rest = HEAD_DIM - ROT_DIM
    c = jnp.concatenate([cos, cos, jnp.ones((n, rest), F32)], axis=1)
    s_next = jnp.concatenate([-sin, jnp.zeros((n, half + rest), F32)], axis=1)
    s_prev = jnp.concatenate([jnp.zeros((n, half), F32), sin, jnp.zeros((n, rest), F32)], axis=1)
    reps = LANES // HEAD_DIM
    return jnp.stack([jnp.tile(c, (1, reps)), jnp.tile(s_next, (1, reps)), jnp.tile(s_prev, (1, reps))])


def kernel(x_prompt, x_sample, c_prompt, c_sample, state_conv, state_h, cache_k, cache_v, ada_w, ada_b, norm_g, rnn_w_in, rnn_conv_w, rnn_conv_b, rnn_gate_w, rnn_gate_b, rnn_lambda, rnn_w_out, kv_ada_w, kv_ada_b, kv_norm_g, w_kv, attn_w_q, attn_sinks, attn_w_o, ffn_w_in, ffn_w_out, final_g):
    nb_p, t_p, d = x_prompt.shape
    nb_s, t_s, _ = x_sample.shape

    assert rnn_w_in.shape[0] == 1 and attn_w_q.shape[0] == 1 and nb_s % SUBLANES == 0

    c_all = jnp.concatenate([c_sample, c_prompt], axis=0)
    ada = _ada_call(c_all, ada_w, ada_b)
    kv_ada = _ada_call(c_all, kv_ada_w[None], kv_ada_b[None])
    prompt_row_block = nb_s // SUBLANES

    bf = lambda w: w.astype(BF16)
    row = lambda v: v.reshape(1, -1)
    ffn_in, ffn_out = bf(ffn_w_in), bf(ffn_w_out)
    wq, wo = (bf(attn_w_q), 0), (bf(attn_w_o), 0)
    l0_consts = [(norm_g, 0), (bf(rnn_w_in), 0), (rnn_conv_w, 0), (rnn_conv_b, None), (bf(rnn_gate_w), 0),
                 (rnn_gate_b, 0), (rnn_lambda, None), (bf(rnn_w_out), 0), (ffn_in, 0), (ffn_out, 0),
                 (row(kv_norm_g), None), (bf(w_kv), None)]
    l1_ffn = [(ffn_in, 1), (ffn_out, 1), (row(final_g), None)]

    rope_p = _rope_tables(jnp.arange(t_p, dtype=jnp.int32))
    x2_p, k_p, v_p, conv_p, hl_p = _prompt_l0_call(x_prompt, ada, kv_ada, prompt_row_block, rope_p, l0_consts)
    y_prompt = _prompt_l1_call(attn_sinks, x2_p, k_p, v_p, ada, prompt_row_block, rope_p,
                               [(norm_g, 1), wq, wo] + l1_ffn)

    rope_s = _rope_tables(PAST_LEN + jnp.arange(t_s, dtype=jnp.int32))
    x2_s, k_s, v_s, q_s, conv_s, hl_s = _sample_l0_call(
        x_sample, ada, kv_ada, state_h[0], state_conv[0].transpose(1, 0, 2),
        [(rope_s, None)] + l0_consts + [(norm_g, 1), wq])
    attn_s, ko_s, vo_s = _sample_attn_call(
        attn_sinks, q_s, k_s, v_s, cache_k.transpose(0, 2, 3, 1), cache_v.transpose(0, 2, 3, 1))
    y_s = _sample_l1_call(x2_s, attn_s, ada, [(norm_g, 1), wo] + l1_ffn)

    kv_shape = (WINDOW, N_KV_HEADS, HEAD_DIM)
    return (y_prompt,
            y_s,
            conv_p[None, :, SUBLANES - (CONV_WIDTH - 1):, :],
            hl_p[None, :, 0, :],
            k_p[:, t_p - WINDOW:, :].reshape((nb_p,) + kv_shape),
            v_p[:, t_p - WINDOW:, :].reshape((nb_p,) + kv_shape),
            conv_s.transpose(1, 0, 2)[None],
            hl_s[None],
            ko_s.transpose(0, 3, 1, 2),
            vo_s.transpose(0, 3, 1, 2))
```

```python
import functools

import jax
import jax.numpy as jnp
from jax import lax
from jax.experimental import pallas as pl
from jax.experimental.pallas import tpu as pltpu

F32 = jnp.float32
BF16 = jnp.bfloat16

D_MODEL = 1024
N_RNN_BLOCKS = 8
RNN_BLOCK = D_MODEL // N_RNN_BLOCKS
CONV_WIDTH = 4
RG_C = 8.0
HEAD_DIM = 64
N_HEADS = D_MODEL // HEAD_DIM
N_KV_HEADS = 4
GROUP = N_HEADS // N_KV_HEADS
KV_DIM = N_KV_HEADS * HEAD_DIM
WINDOW = 128
ROT_DIM = HEAD_DIM // 4
ROPE_THETA = 500000.0
EPS = 1e-6
NEG_INF = -1e30
LOG2_E = 1.4426950408889634
QUERY_SCALE = HEAD_DIM ** -0.5 * LOG2_E
GELU_C = 0.7978845608028654
GELU_K = 0.044715
PAST_LEN = 16384

LANES = 128
SUBLANES = 8
MXU_COLS = 256
WEIGHT_PITCH_COLS = 1024
ROPE_ROWS_ALIGNED = 32
VMEM_LIMIT = 56 * 1024 * 1024

PROMPT_TILE = 256
FFN_SPLIT = 3
L1_FFN_SPLITS = (3, 7)
ADA_TILE_N = 2048
SAMPLE_ATTN_BATCH = 16
SAMPLE_ATTN_UNROLL = 4
SAMPLE_BATCH_TILE = 32
SAMPLE_L1_BATCH_TILE = 64


def _dot(a, b):
    return jnp.dot(a, b, preferred_element_type=F32)


def _dot_nt(a, b):
    return lax.dot_general(a, b, (((1,), (1,)), ((), ())), preferred_element_type=F32)


def _sigmoid(x):
    return 1.0 / (1.0 + jnp.exp2(x * (-LOG2_E)))


def _silu(x):
    return x * _sigmoid(x)


def _gelu_tanh(x):
    half = 0.5 * x
    return half + half * jnp.tanh(x * (GELU_C + (GELU_C * GELU_K) * (x * x)))


def _log_sigmoid(x):
    return -(jnp.maximum(-x, 0.0) + jnp.log1p(jnp.exp(-jnp.abs(x))))


def _rms(x):
    return x * lax.rsqrt(jnp.mean(x * x, axis=-1, keepdims=True) + EPS)


def _rms_mod(x, g, scale, shift):
    return _rms(x) * (g * (1.0 + scale)) + shift


def _tile_rows(m, reps):
    return jnp.concatenate([m] * reps, axis=0)


def _rope_block(blk, c, s_next, s_prev):
    return blk * c + pltpu.roll(blk, LANES - ROT_DIM // 2, 1) * s_next + pltpu.roll(blk, ROT_DIM // 2, 1) * s_prev


def _rope(x, c, s_next, s_prev):
    blocks = [_rope_block(x[:, j * LANES:(j + 1) * LANES], c, s_next, s_prev)
              for j in range(x.shape[1] // LANES)]
    return jnp.concatenate(blocks, axis=1)


def _proj_in(h, w_in_ref, xr_s, yg_s):
    d = xr_s.shape[1]
    cw = 2 * MXU_COLS
    for c in range(d // cw):
        xr_s[:, c * cw:(c + 1) * cw] = _dot(h, w_in_ref[:, c * cw:(c + 1) * cw])
        yg_s[:, c * cw:(c + 1) * cw] = _dot(h, w_in_ref[:, d + c * cw:d + (c + 1) * cw])


ROW_NORM = (0, 11)
ROW_CONV_W = 2
ROW_CONV_B = 6
ROW_LAMBDA = 7
ROW_KV_NORM = 8
ROW_GATE_B = 9
ROW_FINAL_NORM = 13
SMALL_ROWS = 16


def _small_rows(sp_ref, row, n=1):
    return sp_ref.at[row:row + n]


def _gate_bias(sp_ref, n):
    per_row = D_MODEL // (2 * RNN_BLOCK)
    col = (n % per_row) * 2 * RNN_BLOCK
    return sp_ref[ROW_GATE_B + n // per_row:ROW_GATE_B + n // per_row + 1, col:col + 2 * RNN_BLOCK]


def _rglru_gates(xc, gw_ref, sp_ref, a_s, u_s):
    xcb = xc.astype(BF16)
    cl = RG_C * _log_sigmoid(sp_ref[ROW_LAMBDA:ROW_LAMBDA + 1, :])
    for n in range(N_RNN_BLOCKS):
        blk = slice(n * RNN_BLOCK, (n + 1) * RNN_BLOCK)
        g = _dot(xcb[:, blk], gw_ref[n]) + _gate_bias(sp_ref, n)
        r = _sigmoid(g[:, :RNN_BLOCK])
        i = _sigmoid(g[:, RNN_BLOCK:])
        log_a = cl[:, blk] * r
        a = jnp.exp(log_a)
        a_s[:, blk] = a
        u_s[:, blk] = jnp.sqrt(-jnp.tanh(log_a) * (a * a + 1.0)) * (i * xc[:, blk])


def _swiglu_in(h, fin_ref, act_s, chunks):
    d_ff = act_s.shape[1]
    for c in chunks:
        cols = slice(c * MXU_COLS, (c + 1) * MXU_COLS)
        gate = _dot(h, fin_ref[:, cols])
        up = _dot(h, fin_ref[:, d_ff + c * MXU_COLS:d_ff + (c + 1) * MXU_COLS])
        act_s[:, cols] = (_silu(gate) * up).astype(BF16)


def _swiglu(h, fin_ref, fout_ref, act_s):
    _swiglu_in(h, fin_ref, act_s, range(fout_ref.shape[0] // MXU_COLS))
    return _dot(act_s[...], fout_ref[:, :D_MODEL])


def _low_half(shape):
    return lax.broadcasted_iota(jnp.int32, shape, 1) < LANES // 2


def _dup_halves(blk):
    low = _low_half(blk.shape)
    rot = pltpu.roll(blk, LANES // 2, 1)
    return jnp.where(low, blk, rot), jnp.where(low, rot, blk)


def _split_halves(blk):
    low = _low_half(blk.shape)
    zero = jnp.zeros_like(blk)
    return jnp.where(low, blk, zero), jnp.where(low, zero, blk)


def _attn_scores(q_heads, kwin, keys_on_lanes=False):
    q_all = jnp.concatenate(q_heads, axis=0).astype(BF16)
    return _dot(q_all, kwin) if keys_on_lanes else _dot_nt(q_all, kwin)


def _attn_probs(s_all, mask, sinks):
    rb = s_all.shape[0] // GROUP
    probs, maxes = [], []
    for i in range(GROUP):
        s = jnp.where(mask, s_all[i * rb:(i + 1) * rb, :], NEG_INF)
        mx = jnp.maximum(jnp.max(s, axis=-1, keepdims=True), sinks[i])
        probs.append(jnp.exp2(s - mx))
        maxes.append(mx)
    return jnp.concatenate(probs, axis=0).astype(BF16), maxes


def _attn_values(p_all, vaug, keys_on_lanes=False):
    return _dot_nt(p_all, vaug) if keys_on_lanes else _dot(p_all, vaug)


def _attn_finish(res_all, maxes, sinks):
    rb = res_all.shape[0] // GROUP
    outs = []
    for i in range(GROUP):
        res = res_all[i * rb:(i + 1) * rb, :]
        den = res[:, LANES:] + jnp.exp2(sinks[i] - maxes[i])
        outs.append(res[:, :LANES] * (1.0 / den))
    low = _low_half(outs[0].shape)
    return [jnp.where(low, outs[2 * i], outs[2 * i + 1]) for i in range(GROUP // 2)]


def _ada_kernel(c_ref, w_ref, b_ref, o_ref):
    c = c_ref[...]
    o_ref[0] = _dot(_silu(c).astype(BF16), w_ref[0].astype(BF16)) + b_ref[0]


def _ada_call(c, w, b):
    n_layers, d, n = w.shape
    r = c.shape[0]
    return pl.pallas_call(
        _ada_kernel,
        grid=(n_layers, n // ADA_TILE_N),
        in_specs=[pl.BlockSpec((r, d), lambda l, j: (0, 0)),
                  pl.BlockSpec((1, d, ADA_TILE_N), lambda l, j: (l, 0, j)),
                  pl.BlockSpec((1, 1, ADA_TILE_N), lambda l, j: (l, 0, j))],
        out_specs=pl.BlockSpec((1, r, ADA_TILE_N), lambda l, j: (l, 0, j)),
        out_shape=jax.ShapeDtypeStruct((n_layers, r, n), F32),
        compiler_params=pltpu.CompilerParams(
            dimension_semantics=("arbitrary", "arbitrary"), vmem_limit_bytes=VMEM_LIMIT),
        name="ada_mod",
    )(c, w, b.reshape(n_layers, 1, n))


def _prompt_l0_kernel(x_ref, mod_ref, kvmod_ref, rope_ref, sp_ref, w_in_ref, gw_ref, w_out_ref, fin_ref,
                      fout_ref, wkv_ref,
                      x2_ref, k_ref, v_ref, conv_ref, hl_ref,
                      xr_s, yg_s, act_s, hist_s, hc_s, z_s, xk_s, x2k_s, scan_s, *, nt, n_tiles):
    tm, d = x_ref.shape
    s = pl.program_id(0)
    n_chunks = fout_ref.shape[0] // MXU_COLS
    ng_ref = _small_rows(sp_ref, ROW_NORM[0], 2)
    cw_ref = _small_rows(sp_ref, ROW_CONV_W, CONV_WIDTH)
    cb_ref = _small_rows(sp_ref, ROW_CONV_B)
    lam_ref = _small_rows(sp_ref, ROW_LAMBDA)
    kvg_ref = _small_rows(sp_ref, ROW_KV_NORM)
    b_cur = jnp.minimum(s, n_tiles - 1) // nt
    b_prev = jnp.clip(s - 1, 0, n_tiles - 1) // nt
    b_pp = jnp.maximum(s - 2, 0) // nt
    mod = lambda i: _mod_row(mod_ref, b_cur, i)
    modp = lambda i: _mod_row(mod_ref, b_prev, i)

    @pl.when(s == 0)
    def _():
        z_s[...] = jnp.zeros_like(z_s)
        xk_s[...] = jnp.zeros_like(xk_s)
        x2k_s[...] = jnp.zeros_like(x2k_s)

    @pl.when(lax.rem(jnp.minimum(s, n_tiles - 1), nt) == 0)
    def _():
        hist_s[...] = jnp.zeros_like(hist_s)
        hc_s[...] = jnp.zeros_like(hc_s)

    out_prev = _dot(z_s[...], w_out_ref[:, :D_MODEL])
    x_prev = xk_s[...]
    x2_pp = x2k_s[...]

    hk = _rms_mod(x2_pp, kvg_ref[...], _mod_row(kvmod_ref, b_pp, 1), _mod_row(kvmod_ref, b_pp, 0)).astype(BF16)
    kv = _dot(hk, wkv_ref[...])
    k_ref[...] = _rope(kv[:, :KV_DIM], rope_ref[0], rope_ref[1], rope_ref[2])
    v_ref[...] = kv[:, KV_DIM:]

    x = x_ref[...]
    h = _rms_mod(x, ng_ref[0:1, :], mod(1), mod(0)).astype(BF16)
    _proj_in(h, w_in_ref, xr_s, yg_s)
    xk_s[...] = x

    x1 = x_prev + modp(2) * out_prev
    h2 = _rms_mod(x1, ng_ref[1:2, :], modp(4), modp(3)).astype(BF16)
    _swiglu_in(h2, fin_ref, act_s, range(0, FFN_SPLIT))

    xr = xr_s[...]
    hist = hist_s[...]
    row8 = lax.broadcasted_iota(jnp.int32, (SUBLANES, d), 0)

    def shifted(k):
        rolled = pltpu.roll(xr, k, 0)
        first = jnp.where(row8 >= k, rolled[0:SUBLANES], pltpu.roll(hist, k, 0))
        return jnp.concatenate([first, rolled[SUBLANES:]], axis=0)

    xc = cb_ref[...]
    for j in range(CONV_WIDTH - 1):
        xc = xc + cw_ref[j:j + 1, :] * shifted(CONV_WIDTH - 1 - j)
    xc = xc + cw_ref[CONV_WIDTH - 1:CONV_WIDTH, :] * xr
    hist_s[...] = xr[tm - SUBLANES:]
    xr_s[...] = xc

    ncl = -RG_C * _log_sigmoid(lam_ref[...])
    row8b = lax.broadcasted_iota(jnp.int32, (SUBLANES, RNN_BLOCK), 0)
    chunk = tm // SUBLANES
    pitch = chunk + 4

    def recurrent_block(n):
        blk = slice(n * RNN_BLOCK, (n + 1) * RNN_BLOCK)
        a_s, u_s, h_s, p_s = (scan_s.at[n % 2, j] for j in range(4))
        xc_blk = xr_s[:, blk]
        gates = _dot(xc_blk.astype(BF16), gw_ref[n]) + _gate_bias(sp_ref, n)
        r = _sigmoid(gates[:, :RNN_BLOCK])
        i = _sigmoid(gates[:, RNN_BLOCK:])
        neg_log_a = ncl[:, blk] * r
        a_all = jnp.exp2(neg_log_a * (-LOG2_E))
        w = jnp.tanh(neg_log_a) * (a_all * a_all + 1.0)
        u_all = jnp.where(w == 0.0, 0.0, w * lax.rsqrt(w)) * (i * xc_blk)
        for c in range(SUBLANES):
            a_s[c * pitch:c * pitch + chunk, :] = a_all[c * chunk:(c + 1) * chunk, :]
            u_s[c * pitch:c * pitch + chunk, :] = u_all[c * chunk:(c + 1) * chunk, :]
        h = jnp.zeros((SUBLANES, RNN_BLOCK), F32)
        p = jnp.ones((SUBLANES, RNN_BLOCK), F32)
        for g in range(chunk):
            step_rows = pl.ds(g, SUBLANES, stride=pitch)
            a = a_s[step_rows, :]
            h = a * h + u_s[step_rows, :]
            p = a * p
            h_s[step_rows, :] = h
            p_s[step_rows, :] = p
        for step in (1, 2, 4):
            keep = row8b >= step
            h = p * jnp.where(keep, pltpu.roll(h, step, 0), 0.0) + h
            p = p * jnp.where(keep, pltpu.roll(p, step, 0), 1.0)
        hprev = hc_s[:, blk]
        ends = h + p * hprev
        incoming = jnp.where(row8b >= 1, pltpu.roll(ends, 1, 0), hprev)
        hc_s[:, blk] = jnp.broadcast_to(ends[SUBLANES - 1:SUBLANES, :], ends.shape)
        for c in range(SUBLANES):
            rows = slice(c * chunk, (c + 1) * chunk)
            local = slice(c * pitch, c * pitch + chunk)
            hs = h_s[local, :] + p_s[local, :] * incoming[c:c + 1, :]
            z_s[rows, blk] = (hs * _gelu_tanh(yg_s[rows, blk])).astype(BF16)

    blocks = list(range(N_RNN_BLOCKS))
    for c in range(FFN_SPLIT, n_chunks):
        _swiglu_in(h2, fin_ref, act_s, [c])
        if blocks:
            recurrent_block(blocks.pop(0))
    ffn = _dot(act_s[...], fout_ref[:, :D_MODEL])
    for n in blocks:
        recurrent_block(n)

    x2 = jnp.where(s <= n_tiles, x1 + modp(5) * ffn, x2_pp)
    x2_ref[...] = x2
    x2k_s[...] = x2

    @pl.when(s < n_tiles)
    def _():
        conv_ref[...] = hist_s[...]
        hl_ref[...] = hc_s[...]


def _resident(arr, lead=None):
    if lead is None:
        index, shape = (0,) * arr.ndim, arr.shape
    else:
        index, shape = (lead,) + (0,) * (arr.ndim - 1), (None,) + arr.shape[1:]
    return pl.BlockSpec(shape, lambda *_: index, pipeline_mode=pl.Buffered(1))


def _prompt_mod_specs(ada, kv_ada, layer, row_block):
    return [_resident_rows(ada, layer, row_block), _resident_rows(kv_ada, 0, row_block)]


def _resident_rows(arr, lead, row_block):
    index = (lead, row_block, 0)
    return pl.BlockSpec((None, SUBLANES, arr.shape[2]), lambda *_: index, pipeline_mode=pl.Buffered(1))


def _mod_row(ref, b, i):
    return ref[pl.ds(b, 1), i * D_MODEL:(i + 1) * D_MODEL]


def _prompt_l0_call(x, ada, kv_ada, mod_row_block, rope, consts):
    nb, t, d = x.shape
    tm = PROMPT_TILE
    nt = t // tm
    n_tiles = nb * nt
    d_ff = consts[5][0].shape[1]
    cur = lambda s: jnp.minimum(s, n_tiles - 1)
    prv = lambda s: jnp.clip(s - 1, 0, n_tiles - 1)
    pp = lambda s: jnp.maximum(s - 2, 0)
    tile_spec = lambda w, f: pl.BlockSpec((None, tm, w), lambda s: (f(s) // nt, lax.rem(f(s), nt), 0))
    state_spec = pl.BlockSpec((None, SUBLANES, d), lambda s: (cur(s) // nt, 0, 0))
    return pl.pallas_call(
        functools.partial(_prompt_l0_kernel, nt=nt, n_tiles=n_tiles),
        grid=(n_tiles + 2,),
        in_specs=[tile_spec(d, cur)] + _prompt_mod_specs(ada, kv_ada, 0, mod_row_block)
                 + [pl.BlockSpec((3, tm, LANES), lambda s: (0, lax.rem(pp(s), nt), 0))]
                 + [_resident(*c) for c in consts],
        out_specs=[tile_spec(d, prv), tile_spec(KV_DIM, pp), tile_spec(KV_DIM, pp),
                   state_spec, state_spec],
        out_shape=[jax.ShapeDtypeStruct((nb, t, d), F32),
                   jax.ShapeDtypeStruct((nb, t, KV_DIM), F32),
                   jax.ShapeDtypeStruct((nb, t, KV_DIM), F32),
                   jax.ShapeDtypeStruct((nb, SUBLANES, d), F32),
                   jax.ShapeDtypeStruct((nb, SUBLANES, d), F32)],
        scratch_shapes=[pltpu.VMEM((tm, d), F32), pltpu.VMEM((tm, d), F32),
                        pltpu.VMEM((tm, d_ff), BF16),
                        pltpu.VMEM((SUBLANES, d), F32), pltpu.VMEM((SUBLANES, d), F32),
                        pltpu.VMEM((tm, d), BF16), pltpu.VMEM((tm, d), F32), pltpu.VMEM((tm, d), F32),
                        pltpu.VMEM((2, 4, tm + SUBLANES * SUBLANES, RNN_BLOCK), F32)],
        compiler_params=pltpu.CompilerParams(
            dimension_semantics=("arbitrary",), vmem_limit_bytes=VMEM_LIMIT),
        name="prompt_layer0",
    )(x, ada, kv_ada, rope, *[c[0] for c in consts])


def _prompt_l1_kernel(sink_ref, x_ref, k_ref, v_ref, mod_ref, rope_ref, sp_ref, wq_ref, wo_ref,
                      fin_ref, fout_ref,
                      y_ref,
                      kw_s, vw_s, attn_s, act_s, xk_s, *, nt, n_tiles):
    tm, d = x_ref.shape
    s = pl.program_id(0)
    n_chunks = fout_ref.shape[0] // MXU_COLS
    ng_ref = _small_rows(sp_ref, ROW_NORM[1], 2)
    fg_ref = _small_rows(sp_ref, ROW_FINAL_NORM)
    cur_tile = jnp.minimum(s, n_tiles - 1)
    t = lax.rem(cur_tile, nt)
    b_cur = cur_tile // nt
    b_prev = jnp.maximum(s - 1, 0) // nt
    mod = lambda i: _mod_row(mod_ref, b_cur, i)
    modp = lambda i: _mod_row(mod_ref, b_prev, i)

    @pl.when(s == 0)
    def _():
        attn_s[...] = jnp.zeros_like(attn_s)
        xk_s[...] = jnp.zeros_like(xk_s)

    @pl.when(t == 0)
    def _():
        kw_s[0:WINDOW, :] = jnp.zeros((WINDOW, kw_s.shape[1]), BF16)
        vw_s[0:WINDOW, :] = jnp.zeros((WINDOW, vw_s.shape[1]), BF16)

    @pl.when(t > 0)
    def _():
        kw_s[0:WINDOW, :] = kw_s[tm:tm + WINDOW, :]
        vw_s[0:WINDOW, :] = vw_s[tm:tm + WINDOW, :]

    out_prev = _dot(attn_s[...], wo_ref[:, :D_MODEL])
    x_prev = xk_s[...]

    x = x_ref[...]
    h = _rms_mod(x, ng_ref[0:1, :], mod(1), mod(0)).astype(BF16)
    q = _rope(_dot(h, wq_ref[:, :D_MODEL]), rope_ref[0], rope_ref[1], rope_ref[2]) * QUERY_SCALE
    q_split = [_split_halves(q[:, p * LANES:(p + 1) * LANES]) for p in range(d // LANES)]
    xk_s[...] = x

    new_rows = slice(WINDOW, WINDOW + tm)
    k, v = k_ref[...], v_ref[...]
    ones = jnp.ones((tm, LANES), BF16)
    for pb in range(KV_DIM // LANES):
        k_dup = _dup_halves(k[:, pb * LANES:(pb + 1) * LANES])
        v_dup = _dup_halves(v[:, pb * LANES:(pb + 1) * LANES])
        for i in range(2):
            g = 2 * pb + i
            kw_s[new_rows, g * LANES:(g + 1) * LANES] = k_dup[i].astype(BF16)
            vw_s[new_rows, 2 * g * LANES:(2 * g + 1) * LANES] = v_dup[i].astype(BF16)
            vw_s[new_rows, (2 * g + 1) * LANES:(2 * g + 2) * LANES] = ones

    x1 = x_prev + modp(2) * out_prev
    h2 = _rms_mod(x1, ng_ref[1:2, :], modp(4), modp(3)).astype(BF16)
    _swiglu_in(h2, fin_ref, act_s, range(0, L1_FFN_SPLITS[0]))

    span = 2 * WINDOW
    qi = lax.broadcasted_iota(jnp.int32, (WINDOW, span), 0)
    si = lax.broadcasted_iota(jnp.int32, (WINDOW, span), 1)
    band = (si >= qi) & (si <= qi + WINDOW)
    masks = [band & (si >= WINDOW - (t * tm + j * WINDOW)) for j in range(tm // WINDOW)]
    units = [(j, g) for j in range(tm // WINDOW) for g in range(N_KV_HEADS)]
    rows = lambda j: slice(j * WINDOW, (j + 1) * WINDOW)
    win = lambda j: slice(j * WINDOW, j * WINDOW + span)
    sinks = [[sink_ref[0, g * GROUP + i] * LOG2_E for i in range(GROUP)] for g in range(N_KV_HEADS)]
    scores = [_attn_scores([q_split[2 * g + i // 2][i % 2][rows(j), :] for i in range(GROUP)],
                           kw_s[win(j), g * LANES:(g + 1) * LANES]) for j, g in units]
    _swiglu_in(h2, fin_ref, act_s, range(L1_FFN_SPLITS[0], L1_FFN_SPLITS[1]))
    probs = [_attn_probs(s_all, masks[j], sinks[g]) for s_all, (j, g) in zip(scores, units)]
    values = [_attn_values(p_all, vw_s[win(j), 2 * g * LANES:(2 * g + 2) * LANES])
              for (p_all, _), (j, g) in zip(probs, units)]
    _swiglu_in(h2, fin_ref, act_s, range(L1_FFN_SPLITS[1], n_chunks))
    for res_all, (_, maxes), (j, g) in zip(values, probs, units):
        for i, pair in enumerate(_attn_finish(res_all, maxes, sinks[g])):
            col = (2 * g + i) * LANES
            attn_s[rows(j), col:col + LANES] = pair.astype(BF16)

    x2 = x1 + modp(5) * _dot(act_s[...], fout_ref[:, :D_MODEL])
    y_ref[...] = _rms(x2) * fg_ref[...]


def _prompt_l1_call(sinks, x, k, v, ada, mod_row_block, rope, consts):
    nb, t, d = x.shape
    tm = PROMPT_TILE
    nt = t // tm
    n_tiles = nb * nt
    d_ff = consts[4][0].shape[1]
    cur = lambda s: jnp.minimum(s, n_tiles - 1)
    prv = lambda s: jnp.maximum(s - 1, 0)
    tile_spec = lambda w, f: pl.BlockSpec((None, tm, w), lambda s: (f(s) // nt, lax.rem(f(s), nt), 0))
    return pl.pallas_call(
        functools.partial(_prompt_l1_kernel, nt=nt, n_tiles=n_tiles),
        grid=(n_tiles + 1,),
        in_specs=[pl.BlockSpec(memory_space=pltpu.SMEM),
                  tile_spec(d, cur), tile_spec(KV_DIM, cur), tile_spec(KV_DIM, cur),
                  _resident_rows(ada, 1, mod_row_block),
                  pl.BlockSpec((3, tm, LANES), lambda s: (0, lax.rem(cur(s), nt), 0))]
                 + [_resident(*c) for c in consts],
        out_specs=tile_spec(d, prv),
        out_shape=jax.ShapeDtypeStruct((nb, t, d), F32),
        scratch_shapes=[pltpu.VMEM((WINDOW + tm, N_KV_HEADS * LANES), BF16),
                        pltpu.VMEM((WINDOW + tm, N_KV_HEADS * 2 * LANES), BF16),
                        pltpu.VMEM((tm, d), BF16),
                        pltpu.VMEM((tm, d_ff), BF16),
                        pltpu.VMEM((tm, d), F32)],
        compiler_params=pltpu.CompilerParams(
            dimension_semantics=("arbitrary",), vmem_limit_bytes=VMEM_LIMIT),
        name="prompt_layer1",
    )(sinks, x, k, v, ada, rope, *[c[0] for c in consts])


def _sample_l0_kernel(x_ref, mod_ref, kvmod_ref, qmod_ref, h0_ref, cst_ref, rope_ref, sp_ref, w_in_ref,
                      gw_ref, w_out_ref, fin_ref, fout_ref, wkv_ref, wq_ref,
                      x2_ref, k_ref, v_ref, q_ref, conv_ref, hl_ref,
                      xr_s, yg_s, a_s, u_s, o_s, act_s):
    sb, nt, d = x_ref.shape
    ng_ref = _small_rows(sp_ref, ROW_NORM[0], 2)
    cw_ref = _small_rows(sp_ref, ROW_CONV_W, CONV_WIDTH)
    cb_ref = _small_rows(sp_ref, ROW_CONV_B)
    kvg_ref = _small_rows(sp_ref, ROW_KV_NORM)
    qg_ref = _small_rows(sp_ref, ROW_NORM[1], 2)
    rows = nt * sb
    slab = lambda t: slice(t * sb, (t + 1) * sb)
    vec = lambda ref, i: _tile_rows(ref[:, i * d:(i + 1) * d], nt)
    mod = lambda i: vec(mod_ref, i)
    seq_major = lambda a: jnp.swapaxes(a.reshape(nt, sb, a.shape[-1]), 0, 1)

    x = jnp.swapaxes(x_ref[...], 0, 1).reshape(rows, d)
    h = _rms_mod(x, ng_ref[0:1, :], mod(1), mod(0)).astype(BF16)
    _proj_in(h, w_in_ref, xr_s, yg_s)

    def conv_in(j):
        return cst_ref[j] if j < CONV_WIDTH - 1 else xr_s[slab(j - (CONV_WIDTH - 1)), :]

    xc_slabs = []
    for t in range(nt):
        acc = cb_ref[...]
        for j in range(CONV_WIDTH):
            acc = acc + cw_ref[j:j + 1, :] * conv_in(t + j)
        xc_slabs.append(acc)
    xc = jnp.concatenate(xc_slabs, axis=0)
    for j in range(CONV_WIDTH - 1):
        conv_ref[j] = xr_s[slab(nt - (CONV_WIDTH - 1) + j), :]

    _rglru_gates(xc, gw_ref, sp_ref, a_s, u_s)

    hs = h0_ref[...]
    for t in range(nt):
        hs = a_s[slab(t), :] * hs + u_s[slab(t), :]
        o_s[slab(t), :] = hs
    hl_ref[...] = hs

    z = (o_s[...] * _gelu_tanh(yg_s[...])).astype(BF16)
    x1 = x + mod(2) * _dot(z, w_out_ref[:, :D_MODEL])

    h2 = _rms_mod(x1, ng_ref[1:2, :], mod(4), mod(3)).astype(BF16)
    x2 = x1 + mod(5) * _swiglu(h2, fin_ref, fout_ref, act_s)
    x2_ref[...] = x2.reshape(nt, sb, d)

    hk = _rms_mod(x2, kvg_ref[...], vec(kvmod_ref, 1), vec(kvmod_ref, 0)).astype(BF16)
    kv = _dot(hk, wkv_ref[...])
    hq = _rms_mod(x2, qg_ref[0:1, :], vec(qmod_ref, 1), vec(qmod_ref, 0)).astype(BF16)
    q = _dot(hq, wq_ref[:, :D_MODEL])
    k_slabs, q_slabs = [], []
    for t in range(nt):
        c, s_next, s_prev = rope_ref[0, t:t + 1, :], rope_ref[1, t:t + 1, :], rope_ref[2, t:t + 1, :]
        k_slabs.append(_rope(kv[slab(t), :KV_DIM], c, s_next, s_prev))
        q_slabs.append(_rope(q[slab(t), :], c, s_next, s_prev) * QUERY_SCALE)
    k_ref[...] = seq_major(jnp.concatenate(k_slabs, axis=0))
    q_ref[...] = seq_major(jnp.concatenate(q_slabs, axis=0))
    v_ref[...] = seq_major(kv[:, KV_DIM:])


def _sample_mod_spec(arr, lead, sb):
    return pl.BlockSpec((None, sb, arr.shape[2]), lambda i: (lead, i, 0))


def _sample_l0_call(x, ada, kv_ada, h0, cst, consts):
    nb, nt, d = x.shape
    sb = SAMPLE_BATCH_TILE
    d_ff = consts[6][0].shape[1]
    rows = nt * sb
    slab_spec = lambda lead, w: pl.BlockSpec((lead, sb, w), lambda i: (0, i, 0))
    seq_spec = lambda w: pl.BlockSpec((sb, nt, w), lambda i: (i, 0, 0))
    return pl.pallas_call(
        _sample_l0_kernel,
        grid=(nb // sb,),
        in_specs=[seq_spec(d), _sample_mod_spec(ada, 0, sb), _sample_mod_spec(kv_ada, 0, sb),
                  _sample_mod_spec(ada, 1, sb),
                  pl.BlockSpec((sb, d), lambda i: (i, 0)), slab_spec(CONV_WIDTH - 1, d)]
                 + [_resident(*c) for c in consts],
        out_specs=[slab_spec(nt, d), seq_spec(KV_DIM), seq_spec(KV_DIM), seq_spec(d),
                   slab_spec(CONV_WIDTH - 1, d), pl.BlockSpec((sb, d), lambda i: (i, 0))],
        out_shape=[jax.ShapeDtypeStruct((nt, nb, d), F32),
                   jax.ShapeDtypeStruct((nb, nt, KV_DIM), F32),
                   jax.ShapeDtypeStruct((nb, nt, KV_DIM), F32),
                   jax.ShapeDtypeStruct((nb, nt, d), F32),
                   jax.ShapeDtypeStruct((CONV_WIDTH - 1, nb, d), F32),
                   jax.ShapeDtypeStruct((nb, d), F32)],
        scratch_shapes=[pltpu.VMEM((rows, d), F32)] * 5 + [pltpu.VMEM((rows, d_ff), BF16)],
        compiler_params=pltpu.CompilerParams(
            dimension_semantics=("arbitrary",), vmem_limit_bytes=VMEM_LIMIT),
        name="sample_layer0",
    )(x, ada, kv_ada, ada, h0, cst, *[c[0] for c in consts])


def _sample_attn_kernel(sink_ref, q_ref, kn_ref, vn_ref, ck_ref, cv_ref,
                        attn_ref, ko_ref, vo_ref):
    sb, nt, d = q_ref.shape
    span = 2 * WINDOW

    tok = lax.broadcasted_iota(jnp.int32, (nt, span), 0)
    si = lax.broadcasted_iota(jnp.int32, (nt, span), 1)
    mask = (si >= tok) & (si <= tok + WINDOW)
    pad = jnp.zeros((HEAD_DIM, LANES - nt), F32)
    ones = jnp.ones((LANES, span), BF16)

    sinks = [[sink_ref[0, g * GROUP + i] * LOG2_E for i in range(GROUP)] for g in range(N_KV_HEADS)]

    def windows(b):
        knt, vnt = kn_ref[b].T, vn_ref[b].T
        kwin, vaug = [], []
        for g in range(N_KV_HEADS):
            rows = slice(g * HEAD_DIM, (g + 1) * HEAD_DIM)
            tops = []
            for c_ref, nt_new, o_ref in ((ck_ref, knt[rows, :], ko_ref), (cv_ref, vnt[rows, :], vo_ref)):
                cache = c_ref[b, g]
                o_ref[b, g] = pltpu.roll(cache, WINDOW - nt, 1)
                o_ref[b, g, :, WINDOW - nt:WINDOW] = nt_new
                tops.append(jnp.concatenate([cache, nt_new, pad], axis=1))
            kwin.append(jnp.concatenate([tops[0], tops[0]], axis=0).astype(BF16))
            vaug.append(jnp.concatenate([tops[1].astype(BF16), tops[1].astype(BF16), ones], axis=0))
        return kwin, vaug

    def sequences(it, carry):
        seqs = [it * SAMPLE_ATTN_UNROLL + u for u in range(SAMPLE_ATTN_UNROLL)]
        wins = [windows(b) for b in seqs]
        units = [(u, g) for u in range(SAMPLE_ATTN_UNROLL) for g in range(N_KV_HEADS)]
        q_split = [[_split_halves(q_ref[b][:, p * LANES:(p + 1) * LANES]) for p in range(d // LANES)]
                   for b in seqs]
        scores = [_attn_scores([q_split[u][2 * g + i // 2][i % 2] for i in range(GROUP)], wins[u][0][g], True)
                  for u, g in units]
        probs = [_attn_probs(s_all, mask, sinks[g]) for s_all, (u, g) in zip(scores, units)]
        values = [_attn_values(p_all, wins[u][1][g], True) for (p_all, _), (u, g) in zip(probs, units)]
        pairs = [[] for _ in seqs]
        for res_all, (_, maxes), (u, g) in zip(values, probs, units):
            pairs[u] += _attn_finish(res_all, maxes, sinks[g])
        for u, b in enumerate(seqs):
            attn_ref[b] = jnp.concatenate(pairs[u], axis=1)
        return carry

    lax.fori_loop(0, sb // SAMPLE_ATTN_UNROLL, sequences, 0)


def _sample_attn_call(sinks, q, kn, vn, ck, cv):
    nb, nt, d = q.shape
    sb = SAMPLE_ATTN_BATCH
    seq_spec = lambda r, w: pl.BlockSpec((sb, r, w), lambda i: (i, 0, 0))
    cache_spec = pl.BlockSpec((sb,) + ck.shape[1:], lambda i: (i, 0, 0, 0))
    return pl.pallas_call(
        _sample_attn_kernel,
        grid=(nb // sb,),
        in_specs=[pl.BlockSpec(memory_space=pltpu.SMEM),
                  seq_spec(nt, d), seq_spec(nt, KV_DIM), seq_spec(nt, KV_DIM), cache_spec, cache_spec],
        out_specs=[seq_spec(nt, d), cache_spec, cache_spec],
        out_shape=[jax.ShapeDtypeStruct((nb, nt, d), F32),
                   jax.ShapeDtypeStruct(ck.shape, F32),
                   jax.ShapeDtypeStruct(cv.shape, F32)],
        compiler_params=pltpu.CompilerParams(
            dimension_semantics=("arbitrary",), vmem_limit_bytes=VMEM_LIMIT),
        name="sample_attention",
    )(sinks, q, kn, vn, ck, cv)


def _sample_l1_kernel(x_ref, attn_ref, mod_ref, sp_ref, wo_ref, fin_ref, fout_ref,
                      y_ref, act_s):
    nt, sb, d = x_ref.shape
    ng_ref = _small_rows(sp_ref, ROW_NORM[1], 2)
    fg_ref = _small_rows(sp_ref, ROW_FINAL_NORM)
    rows = nt * sb
    mod = lambda i: _tile_rows(mod_ref[:, i * d:(i + 1) * d], nt)
    x = x_ref[...].reshape(rows, d)
    attn = jnp.swapaxes(attn_ref[...], 0, 1).reshape(rows, d).astype(BF16)
    x1 = x + mod(2) * _dot(attn, wo_ref[:, :D_MODEL])
    h2 = _rms_mod(x1, ng_ref[1:2, :], mod(4), mod(3)).astype(BF16)
    x2 = x1 + mod(5) * _swiglu(h2, fin_ref, fout_ref, act_s)
    y_ref[...] = jnp.swapaxes((_rms(x2) * fg_ref[...]).reshape(nt, sb, d), 0, 1)


def _sample_l1_call(x, attn, ada, consts):
    nt, nb, d = x.shape
    sb = SAMPLE_L1_BATCH_TILE
    d_ff = consts[3][0].shape[1]
    slab_spec = pl.BlockSpec((nt, sb, d), lambda i: (0, i, 0))
    seq_spec = pl.BlockSpec((sb, nt, d), lambda i: (i, 0, 0))
    return pl.pallas_call(
        _sample_l1_kernel,
        grid=(nb // sb,),
        in_specs=[slab_spec, seq_spec, _sample_mod_spec(ada, 1, sb)]
                 + [_resident(*c) for c in consts],
        out_specs=seq_spec,
        out_shape=jax.ShapeDtypeStruct((nb, nt, d), F32),
        scratch_shapes=[pltpu.VMEM((nt * sb, d_ff), BF16)],
        compiler_params=pltpu.CompilerParams(
            dimension_semantics=("arbitrary",), vmem_limit_bytes=VMEM_LIMIT),
        name="sample_layer1",
    )(x, attn, ada, *[c[0] for c in consts])


def _rope_tables(pos):
    half = ROT_DIM // 2
    inv = ROPE_THETA ** (-jnp.arange(0, ROT_DIM, 2, dtype=F32) / ROT_DIM)
    ang = pos.astype(F32)[:, None] * inv[None, :]
    cos, sin = jnp.cos(ang), jnp.sin(ang)
    n = pos.shape[0]
    rest = HEAD_DIM - ROT_DIM
    c = jnp.concatenate([cos, cos, jnp.ones((n, rest), F32)], axis=1)
    s_next = jnp.concatenate([-sin, jnp.zeros((n, half + rest), F32)], axis=1)
    s_prev = jnp.concatenate([jnp.zeros((n, half), F32), sin, jnp.zeros((n, rest), F32)], axis=1)
    reps = LANES // HEAD_DIM
    return jnp.stack([jnp.tile(c, (1, reps)), jnp.tile(s_next, (1, reps)), jnp.tile(s_prev, (1, reps))])


def kernel(x_prompt, x_sample, c_prompt, c_sample, state_conv, state_h, cache_k, cache_v, ada_w, ada_b, norm_g, rnn_w_in, rnn_conv_w, rnn_conv_b, rnn_gate_w, rnn_gate_b, rnn_lambda, rnn_w_out, kv_ada_w, kv_ada_b, kv_norm_g, w_kv, attn_w_q, attn_sinks, attn_w_o, ffn_w_in, ffn_w_out, final_g):
    nb_p, t_p, d = x_prompt.shape
    nb_s, t_s, _ = x_sample.shape

    assert rnn_w_in.shape[0] == 1 and attn_w_q.shape[0] == 1 and nb_s % SUBLANES == 0

    c_all = jnp.concatenate([c_sample, c_prompt], axis=0)
    ada = _ada_call(c_all, ada_w, ada_b)
    kv_ada = _ada_call(c_all, kv_ada_w[None], kv_ada_b[None])
    prompt_row_block = nb_s // SUBLANES

    bf = lambda w: w.astype(BF16)
    row = lambda v: v.reshape(1, -1)

    def bf_padded(w):
        assert w.shape[-1] % WEIGHT_PITCH_COLS == 0
        return jnp.pad(bf(w), ((0, 0),) * (w.ndim - 1) + ((0, LANES),))

    ffn_in, ffn_out = bf(ffn_w_in), bf_padded(ffn_w_out)
    wq, wo = (bf_padded(attn_w_q), 0), (bf_padded(attn_w_o), 0)
    small = jnp.concatenate(
        [norm_g[0], rnn_conv_w[0], rnn_conv_b, rnn_lambda, row(kv_norm_g), rnn_gate_b[0].reshape(-1, d),
         norm_g[1], row(final_g)], axis=0)
    small = (jnp.pad(small, ((0, SMALL_ROWS - small.shape[0]), (0, 0))), None)
    l0_consts = [small, (bf_padded(rnn_w_in), 0), (bf(rnn_gate_w), 0), (bf_padded(rnn_w_out), 0),
                 (ffn_in, 0), (ffn_out, 0), (bf(w_kv), None)]
    l1_ffn = [(ffn_in, 1), (ffn_out, 1)]

    rope_p = _rope_tables(jnp.arange(t_p, dtype=jnp.int32))
    x2_p, k_p, v_p, conv_p, hl_p = _prompt_l0_call(x_prompt, ada, kv_ada, prompt_row_block, rope_p, l0_consts)
    y_prompt = _prompt_l1_call(attn_sinks, x2_p, k_p, v_p, ada, prompt_row_block, rope_p,
                               [small, wq, wo] + l1_ffn)

    rope_s = _rope_tables(PAST_LEN + jnp.arange(t_s, dtype=jnp.int32))
    rope_s = jnp.pad(rope_s, ((0, 0), (0, ROPE_ROWS_ALIGNED - t_s), (0, 0)))
    x2_s, k_s, v_s, q_s, conv_s, hl_s = _sample_l0_call(
        x_sample, ada, kv_ada, state_h[0], state_conv[0].transpose(1, 0, 2),
        [(rope_s, None)] + l0_consts + [wq])
    attn_s, ko_s, vo_s = _sample_attn_call(
        attn_sinks, q_s, k_s, v_s, cache_k.transpose(0, 2, 3, 1), cache_v.transpose(0, 2, 3, 1))
    y_s = _sample_l1_call(x2_s, attn_s, ada, [small, wo] + l1_ffn)

    kv_shape = (WINDOW, N_KV_HEADS, HEAD_DIM)
    return (y_prompt,
            y_s,
            conv_p[None, :, SUBLANES - (CONV_WIDTH - 1):, :],
            hl_p[None, :, 0, :],
            k_p[:, t_p - WINDOW:, :].reshape((nb_p,) + kv_shape),
            v_p[:, t_p - WINDOW:, :].reshape((nb_p,) + kv_shape),
            conv_s.transpose(1, 0, 2)[None],
            hl_s[None],
            ko_s.transpose(0, 3, 1, 2),
            vo_s.transpose(0, 3, 1, 2))
```

```python
import functools

import jax
import jax.numpy as jnp
from jax import lax
from jax.experimental import pallas as pl
from jax.experimental.pallas import tpu as pltpu

F32 = jnp.float32
BF16 = jnp.bfloat16

D_MODEL = 1024
N_RNN_BLOCKS = 8
RNN_BLOCK = D_MODEL // N_RNN_BLOCKS
CONV_WIDTH = 4
RG_C = 8.0
HEAD_DIM = 64
N_HEADS = D_MODEL // HEAD_DIM
N_KV_HEADS = 4
GROUP = N_HEADS // N_KV_HEADS
KV_DIM = N_KV_HEADS * HEAD_DIM
WINDOW = 128
ROT_DIM = HEAD_DIM // 4
ROPE_THETA = 500000.0
EPS = 1e-6
NEG_INF = -1e30
LOG2_E = 1.4426950408889634
QUERY_SCALE = HEAD_DIM ** -0.5 * LOG2_E
GELU_C = 0.7978845608028654
GELU_K = 0.044715
PAST_LEN = 16384

LANES = 128
SUBLANES = 8
MXU_COLS = 256
WEIGHT_PITCH_COLS = 1024
VMEM_LIMIT = 56 * 1024 * 1024

PROMPT_TILE = 256
FFN_SPLIT = 3
L1_FFN_SPLITS = (3, 7)
ADA_TILE_N = 1024
ADA_BUFFERS = 4
SAMPLE_ATTN_BATCH = 16
SAMPLE_ATTN_UNROLL = 4
SAMPLE_BATCH_TILE = 32
SAMPLE_L1_BATCH_TILE = 64


def _dot(a, b):
    return jnp.dot(a, b, preferred_element_type=F32)


def _dot_nt(a, b):
    return lax.dot_general(a, b, (((1,), (1,)), ((), ())), preferred_element_type=F32)


def _sigmoid(x):
    return 1.0 / (1.0 + jnp.exp2(x * (-LOG2_E)))


def _silu(x):
    return x * _sigmoid(x)


def _gelu_tanh(x):
    half = 0.5 * x
    return half + half * jnp.tanh(x * (GELU_C + (GELU_C * GELU_K) * (x * x)))


def _log_sigmoid(x):
    return -(jnp.maximum(-x, 0.0) + jnp.log1p(jnp.exp(-jnp.abs(x))))


def _rms(x):
    return x * lax.rsqrt(jnp.mean(x * x, axis=-1, keepdims=True) + EPS)


def _rms_mod(x, g, scale, shift):
    return _rms(x) * (g * (1.0 + scale)) + shift


def _tile_rows(m, reps):
    return jnp.concatenate([m] * reps, axis=0)


def _rope_block(blk, c, s_next, s_prev):
    return blk * c + pltpu.roll(blk, LANES - ROT_DIM // 2, 1) * s_next + pltpu.roll(blk, ROT_DIM // 2, 1) * s_prev


def _rope(x, c, s_next, s_prev):
    blocks = [_rope_block(x[:, j * LANES:(j + 1) * LANES], c, s_next, s_prev)
              for j in range(x.shape[1] // LANES)]
    return jnp.concatenate(blocks, axis=1)


def _proj_in(h, w_in_ref, xr_s, yg_s):
    d = xr_s.shape[1]
    cw = 2 * MXU_COLS
    for c in range(d // cw):
        xr_s[:, c * cw:(c + 1) * cw] = _dot(h, w_in_ref[:, c * cw:(c + 1) * cw])
        yg_s[:, c * cw:(c + 1) * cw] = _dot(h, w_in_ref[:, d + c * cw:d + (c + 1) * cw])


def _rglru_gates(xc, gw_ref, gb_ref, lam_ref, a_s, u_s):
    xcb = xc.astype(BF16)
    cl = RG_C * _log_sigmoid(lam_ref[...])
    for n in range(N_RNN_BLOCKS):
        blk = slice(n * RNN_BLOCK, (n + 1) * RNN_BLOCK)
        g = _dot(xcb[:, blk], gw_ref[n]) + gb_ref[n:n + 1, :]
        r = _sigmoid(g[:, :RNN_BLOCK])
        i = _sigmoid(g[:, RNN_BLOCK:])
        log_a = cl[:, blk] * r
        a = jnp.exp(log_a)
        a_s[:, blk] = a
        u_s[:, blk] = jnp.sqrt(-jnp.tanh(log_a) * (a * a + 1.0)) * (i * xc[:, blk])


def _swiglu_in(h, fin_ref, act_s, chunks):
    d_ff = act_s.shape[1]
    for c in chunks:
        cols = slice(c * MXU_COLS, (c + 1) * MXU_COLS)
        gate = _dot(h, fin_ref[:, cols])
        up = _dot(h, fin_ref[:, d_ff + c * MXU_COLS:d_ff + (c + 1) * MXU_COLS])
        act_s[:, cols] = (_silu(gate) * up).astype(BF16)


def _swiglu(h, fin_ref, fout_ref, act_s):
    _swiglu_in(h, fin_ref, act_s, range(fout_ref.shape[0] // MXU_COLS))
    return _dot(act_s[...], fout_ref[:, :D_MODEL])


def _low_half(shape):
    return lax.broadcasted_iota(jnp.int32, shape, 1) < LANES // 2


def _dup_halves(blk):
    low = _low_half(blk.shape)
    rot = pltpu.roll(blk, LANES // 2, 1)
    return jnp.where(low, blk, rot), jnp.where(low, rot, blk)


def _split_halves(blk):
    low = _low_half(blk.shape)
    zero = jnp.zeros_like(blk)
    return jnp.where(low, blk, zero), jnp.where(low, zero, blk)


def _attn_scores(q_heads, kwin, keys_on_lanes=False):
    q_all = jnp.concatenate(q_heads, axis=0).astype(BF16)
    return _dot(q_all, kwin) if keys_on_lanes else _dot_nt(q_all, kwin)


def _attn_probs(s_all, mask, sinks):
    rb = s_all.shape[0] // GROUP
    probs, maxes = [], []
    for i in range(GROUP):
        s = jnp.where(mask, s_all[i * rb:(i + 1) * rb, :], NEG_INF)
        mx = jnp.maximum(jnp.max(s, axis=-1, keepdims=True), sinks[i])
        probs.append(jnp.exp2(s - mx))
        maxes.append(mx)
    return jnp.concatenate(probs, axis=0).astype(BF16), maxes


def _attn_values(p_all, vaug, keys_on_lanes=False):
    return _dot_nt(p_all, vaug) if keys_on_lanes else _dot(p_all, vaug)


def _attn_finish(res_all, maxes, sinks):
    rb = res_all.shape[0] // GROUP
    outs = []
    for i in range(GROUP):
        res = res_all[i * rb:(i + 1) * rb, :]
        den = res[:, LANES:] + jnp.exp2(sinks[i] - maxes[i])
        outs.append(res[:, :LANES] * (1.0 / den))
    low = _low_half(outs[0].shape)
    return [jnp.where(low, outs[2 * i], outs[2 * i + 1]) for i in range(GROUP // 2)]


def _ada_kernel(c_ref, b_ref, w_hbm, o_ref, wbuf, sems, *, tiles):
    s = _silu(c_ref[...]).astype(BF16)

    def tile_copy(t):
        layer, j = tiles[t]
        slot = t % ADA_BUFFERS
        return pltpu.make_async_copy(w_hbm.at[layer, :, pl.ds(j * ADA_TILE_N, ADA_TILE_N)],
                                     wbuf.at[slot], sems.at[slot])

    for t in range(min(ADA_BUFFERS, len(tiles))):
        tile_copy(t).start()
    for t, (layer, j) in enumerate(tiles):
        tile_copy(t).wait()
        cols = slice(j * ADA_TILE_N, (j + 1) * ADA_TILE_N)
        o_ref[layer, :, cols] = _dot(s, wbuf[t % ADA_BUFFERS].astype(BF16)) + b_ref[layer, :, cols]
        if t + ADA_BUFFERS < len(tiles):
            tile_copy(t + ADA_BUFFERS).start()


def _ada_call(c, w, b):
    n_layers, d, n = w.shape
    r = c.shape[0]
    tiles = tuple((layer, j) for layer in range(n_layers) for j in range(n // ADA_TILE_N))
    vmem = pl.BlockSpec(memory_space=pltpu.VMEM)
    return pl.pallas_call(
        functools.partial(_ada_kernel, tiles=tiles),
        in_specs=[vmem, vmem, pl.BlockSpec(memory_space=pl.ANY)],
        out_specs=vmem,
        out_shape=jax.ShapeDtypeStruct((n_layers, r, n), F32),
        scratch_shapes=[pltpu.VMEM((ADA_BUFFERS, d, ADA_TILE_N), F32),
                        pltpu.SemaphoreType.DMA((ADA_BUFFERS,))],
        compiler_params=pltpu.CompilerParams(vmem_limit_bytes=VMEM_LIMIT),
        name="ada_mod",
    )(c, b.reshape(n_layers, 1, n), w)


def _prompt_l0_kernel(x_ref, mod_ref, kvmod_ref, rope_ref, ng_ref, w_in_ref, cw_ref, cb_ref, gw_ref,
                      gb_ref, lam_ref, w_out_ref, fin_ref, fout_ref, kvg_ref, wkv_ref,
                      x2_ref, k_ref, v_ref, conv_ref, hl_ref,
                      xr_s, yg_s, act_s, hist_s, hc_s, z_s, xk_s, x2k_s, scan_s, *, nt, n_tiles):
    tm, d = x_ref.shape
    s = pl.program_id(0)
    n_chunks = fout_ref.shape[0] // MXU_COLS
    b_cur = jnp.minimum(s, n_tiles - 1) // nt
    b_prev = jnp.clip(s - 1, 0, n_tiles - 1) // nt
    b_pp = jnp.maximum(s - 2, 0) // nt
    mod = lambda i: _mod_row(mod_ref, b_cur, i)
    modp = lambda i: _mod_row(mod_ref, b_prev, i)

    @pl.when(s == 0)
    def _():
        z_s[...] = jnp.zeros_like(z_s)
        xk_s[...] = jnp.zeros_like(xk_s)
        x2k_s[...] = jnp.zeros_like(x2k_s)

    @pl.when(lax.rem(jnp.minimum(s, n_tiles - 1), nt) == 0)
    def _():
        hist_s[...] = jnp.zeros_like(hist_s)
        hc_s[...] = jnp.zeros_like(hc_s)

    out_prev = _dot(z_s[...], w_out_ref[:, :D_MODEL])
    x_prev = xk_s[...]
    x2_pp = x2k_s[...]

    hk = _rms_mod(x2_pp, kvg_ref[...], _mod_row(kvmod_ref, b_pp, 1), _mod_row(kvmod_ref, b_pp, 0)).astype(BF16)
    kv = _dot(hk, wkv_ref[...])
    k_ref[...] = _rope(kv[:, :KV_DIM], rope_ref[0], rope_ref[1], rope_ref[2])
    v_ref[...] = kv[:, KV_DIM:]

    x = x_ref[...]
    h = _rms_mod(x, ng_ref[0:1, :], mod(1), mod(0)).astype(BF16)
    _proj_in(h, w_in_ref, xr_s, yg_s)
    xk_s[...] = x

    x1 = x_prev + modp(2) * out_prev
    h2 = _rms_mod(x1, ng_ref[1:2, :], modp(4), modp(3)).astype(BF16)
    _swiglu_in(h2, fin_ref, act_s, range(0, FFN_SPLIT))

    xr = xr_s[...]
    hist = hist_s[...]
    row8 = lax.broadcasted_iota(jnp.int32, (SUBLANES, d), 0)

    def shifted(k):
        rolled = pltpu.roll(xr, k, 0)
        first = jnp.where(row8 >= k, rolled[0:SUBLANES], pltpu.roll(hist, k, 0))
        return jnp.concatenate([first, rolled[SUBLANES:]], axis=0)

    xc = cb_ref[...]
    for j in range(CONV_WIDTH - 1):
        xc = xc + cw_ref[j:j + 1, :] * shifted(CONV_WIDTH - 1 - j)
    xc = xc + cw_ref[CONV_WIDTH - 1:CONV_WIDTH, :] * xr
    hist_s[...] = xr[tm - SUBLANES:]
    xr_s[...] = xc

    ncl = -RG_C * _log_sigmoid(lam_ref[...])
    row8b = lax.broadcasted_iota(jnp.int32, (SUBLANES, RNN_BLOCK), 0)
    chunk = tm // SUBLANES
    pitch = chunk + 4

    def recurrent_block(n):
        blk = slice(n * RNN_BLOCK, (n + 1) * RNN_BLOCK)
        a_s, u_s, h_s, p_s = (scan_s.at[n % 2, j] for j in range(4))
        xc_blk = xr_s[:, blk]
        gates = _dot(xc_blk.astype(BF16), gw_ref[n]) + gb_ref[n:n + 1, :]
        r = _sigmoid(gates[:, :RNN_BLOCK])
        i = _sigmoid(gates[:, RNN_BLOCK:])
        neg_log_a = ncl[:, blk] * r
        a_all = jnp.exp2(neg_log_a * (-LOG2_E))
        w = jnp.tanh(neg_log_a) * (a_all * a_all + 1.0)
        u_all = jnp.where(w == 0.0, 0.0, w * lax.rsqrt(w)) * (i * xc_blk)
        for c in range(SUBLANES):
            a_s[c * pitch:c * pitch + chunk, :] = a_all[c * chunk:(c + 1) * chunk, :]
            u_s[c * pitch:c * pitch + chunk, :] = u_all[c * chunk:(c + 1) * chunk, :]
        h = jnp.zeros((SUBLANES, RNN_BLOCK), F32)
        p = jnp.ones((SUBLANES, RNN_BLOCK), F32)
        for g in range(chunk):
            step_rows = pl.ds(g, SUBLANES, stride=pitch)
            a = a_s[step_rows, :]
            h = a * h + u_s[step_rows, :]
            p = a * p
            h_s[step_rows, :] = h
            p_s[step_rows, :] = p
        for step in (1, 2, 4):
            keep = row8b >= step
            h = p * jnp.where(keep, pltpu.roll(h, step, 0), 0.0) + h
            p = p * jnp.where(keep, pltpu.roll(p, step, 0), 1.0)
        hprev = hc_s[:, blk]
        ends = h + p * hprev
        incoming = jnp.where(row8b >= 1, pltpu.roll(ends, 1, 0), hprev)
        hc_s[:, blk] = jnp.broadcast_to(ends[SUBLANES - 1:SUBLANES, :], ends.shape)
        for c in range(SUBLANES):
            rows = slice(c * chunk, (c + 1) * chunk)
            local = slice(c * pitch, c * pitch + chunk)
            hs = h_s[local, :] + p_s[local, :] * incoming[c:c + 1, :]
            z_s[rows, blk] = (hs * _gelu_tanh(yg_s[rows, blk])).astype(BF16)

    blocks = list(range(N_RNN_BLOCKS))
    for c in range(FFN_SPLIT, n_chunks):
        _swiglu_in(h2, fin_ref, act_s, [c])
        if blocks:
            recurrent_block(blocks.pop(0))
    ffn = _dot(act_s[...], fout_ref[:, :D_MODEL])
    for n in blocks:
        recurrent_block(n)

    x2 = jnp.where(s <= n_tiles, x1 + modp(5) * ffn, x2_pp)
    x2_ref[...] = x2
    x2k_s[...] = x2

    @pl.when(s < n_tiles)
    def _():
        conv_ref[...] = hist_s[...]
        hl_ref[...] = hc_s[...]


def _resident(arr, lead=None):
    if lead is None:
        index, shape = (0,) * arr.ndim, arr.shape
    else:
        index, shape = (lead,) + (0,) * (arr.ndim - 1), (None,) + arr.shape[1:]
    return pl.BlockSpec(shape, lambda *_: index, pipeline_mode=pl.Buffered(1))


def _prompt_mod_specs(ada, kv_ada, layer, row_block):
    return [_resident_rows(ada, layer, row_block), _resident_rows(kv_ada, 0, row_block)]


def _resident_rows(arr, lead, row_block):
    index = (lead, row_block, 0)
    return pl.BlockSpec((None, SUBLANES, arr.shape[2]), lambda *_: index, pipeline_mode=pl.Buffered(1))


def _mod_row(ref, b, i):
    return ref[pl.ds(b, 1), i * D_MODEL:(i + 1) * D_MODEL]


def _prompt_l0_call(x, ada, kv_ada, mod_row_block, rope, consts):
    nb, t, d = x.shape
    tm = PROMPT_TILE
    nt = t // tm
    n_tiles = nb * nt
    d_ff = consts[9][0].shape[1]
    cur = lambda s: jnp.minimum(s, n_tiles - 1)
    prv = lambda s: jnp.clip(s - 1, 0, n_tiles - 1)
    pp = lambda s: jnp.maximum(s - 2, 0)
    tile_spec = lambda w, f: pl.BlockSpec((None, tm, w), lambda s: (f(s) // nt, lax.rem(f(s), nt), 0))
    state_spec = pl.BlockSpec((None, SUBLANES, d), lambda s: (cur(s) // nt, 0, 0))
    return pl.pallas_call(
        functools.partial(_prompt_l0_kernel, nt=nt, n_tiles=n_tiles),
        grid=(n_tiles + 2,),
        in_specs=[tile_spec(d, cur)] + _prompt_mod_specs(ada, kv_ada, 0, mod_row_block)
                 + [pl.BlockSpec((3, tm, LANES), lambda s: (0, lax.rem(pp(s), nt), 0))]
                 + [_resident(*c) for c in consts],
        out_specs=[tile_spec(d, prv), tile_spec(KV_DIM, pp), tile_spec(KV_DIM, pp),
                   state_spec, state_spec],
        out_shape=[jax.ShapeDtypeStruct((nb, t, d), F32),
                   jax.ShapeDtypeStruct((nb, t, KV_DIM), F32),
                   jax.ShapeDtypeStruct((nb, t, KV_DIM), F32),
                   jax.ShapeDtypeStruct((nb, SUBLANES, d), F32),
                   jax.ShapeDtypeStruct((nb, SUBLANES, d), F32)],
        scratch_shapes=[pltpu.VMEM((tm, d), F32), pltpu.VMEM((tm, d), F32),
                        pltpu.VMEM((tm, d_ff), BF16),
                        pltpu.VMEM((SUBLANES, d), F32), pltpu.VMEM((SUBLANES, d), F32),
                        pltpu.VMEM((tm, d), BF16), pltpu.VMEM((tm, d), F32), pltpu.VMEM((tm, d), F32),
                        pltpu.VMEM((2, 4, tm + SUBLANES * SUBLANES, RNN_BLOCK), F32)],
        compiler_params=pltpu.CompilerParams(
            dimension_semantics=("arbitrary",), vmem_limit_bytes=VMEM_LIMIT),
        name="prompt_layer0",
    )(x, ada, kv_ada, rope, *[c[0] for c in consts])


def _prompt_l1_kernel(sink_ref, x_ref, k_ref, v_ref, mod_ref, rope_ref, ng_ref, wq_ref, wo_ref,
                      fin_ref, fout_ref, fg_ref,
                      y_ref,
                      kw_s, vw_s, attn_s, act_s, xk_s, *, nt, n_tiles):
    tm, d = x_ref.shape
    s = pl.program_id(0)
    n_chunks = fout_ref.shape[0] // MXU_COLS
    cur_tile = jnp.minimum(s, n_tiles - 1)
    t = lax.rem(cur_tile, nt)
    b_cur = cur_tile // nt
    b_prev = jnp.maximum(s - 1, 0) // nt
    mod = lambda i: _mod_row(mod_ref, b_cur, i)
    modp = lambda i: _mod_row(mod_ref, b_prev, i)

    @pl.when(s == 0)
    def _():
        attn_s[...] = jnp.zeros_like(attn_s)
        xk_s[...] = jnp.zeros_like(xk_s)

    @pl.when(t == 0)
    def _():
        kw_s[0:WINDOW, :] = jnp.zeros((WINDOW, kw_s.shape[1]), BF16)
        vw_s[0:WINDOW, :] = jnp.zeros((WINDOW, vw_s.shape[1]), BF16)

    @pl.when(t > 0)
    def _():
        kw_s[0:WINDOW, :] = kw_s[tm:tm + WINDOW, :]
        vw_s[0:WINDOW, :] = vw_s[tm:tm + WINDOW, :]

    out_prev = _dot(attn_s[...], wo_ref[:, :D_MODEL])
    x_prev = xk_s[...]

    x = x_ref[...]
    h = _rms_mod(x, ng_ref[0:1, :], mod(1), mod(0)).astype(BF16)
    q = _rope(_dot(h, wq_ref[:, :D_MODEL]), rope_ref[0], rope_ref[1], rope_ref[2]) * QUERY_SCALE
    q_split = [_split_halves(q[:, p * LANES:(p + 1) * LANES]) for p in range(d // LANES)]
    xk_s[...] = x

    new_rows = slice(WINDOW, WINDOW + tm)
    k, v = k_ref[...], v_ref[...]
    ones = jnp.ones((tm, LANES), BF16)
    for pb in range(KV_DIM // LANES):
        k_dup = _dup_halves(k[:, pb * LANES:(pb + 1) * LANES])
        v_dup = _dup_halves(v[:, pb * LANES:(pb + 1) * LANES])
        for i in range(2):
            g = 2 * pb + i
            kw_s[new_rows, g * LANES:(g + 1) * LANES] = k_dup[i].astype(BF16)
            vw_s[new_rows, 2 * g * LANES:(2 * g + 1) * LANES] = v_dup[i].astype(BF16)
            vw_s[new_rows, (2 * g + 1) * LANES:(2 * g + 2) * LANES] = ones

    x1 = x_prev + modp(2) * out_prev
    h2 = _rms_mod(x1, ng_ref[1:2, :], modp(4), modp(3)).astype(BF16)
    _swiglu_in(h2, fin_ref, act_s, range(0, L1_FFN_SPLITS[0]))

    span = 2 * WINDOW
    qi = lax.broadcasted_iota(jnp.int32, (WINDOW, span), 0)
    si = lax.broadcasted_iota(jnp.int32, (WINDOW, span), 1)
    band = (si >= qi) & (si <= qi + WINDOW)
    masks = [band & (si >= WINDOW - (t * tm + j * WINDOW)) for j in range(tm // WINDOW)]
    units = [(j, g) for j in range(tm // WINDOW) for g in range(N_KV_HEADS)]
    rows = lambda j: slice(j * WINDOW, (j + 1) * WINDOW)
    win = lambda j: slice(j * WINDOW, j * WINDOW + span)
    sinks = [[sink_ref[0, g * GROUP + i] * LOG2_E for i in range(GROUP)] for g in range(N_KV_HEADS)]
    scores = [_attn_scores([q_split[2 * g + i // 2][i % 2][rows(j), :] for i in range(GROUP)],
                           kw_s[win(j), g * LANES:(g + 1) * LANES]) for j, g in units]
    _swiglu_in(h2, fin_ref, act_s, range(L1_FFN_SPLITS[0], L1_FFN_SPLITS[1]))
    probs = [_attn_probs(s_all, masks[j], sinks[g]) for s_all, (j, g) in zip(scores, units)]
    values = [_attn_values(p_all, vw_s[win(j), 2 * g * LANES:(2 * g + 2) * LANES])
              for (p_all, _), (j, g) in zip(probs, units)]
    _swiglu_in(h2, fin_ref, act_s, range(L1_FFN_SPLITS[1], n_chunks))
    for res_all, (_, maxes), (j, g) in zip(values, probs, units):
        for i, pair in enumerate(_attn_finish(res_all, maxes, sinks[g])):
            col = (2 * g + i) * LANES
            attn_s[rows(j), col:col + LANES] = pair.astype(BF16)

    x2 = x1 + modp(5) * _dot(act_s[...], fout_ref[:, :D_MODEL])
    y_ref[...] = _rms(x2) * fg_ref[...]


def _prompt_l1_call(sinks, x, k, v, ada, mod_row_block, rope, consts):
    nb, t, d = x.shape
    tm = PROMPT_TILE
    nt = t // tm
    n_tiles = nb * nt
    d_ff = consts[4][0].shape[1]
    cur = lambda s: jnp.minimum(s, n_tiles - 1)
    prv = lambda s: jnp.maximum(s - 1, 0)
    tile_spec = lambda w, f: pl.BlockSpec((None, tm, w), lambda s: (f(s) // nt, lax.rem(f(s), nt), 0))
    return pl.pallas_call(
        functools.partial(_prompt_l1_kernel, nt=nt, n_tiles=n_tiles),
        grid=(n_tiles + 1,),
        in_specs=[pl.BlockSpec(memory_space=pltpu.SMEM),
                  tile_spec(d, cur), tile_spec(KV_DIM, cur), tile_spec(KV_DIM, cur),
                  _resident_rows(ada, 1, mod_row_block),
                  pl.BlockSpec((3, tm, LANES), lambda s: (0, lax.rem(cur(s), nt), 0))]
                 + [_resident(*c) for c in consts],
        out_specs=tile_spec(d, prv),
        out_shape=jax.ShapeDtypeStruct((nb, t, d), F32),
        scratch_shapes=[pltpu.VMEM((WINDOW + tm, N_KV_HEADS * LANES), BF16),
                        pltpu.VMEM((WINDOW + tm, N_KV_HEADS * 2 * LANES), BF16),
                        pltpu.VMEM((tm, d), BF16),
                        pltpu.VMEM((tm, d_ff), BF16),
                        pltpu.VMEM((tm, d), F32)],
        compiler_params=pltpu.CompilerParams(
            dimension_semantics=("arbitrary",), vmem_limit_bytes=VMEM_LIMIT),
        name="prompt_layer1",
    )(sinks, x, k, v, ada, rope, *[c[0] for c in consts])


def _sample_l0_kernel(x_ref, mod_ref, kvmod_ref, qmod_ref, h0_ref, cst_ref, rope_ref, ng_ref, w_in_ref,
                      cw_ref, cb_ref, gw_ref, gb_ref, lam_ref, w_out_ref, fin_ref, fout_ref, kvg_ref,
                      wkv_ref, qg_ref, wq_ref,
                      x2_ref, k_ref, v_ref, q_ref, conv_ref, hl_ref,
                      xr_s, yg_s, a_s, u_s, o_s, act_s):
    sb, nt, d = x_ref.shape
    rows = nt * sb
    slab = lambda t: slice(t * sb, (t + 1) * sb)
    vec = lambda ref, i: _tile_rows(ref[:, i * d:(i + 1) * d], nt)
    mod = lambda i: vec(mod_ref, i)
    seq_major = lambda a: jnp.swapaxes(a.reshape(nt, sb, a.shape[-1]), 0, 1)

    x = jnp.swapaxes(x_ref[...], 0, 1).reshape(rows, d)
    h = _rms_mod(x, ng_ref[0:1, :], mod(1), mod(0)).astype(BF16)
    _proj_in(h, w_in_ref, xr_s, yg_s)

    def conv_in(j):
        return cst_ref[j] if j < CONV_WIDTH - 1 else xr_s[slab(j - (CONV_WIDTH - 1)), :]

    xc_slabs = []
    for t in range(nt):
        acc = cb_ref[...]
        for j in range(CONV_WIDTH):
            acc = acc + cw_ref[j:j + 1, :] * conv_in(t + j)
        xc_slabs.append(acc)
    xc = jnp.concatenate(xc_slabs, axis=0)
    for j in range(CONV_WIDTH - 1):
        conv_ref[j] = xr_s[slab(nt - (CONV_WIDTH - 1) + j), :]

    _rglru_gates(xc, gw_ref, gb_ref, lam_ref, a_s, u_s)

    hs = h0_ref[...]
    for t in range(nt):
        hs = a_s[slab(t), :] * hs + u_s[slab(t), :]
        o_s[slab(t), :] = hs
    hl_ref[...] = hs

    z = (o_s[...] * _gelu_tanh(yg_s[...])).astype(BF16)
    x1 = x + mod(2) * _dot(z, w_out_ref[:, :D_MODEL])

    h2 = _rms_mod(x1, ng_ref[1:2, :], mod(4), mod(3)).astype(BF16)
    x2 = x1 + mod(5) * _swiglu(h2, fin_ref, fout_ref, act_s)
    x2_ref[...] = x2.reshape(nt, sb, d)

    hk = _rms_mod(x2, kvg_ref[...], vec(kvmod_ref, 1), vec(kvmod_ref, 0)).astype(BF16)
    kv = _dot(hk, wkv_ref[...])
    hq = _rms_mod(x2, qg_ref[0:1, :], vec(qmod_ref, 1), vec(qmod_ref, 0)).astype(BF16)
    q = _dot(hq, wq_ref[:, :D_MODEL])
    k_slabs, q_slabs = [], []
    for t in range(nt):
        c, s_next, s_prev = rope_ref[0, t:t + 1, :], rope_ref[1, t:t + 1, :], rope_ref[2, t:t + 1, :]
        k_slabs.append(_rope(kv[slab(t), :KV_DIM], c, s_next, s_prev))
        q_slabs.append(_rope(q[slab(t), :], c, s_next, s_prev) * QUERY_SCALE)
    k_ref[...] = seq_major(jnp.concatenate(k_slabs, axis=0))
    q_ref[...] = seq_major(jnp.concatenate(q_slabs, axis=0))
    v_ref[...] = seq_major(kv[:, KV_DIM:])


def _sample_mod_spec(arr, lead, sb):
    return pl.BlockSpec((None, sb, arr.shape[2]), lambda i: (lead, i, 0))


def _sample_l0_call(x, ada, kv_ada, h0, cst, consts):
    nb, nt, d = x.shape
    sb = SAMPLE_BATCH_TILE
    d_ff = consts[10][0].shape[1]
    rows = nt * sb
    slab_spec = lambda lead, w: pl.BlockSpec((lead, sb, w), lambda i: (0, i, 0))
    seq_spec = lambda w: pl.BlockSpec((sb, nt, w), lambda i: (i, 0, 0))
    return pl.pallas_call(
        _sample_l0_kernel,
        grid=(nb // sb,),
        in_specs=[seq_spec(d), _sample_mod_spec(ada, 0, sb), _sample_mod_spec(kv_ada, 0, sb),
                  _sample_mod_spec(ada, 1, sb),
                  pl.BlockSpec((sb, d), lambda i: (i, 0)), slab_spec(CONV_WIDTH - 1, d)]
                 + [_resident(*c) for c in consts],
        out_specs=[slab_spec(nt, d), seq_spec(KV_DIM), seq_spec(KV_DIM), seq_spec(d),
                   slab_spec(CONV_WIDTH - 1, d), pl.BlockSpec((sb, d), lambda i: (i, 0))],
        out_shape=[jax.ShapeDtypeStruct((nt, nb, d), F32),
                   jax.ShapeDtypeStruct((nb, nt, KV_DIM), F32),
                   jax.ShapeDtypeStruct((nb, nt, KV_DIM), F32),
                   jax.ShapeDtypeStruct((nb, nt, d), F32),
                   jax.ShapeDtypeStruct((CONV_WIDTH - 1, nb, d), F32),
                   jax.ShapeDtypeStruct((nb, d), F32)],
        scratch_shapes=[pltpu.VMEM((rows, d), F32)] * 5 + [pltpu.VMEM((rows, d_ff), BF16)],
        compiler_params=pltpu.CompilerParams(
            dimension_semantics=("arbitrary",), vmem_limit_bytes=VMEM_LIMIT),
        name="sample_layer0",
    )(x, ada, kv_ada, ada, h0, cst, *[c[0] for c in consts])


def _sample_attn_kernel(sink_ref, q_ref, kn_ref, vn_ref, ck_ref, cv_ref,
                        attn_ref, ko_ref, vo_ref):
    sb, nt, d = q_ref.shape
    span = 2 * WINDOW

    tok = lax.broadcasted_iota(jnp.int32, (nt, span), 0)
    si = lax.broadcasted_iota(jnp.int32, (nt, span), 1)
    mask = (si >= tok) & (si <= tok + WINDOW)
    pad = jnp.zeros((HEAD_DIM, LANES - nt), F32)
    ones = jnp.ones((LANES, span), BF16)

    sinks = [[sink_ref[0, g * GROUP + i] * LOG2_E for i in range(GROUP)] for g in range(N_KV_HEADS)]

    def windows(b):
        knt, vnt = kn_ref[b].T, vn_ref[b].T
        kwin, vaug = [], []
        for g in range(N_KV_HEADS):
            rows = slice(g * HEAD_DIM, (g + 1) * HEAD_DIM)
            tops = []
            for c_ref, nt_new, o_ref in ((ck_ref, knt[rows, :], ko_ref), (cv_ref, vnt[rows, :], vo_ref)):
                cache = c_ref[b, g]
                o_ref[b, g] = pltpu.roll(cache, WINDOW - nt, 1)
                o_ref[b, g, :, WINDOW - nt:WINDOW] = nt_new
                tops.append(jnp.concatenate([cache, nt_new, pad], axis=1))
            kwin.append(jnp.concatenate([tops[0], tops[0]], axis=0).astype(BF16))
            vaug.append(jnp.concatenate([tops[1].astype(BF16), tops[1].astype(BF16), ones], axis=0))
        return kwin, vaug

    def sequences(it, carry):
        seqs = [it * SAMPLE_ATTN_UNROLL + u for u in range(SAMPLE_ATTN_UNROLL)]
        wins = [windows(b) for b in seqs]
        units = [(u, g) for u in range(SAMPLE_ATTN_UNROLL) for g in range(N_KV_HEADS)]
        q_split = [[_split_halves(q_ref[b][:, p * LANES:(p + 1) * LANES]) for p in range(d // LANES)]
                   for b in seqs]
        scores = [_attn_scores([q_split[u][2 * g + i // 2][i % 2] for i in range(GROUP)], wins[u][0][g], True)
                  for u, g in units]
        probs = [_attn_probs(s_all, mask, sinks[g]) for s_all, (u, g) in zip(scores, units)]
        values = [_attn_values(p_all, wins[u][1][g], True) for (p_all, _), (u, g) in zip(probs, units)]
        pairs = [[] for _ in seqs]
        for res_all, (_, maxes), (u, g) in zip(values, probs, units):
            pairs[u] += _attn_finish(res_all, maxes, sinks[g])
        for u, b in enumerate(seqs):
            attn_ref[b] = jnp.concatenate(pairs[u], axis=1)
        return carry

    lax.fori_loop(0, sb // SAMPLE_ATTN_UNROLL, sequences, 0)


def _sample_attn_call(sinks, q, kn, vn, ck, cv):
    nb, nt, d = q.shape
    sb = SAMPLE_ATTN_BATCH
    seq_spec = lambda r, w: pl.BlockSpec((sb, r, w), lambda i: (i, 0, 0))
    cache_spec = pl.BlockSpec((sb,) + ck.shape[1:], lambda i: (i, 0, 0, 0))
    return pl.pallas_call(
        _sample_attn_kernel,
        grid=(nb // sb,),
        in_specs=[pl.BlockSpec(memory_space=pltpu.SMEM),
                  seq_spec(nt, d), seq_spec(nt, KV_DIM), seq_spec(nt, KV_DIM), cache_spec, cache_spec],
        out_specs=[seq_spec(nt, d), cache_spec, cache_spec],
        out_shape=[jax.ShapeDtypeStruct((nb, nt, d), F32),
                   jax.ShapeDtypeStruct(ck.shape, F32),
                   jax.ShapeDtypeStruct(cv.shape, F32)],
        compiler_params=pltpu.CompilerParams(
            dimension_semantics=("arbitrary",), vmem_limit_bytes=VMEM_LIMIT),
        name="sample_attention",
    )(sinks, q, kn, vn, ck, cv)


def _sample_l1_kernel(x_ref, attn_ref, mod_ref, ng_ref, wo_ref, fin_ref, fout_ref, fg_ref,
                      y_ref, act_s):
    nt, sb, d = x_ref.shape
    rows = nt * sb
    mod = lambda i: _tile_rows(mod_ref[:, i * d:(i + 1) * d], nt)
    x = x_ref[...].reshape(rows, d)
    attn = jnp.swapaxes(attn_ref[...], 0, 1).reshape(rows, d).astype(BF16)
    x1 = x + mod(2) * _dot(attn, wo_ref[:, :D_MODEL])
    h2 = _rms_mod(x1, ng_ref[1:2, :], mod(4), mod(3)).astype(BF16)
    x2 = x1 + mod(5) * _swiglu(h2, fin_ref, fout_ref, act_s)
    y_ref[...] = jnp.swapaxes((_rms(x2) * fg_ref[...]).reshape(nt, sb, d), 0, 1)


def _sample_l1_call(x, attn, ada, consts):
    nt, nb, d = x.shape
    sb = SAMPLE_L1_BATCH_TILE
    d_ff = consts[3][0].shape[1]
    slab_spec = pl.BlockSpec((nt, sb, d), lambda i: (0, i, 0))
    seq_spec = pl.BlockSpec((sb, nt, d), lambda i: (i, 0, 0))
    return pl.pallas_call(
        _sample_l1_kernel,
        grid=(nb // sb,),
        in_specs=[slab_spec, seq_spec, _sample_mod_spec(ada, 1, sb)]
                 + [_resident(*c) for c in consts],
        out_specs=seq_spec,
        out_shape=jax.ShapeDtypeStruct((nb, nt, d), F32),
        scratch_shapes=[pltpu.VMEM((nt * sb, d_ff), BF16)],
        compiler_params=pltpu.CompilerParams(
            dimension_semantics=("arbitrary",), vmem_limit_bytes=VMEM_LIMIT),
        name="sample_layer1",
    )(x, attn, ada, *[c[0] for c in consts])


def _rope_tables(pos):
    half = ROT_DIM // 2
    inv = ROPE_THETA ** (-jnp.arange(0, ROT_DIM, 2, dtype=F32) / ROT_DIM)
    ang = pos.astype(F32)[:, None] * inv[None, :]
    cos, sin = jnp.cos(ang), jnp.sin(ang)
    n = pos.shape[0]
    rest = HEAD_DIM - ROT_DIM
    c = jnp.concatenate([cos, cos, jnp.ones((n, rest), F32)], axis=1)
    s_next = jnp.concatenate([-sin, jnp.zeros((n, half + rest), F32)], axis=1)
    s_prev = jnp.concatenate([jnp.zeros((n, half), F32), sin, jnp.zeros((n, rest), F32)], axis=1)
    reps = LANES // HEAD_DIM
    return jnp.stack([jnp.tile(c, (1, reps)), jnp.tile(s_next, (1, reps)), jnp.tile(s_prev, (1, reps))])


def kernel(x_prompt, x_sample, c_prompt, c_sample, state_conv, state_h, cache_k, cache_v, ada_w, ada_b, norm_g, rnn_w_in, rnn_conv_w, rnn_conv_b, rnn_gate_w, rnn_gate_b, rnn_lambda, rnn_w_out, kv_ada_w, kv_ada_b, kv_norm_g, w_kv, attn_w_q, attn_sinks, attn_w_o, ffn_w_in, ffn_w_out, final_g):
    nb_p, t_p, d = x_prompt.shape
    nb_s, t_s, _ = x_sample.shape

    assert rnn_w_in.shape[0] == 1 and attn_w_q.shape[0] == 1 and nb_s % SUBLANES == 0

    c_all = jnp.concatenate([c_sample, c_prompt], axis=0)
    ada = _ada_call(c_all, ada_w, ada_b)
    kv_ada = _ada_call(c_all, kv_ada_w[None], kv_ada_b[None])
    prompt_row_block = nb_s // SUBLANES

    bf = lambda w: w.astype(BF16)
    row = lambda v: v.reshape(1, -1)

    def bf_padded(w):
        assert w.shape[-1] % WEIGHT_PITCH_COLS == 0
        return jnp.pad(bf(w), ((0, 0),) * (w.ndim - 1) + ((0, LANES),))

    ffn_in, ffn_out = bf(ffn_w_in), bf_padded(ffn_w_out)
    wq, wo = (bf_padded(attn_w_q), 0), (bf_padded(attn_w_o), 0)
    l0_consts = [(norm_g, 0), (bf_padded(rnn_w_in), 0), (rnn_conv_w, 0), (rnn_conv_b, None), (bf(rnn_gate_w), 0),
                 (rnn_gate_b, 0), (rnn_lambda, None), (bf_padded(rnn_w_out), 0), (ffn_in, 0), (ffn_out, 0),
                 (row(kv_norm_g), None), (bf(w_kv), None)]
    l1_ffn = [(ffn_in, 1), (ffn_out, 1), (row(final_g), None)]

    rope_p = _rope_tables(jnp.arange(t_p, dtype=jnp.int32))
    x2_p, k_p, v_p, conv_p, hl_p = _prompt_l0_call(x_prompt, ada, kv_ada, prompt_row_block, rope_p, l0_consts)
    y_prompt = _prompt_l1_call(attn_sinks, x2_p, k_p, v_p, ada, prompt_row_block, rope_p,
                               [(norm_g, 1), wq, wo] + l1_ffn)

    rope_s = _rope_tables(PAST_LEN + jnp.arange(t_s, dtype=jnp.int32))
    x2_s, k_s, v_s, q_s, conv_s, hl_s = _sample_l0_call(
        x_sample, ada, kv_ada, state_h[0], state_conv[0].transpose(1, 0, 2),
        [(rope_s, None)] + l0_consts + [(norm_g, 1), wq])
    attn_s, ko_s, vo_s = _sample_attn_call(
        attn_sinks, q_s, k_s, v_s, cache_k.transpose(0, 2, 3, 1), cache_v.transpose(0, 2, 3, 1))
    y_s = _sample_l1_call(x2_s, attn_s, ada, [(norm_g, 1), wo] + l1_ffn)

    kv_shape = (WINDOW, N_KV_HEADS, HEAD_DIM)
    return (y_prompt,
            y_s,
            conv_p[None, :, SUBLANES - (CONV_WIDTH - 1):, :],
            hl_p[None, :, 0, :],
            k_p[:, t_p - WINDOW:, :].reshape((nb_p,) + kv_shape),
            v_p[:, t_p - WINDOW:, :].reshape((nb_p,) + kv_shape),
            conv_s.transpose(1, 0, 2)[None],
            hl_s[None],
            ko_s.transpose(0, 3, 1, 2),
            vo_s.transpose(0, 3, 1, 2))
```
